```python
import jax
import jax.numpy as jnp
from jax import lax
import numpy as np

D_MODEL = 1024
BATCH = 8
SEQ = 4096
DEPTH = 1

EPS = 1e-6
GLA_HEADS = 4
GLA_DK = 128
GLA_DV = 256
GLA_GATE_RANK = 16
GLA_GATE_NORMALIZER = 16.0
GLA_CHUNK = 64
GLA_QK_WIDTH = GLA_HEADS * GLA_DK
GLA_V_WIDTH = GLA_HEADS * GLA_DV
MLA_HEADS = 8
MLA_Q_RANK = 384
MLA_KV_RANK = 256
MLA_NOPE = 128
MLA_ROPE = 64
MLA_V = 128
MLA_QK = MLA_NOPE + MLA_ROPE
MLA_Q_BLOCK = 128
ROPE_BASE = 10000.0
N_EXPERTS = 32
TOP_K = 4
D_FF = 1024
SWIGLU_LIMIT = 7.0
SWIGLU_ALPHA = 1.702
MOE_SEQ_BLOCK = 128
IN_SPLIT_SIZES = (GLA_QK_WIDTH, GLA_QK_WIDTH, GLA_V_WIDTH, GLA_V_WIDTH, GLA_GATE_RANK,
                  MLA_Q_RANK, MLA_KV_RANK, MLA_ROPE, D_MODEL, D_MODEL)
IN_WIDTH = sum(IN_SPLIT_SIZES)

kernel_name = 'hybrid_gla_mla_moe_block'


def rms_norm(x, w):
    xf = x.astype(jnp.float32)
    xf = xf * lax.rsqrt(jnp.mean(xf * xf, axis=-1, keepdims=True) + EPS)
    return (xf * w.astype(jnp.float32)).astype(x.dtype)


def apply_rope(x, positions):
    half = x.shape[-1] // 2
    inv_freq = ROPE_BASE ** (-jnp.arange(half, dtype=jnp.float32) / half)
    ang = positions.astype(jnp.float32)[:, :, None, None] * inv_freq
    cos, sin = jnp.cos(ang), jnp.sin(ang)
    xf = x.astype(jnp.float32)
    x1, x2 = xf[..., :half], xf[..., half:]
    return jnp.concatenate([x1 * cos - x2 * sin, x2 * cos + x1 * sin], axis=-1).astype(x.dtype)


def gla_chunked(q, k, v, log_a):
    B, S, H, DK = q.shape
    DV = v.shape[-1]
    n_chunks = S // GLA_CHUNK

    def to_chunks(t):
        return t.astype(jnp.float32).reshape(B, n_chunks, GLA_CHUNK, H, t.shape[-1]).transpose(1, 0, 3, 2, 4)

    xs = (to_chunks(q), to_chunks(k), to_chunks(v), to_chunks(log_a))
    causal = jnp.tril(jnp.ones((GLA_CHUNK, GLA_CHUNK), dtype=bool))[:, :, None]

    def step(state, inp):
        qc, kc, vc, gc = inp
        cum_g = jnp.cumsum(gc, axis=2)
        o_inter = jnp.einsum('bhik,bhkv->bhiv', qc * jnp.exp(cum_g), state)
        diff = cum_g[:, :, :, None, :] - cum_g[:, :, None, :, :]
        decay = jnp.exp(jnp.where(causal, diff, -jnp.inf))
        scores = jnp.einsum('bhik,bhjk,bhijk->bhij', qc, kc, decay)
        o_intra = jnp.einsum('bhij,bhjv->bhiv', scores, vc)
        g_last = cum_g[:, :, -1:, :]
        state = (jnp.exp(g_last[:, :, 0, :, None]) * state
                 + jnp.einsum('bhjk,bhjv->bhkv', kc * jnp.exp(g_last - cum_g), vc))
        return state, o_inter + o_intra

    state0 = jnp.zeros((B, H, DK, DV), jnp.float32)
    _, o = lax.scan(step, state0, xs)
    return o.transpose(1, 0, 3, 2, 4).reshape(B, S, H, DV).astype(v.dtype)


def mla_causal_attention(q, k, v):
    B, S, H, DQ = q.shape
    n_blocks = S // MLA_Q_BLOCK
    scale = DQ ** -0.5
    kf = k.astype(jnp.float32)
    vf = v.astype(jnp.float32)
    qb = q.astype(jnp.float32).reshape(B, n_blocks, MLA_Q_BLOCK, H, DQ).transpose(1, 0, 2, 3, 4)
    key_idx = jnp.arange(S)

    def one_block(args):
        qi, blk = args
        s = jnp.einsum('bqhd,bkhd->bhqk', qi, kf) * scale
        q_idx = blk * MLA_Q_BLOCK + jnp.arange(MLA_Q_BLOCK)
        s = jnp.where(key_idx[None, :] <= q_idx[:, None], s, -jnp.inf)
        p = jax.nn.softmax(s, axis=-1)
        return jnp.einsum('bhqk,bkhd->bqhd', p, vf)

    o = lax.map(one_block, (qb, jnp.arange(n_blocks)))
    return o.transpose(1, 0, 2, 3, 4).reshape(B, S, H, v.shape[-1]).astype(v.dtype)


def moe_ffn(x, w_router, b_router, w_gate, b_gate, w_up, b_up, w_down, b_down):
    B, S, D = x.shape
    logits = (jnp.einsum('bsd,de->bse', x, w_router) + b_router).astype(jnp.float32)
    top_vals, top_idx = lax.top_k(logits, TOP_K)
    top_w = jax.nn.softmax(top_vals, axis=-1)
    combine = jnp.einsum('bske,bsk->bse', jax.nn.one_hot(top_idx, N_EXPERTS, dtype=jnp.float32),
                         top_w).astype(x.dtype)
    n_blocks = S // MOE_SEQ_BLOCK

    def split_blocks(t):
        return t.reshape(B, n_blocks, MOE_SEQ_BLOCK, t.shape[-1]).transpose(1, 0, 2, 3)

    def expert_block(args):
        xi, ci = args
        g = jnp.einsum('btd,edf->btef', xi, w_gate) + b_gate
        u = jnp.einsum('btd,edf->btef', xi, w_up) + b_up
        g = jnp.minimum(g, SWIGLU_LIMIT)
        u = jnp.clip(u, -SWIGLU_LIMIT, SWIGLU_LIMIT)
        hidden = (u + 1.0) * g * jax.nn.sigmoid(SWIGLU_ALPHA * g) * ci[..., None]
        return jnp.einsum('btef,efd->btd', hidden, w_down) + jnp.einsum('bte,ed->btd', ci, b_down)

    y = lax.map(expert_block, (split_blocks(x), split_blocks(combine)))
    return y.transpose(1, 0, 2, 3).reshape(B, S, D)


def setup_inputs(seed: int = 0) -> dict:
    key = jax.random.key(seed)
    ks = jax.random.split(key, 25)
    f32 = jnp.float32
    L = DEPTH

    def normal(k, shape, scale):
        return jax.random.normal(k, shape, f32) * scale

    def gain(k, n):
        return 1.0 + 0.01 * jax.random.normal(k, (L, n), f32)

    return {
        'x': normal(ks[0], (BATCH, SEQ, D_MODEL), 1.0),
        'positions': jnp.broadcast_to(jnp.arange(SEQ, dtype=jnp.int32)[None, :], (BATCH, SEQ)),
        'attn_norm_w': gain(ks[1], D_MODEL),
        'w_in': normal(ks[2], (L, D_MODEL, IN_WIDTH), D_MODEL ** -0.5),
        'w_gla_gk': normal(ks[3], (L, GLA_GATE_RANK, GLA_QK_WIDTH), GLA_GATE_RANK ** -0.5),
        'b_gla_gk': normal(ks[4], (L, GLA_QK_WIDTH), 0.1),
        'gla_out_norm_w': gain(ks[5], GLA_DV),
        'w_gla_out': normal(ks[6], (L, GLA_V_WIDTH, D_MODEL), GLA_V_WIDTH ** -0.5),
        'mla_q_norm_w': gain(ks[7], MLA_Q_RANK),
        'w_mla_uq': normal(ks[8], (L, MLA_Q_RANK, MLA_HEADS * MLA_QK), MLA_Q_RANK ** -0.5),
        'mla_kv_norm_w': gain(ks[9], MLA_KV_RANK),
        'w_mla_ukv': normal(ks[10], (L, MLA_KV_RANK, MLA_HEADS * (MLA_NOPE + MLA_V)), MLA_KV_RANK ** -0.5),
        'mla_qk_q_norm_w': gain(ks[11], MLA_QK),
        'mla_qk_k_norm_w': gain(ks[12], MLA_QK),
        'w_mla_out': normal(ks[13], (L, MLA_HEADS * MLA_V, D_MODEL), (MLA_HEADS * MLA_V) ** -0.5),
        'w_out': normal(ks[14], (L, D_MODEL, D_MODEL), D_MODEL ** -0.5),
        'moe_norm_w': gain(ks[15], D_MODEL),
        'w_router': normal(ks[16], (L, D_MODEL, N_EXPERTS), D_MODEL ** -0.5),
        'b_router': normal(ks[17], (L, N_EXPERTS), 0.01),
        'w_exp_gate': normal(ks[18], (L, N_EXPERTS, D_MODEL, D_FF), D_MODEL ** -0.5),
        'b_exp_gate': normal(ks[19], (L, N_EXPERTS, D_FF), 0.01),
        'w_exp_up': normal(ks[20], (L, N_EXPERTS, D_MODEL, D_FF), D_MODEL ** -0.5),
        'b_exp_up': normal(ks[21], (L, N_EXPERTS, D_FF), 0.01),
        'w_exp_down': normal(ks[22], (L, N_EXPERTS, D_FF, D_MODEL), D_FF ** -0.5),
        'b_exp_down': normal(ks[23], (L, N_EXPERTS, D_MODEL), 0.01),
    }


def reference(x, positions, attn_norm_w, w_in, w_gla_gk, b_gla_gk, gla_out_norm_w, w_gla_out,
              mla_q_norm_w, w_mla_uq, mla_kv_norm_w, w_mla_ukv, mla_qk_q_norm_w, mla_qk_k_norm_w,
              w_mla_out, w_out, moe_norm_w, w_router, b_router, w_exp_gate, b_exp_gate,
              w_exp_up, b_exp_up, w_exp_down, b_exp_down):
    B, S, _ = x.shape
    offsets = np.cumsum(IN_SPLIT_SIZES)[:-1].tolist()
    for l in range(DEPTH):
        h = rms_norm(x, attn_norm_w[l])
        proj = jnp.einsum('bsd,de->bse', h, w_in[l])
        (gla_q, gla_k, gla_v, gla_g, gla_lr, mla_cq, mla_ckv, mla_kr,
         gate_a, gate_b) = jnp.split(proj, offsets, axis=-1)

        q_a = gla_q.reshape(B, S, GLA_HEADS, GLA_DK) * (GLA_DK ** -0.5)
        k_a = gla_k.reshape(B, S, GLA_HEADS, GLA_DK)
        v_a = gla_v.reshape(B, S, GLA_HEADS, GLA_DV)
        a_logit = jnp.einsum('bsr,rk->bsk', gla_lr, w_gla_gk[l]) + b_gla_gk[l]
        log_a = (jax.nn.log_sigmoid(a_logit.astype(jnp.float32)) / GLA_GATE_NORMALIZER
                 ).reshape(B, S, GLA_HEADS, GLA_DK)
        o_a = gla_chunked(q_a, k_a, v_a, log_a)
        o_a = rms_norm(o_a, gla_out_norm_w[l]).reshape(B, S, GLA_V_WIDTH) * jax.nn.silu(gla_g)
        y_a = jnp.einsum('bse,ed->bsd', o_a, w_gla_out[l])

        q_b = jnp.einsum('bsr,re->bse', rms_norm(mla_cq, mla_q_norm_w[l]), w_mla_uq[l]
                         ).reshape(B, S, MLA_HEADS, MLA_QK)
        kv_b = jnp.einsum('bsr,re->bse', rms_norm(mla_ckv, mla_kv_norm_w[l]), w_mla_ukv[l]
                          ).reshape(B, S, MLA_HEADS, MLA_NOPE + MLA_V)
        k_nope, v_b = kv_b[..., :MLA_NOPE], kv_b[..., MLA_NOPE:]
        k_rope = jnp.broadcast_to(mla_kr[:, :, None, :], (B, S, MLA_HEADS, MLA_ROPE))
        k_b = jnp.concatenate([k_nope, k_rope], axis=-1)
        q_b = rms_norm(q_b, mla_qk_q_norm_w[l])
        k_b = rms_norm(k_b, mla_qk_k_norm_w[l])
        q_b = jnp.concatenate([q_b[..., :MLA_NOPE], apply_rope(q_b[..., MLA_NOPE:], positions)], axis=-1)
        k_b = jnp.concatenate([k_b[..., :MLA_NOPE], apply_rope(k_b[..., MLA_NOPE:], positions)], axis=-1)
        o_b = mla_causal_attention(q_b, k_b, v_b).reshape(B, S, MLA_HEADS * MLA_V)
        y_b = jnp.einsum('bse,ed->bsd', o_b, w_mla_out[l])

        merged = jax.nn.sigmoid(gate_a) * y_a + jax.nn.sigmoid(gate_b) * y_b
        x = x + jnp.einsum('bsd,de->bse', merged, w_out[l])

        x = x + moe_ffn(rms_norm(x, moe_norm_w[l]), w_router[l], b_router[l],
                        w_exp_gate[l], b_exp_gate[l], w_exp_up[l], b_exp_up[l],
                        w_exp_down[l], b_exp_down[l])
    return x
```

```python
import functools

import jax
import jax.numpy as jnp
import numpy as np
from jax import lax
from jax.experimental import pallas as pl
from jax.experimental.pallas import tpu as pltpu

F32 = jnp.float32
BF16 = jnp.bfloat16
I32 = jnp.int32

D_MODEL = 1024
EPS = 1e-6
GLA_HEADS = 4
GLA_DK = 128
GLA_DV = 256
GLA_GATE_RANK = 16
GLA_GATE_NORMALIZER = 16.0
GLA_QK_WIDTH = GLA_HEADS * GLA_DK
GLA_V_WIDTH = GLA_HEADS * GLA_DV
MLA_HEADS = 8
MLA_Q_RANK = 384
MLA_KV_RANK = 256
MLA_NOPE = 128
MLA_ROPE = 64
MLA_V = 128
MLA_QK = MLA_NOPE + MLA_ROPE
ROPE_BASE = 10000.0
N_EXPERTS = 32
TOP_K = 4
D_FF = 1024
SWIGLU_LIMIT = 7.0
SWIGLU_ALPHA = 1.702

LANES = 128
MLA_HEAD_PAD = 256
GLA_CHUNK = 128
VMEM_LIMIT = 56 * 1024 * 1024

_GLA_W = 2 * GLA_QK_WIDTH + 2 * GLA_V_WIDTH
_LR_W = LANES
_MLA_W = 768
_GATE_W = 2 * D_MODEL
_PACK_W = _GLA_W + _LR_W + _MLA_W + _GATE_W


def _dot(a, b):
    return jnp.dot(a, b, preferred_element_type=F32)


def _dot_nt(a, b):
    return lax.dot_general(a, b, (((1,), (1,)), ((), ())), preferred_element_type=F32)


def _dot_tn(a, b):
    return lax.dot_general(a, b, (((0,), (0,)), ((), ())), preferred_element_type=F32)


def _split_bf16(a):
    hi = a.astype(BF16)
    lo = (a - hi.astype(F32)).astype(BF16)
    return hi, lo


def _params(*sem):
    return pltpu.CompilerParams(dimension_semantics=sem, vmem_limit_bytes=VMEM_LIMIT)


def _inproj_kernel(x_ref, nw_ref, w_ref, wgk_hi_ref, wgk_lo_ref, bgk_ref,
                   gla_ref, la_ref, mla_ref, gate_ref):
    x = x_ref[...]
    ms = jnp.mean(x * x, axis=-1, keepdims=True)
    h = (x * lax.rsqrt(ms + EPS)) * nw_ref[...]
    hb = h.astype(BF16)
    o0 = 0
    q = _dot(hb, w_ref[:, 0:GLA_QK_WIDTH]) * (GLA_DK ** -0.5)
    gla_ref[:, 0:GLA_QK_WIDTH] = q.astype(BF16)
    gla_ref[:, GLA_QK_WIDTH:_GLA_W] = _dot(hb, w_ref[:, GLA_QK_WIDTH:_GLA_W]).astype(BF16)
    o0 = _GLA_W
    lr = _dot(hb, w_ref[:, o0:o0 + _LR_W])
    lr_hi, lr_lo = _split_bf16(lr)
    a_logit = (_dot(lr_hi, wgk_hi_ref[...]) + _dot(lr_lo, wgk_hi_ref[...])
               + _dot(lr_hi, wgk_lo_ref[...])) + bgk_ref[...]
    log_sig = jnp.minimum(a_logit, 0.0) - jnp.log1p(jnp.exp(-jnp.abs(a_logit)))
    la_ref[...] = log_sig / GLA_GATE_NORMALIZER
    o0 += _LR_W
    mla_ref[...] = _dot(hb, w_ref[:, o0:o0 + _MLA_W]).astype(BF16)
    o0 += _MLA_W
    gate_ref[...] = _dot(hb, w_ref[:, o0:o0 + _GATE_W]).astype(BF16)


def _inproj(x2, nw, w_pack, wgk_hi, wgk_lo, bgk, tm):
    t = x2.shape[0]
    const = lambda i: (0, 0)
    row = lambda i: (i, 0)
    return pl.pallas_call(
        _inproj_kernel,
        grid=(t // tm,),
        in_specs=[
            pl.BlockSpec((tm, D_MODEL), row),
            pl.BlockSpec((1, D_MODEL), const),
            pl.BlockSpec((D_MODEL, _PACK_W), const, pipeline_mode=pl.Buffered(1)),
            pl.BlockSpec((LANES, GLA_QK_WIDTH), const),
            pl.BlockSpec((LANES, GLA_QK_WIDTH), const),
            pl.BlockSpec((1, GLA_QK_WIDTH), const),
        ],
        out_specs=[
            pl.BlockSpec((tm, _GLA_W), row),
            pl.BlockSpec((tm, GLA_QK_WIDTH), row),
            pl.BlockSpec((tm, _MLA_W), row),
            pl.BlockSpec((tm, _GATE_W), row),
        ],
        out_shape=[
            jax.ShapeDtypeStruct((t, _GLA_W), BF16),
            jax.ShapeDtypeStruct((t, GLA_QK_WIDTH), F32),
            jax.ShapeDtypeStruct((t, _MLA_W), BF16),
            jax.ShapeDtypeStruct((t, _GATE_W), BF16),
        ],
        compiler_params=_params("parallel"),
        name="inproj",
    )(x2, nw, w_pack, wgk_hi, wgk_lo, bgk)


def _gla_kernel(q_ref, k_ref, v_ref, g_ref, la_ref, nw_ref, o_ref, state_ref, *, n_sub):
    @pl.when(pl.program_id(1) == 0)
    def _():
        state_ref[...] = jnp.zeros_like(state_ref)

    c = GLA_CHUNK
    row = lax.broadcasted_iota(I32, (c, c), 0)
    col = lax.broadcasted_iota(I32, (c, c), 1)
    causal = row >= col
    tri = causal.astype(BF16)
    mid = c // 2 - 1
    for s in range(n_sub):
        r0 = s * c
        la = la_ref[r0:r0 + c, :]
        la_hi, la_lo = _split_bf16(la)
        cum_all = _dot(tri, la_hi) + _dot(tri, la_lo)
        for h in range(GLA_HEADS):
            ks = slice(h * GLA_DK, (h + 1) * GLA_DK)
            vs = slice(h * GLA_DV, (h + 1) * GLA_DV)
            cum = cum_all[:, ks]
            q = q_ref[r0:r0 + c, ks].astype(F32)
            k = k_ref[r0:r0 + c, ks].astype(F32)
            v = v_ref[r0:r0 + c, vs]
            ref_row = cum[mid:mid + 1, :]
            last = cum[c - 1:c, :]
            qg = (q * jnp.exp(cum - ref_row)).astype(BF16)
            kg = (k * jnp.exp(ref_row - cum)).astype(BF16)
            scores = jnp.where(causal, _dot_nt(qg, kg), 0.0).astype(BF16)
            o = _dot(scores, v)
            st = state_ref[h]
            qe = (q * jnp.exp(cum)).astype(BF16)
            o = o + _dot_nt(qe, st.astype(BF16))
            ko = (k * jnp.exp(last - cum)).astype(BF16)
            state_ref[h] = st * jnp.exp(last) + _dot_tn(v, ko)
            ms = jnp.mean(o * o, axis=-1, keepdims=True)
            on = (o * lax.rsqrt(ms + EPS)) * nw_ref[...]
            g = g_ref[r0:r0 + c, vs].astype(F32)
            o_ref[r0:r0 + c, vs] = (on * (g * jax.nn.sigmoid(g))).astype(BF16)


def _gla(gla, la, nw, batch, seq, ts):
    t = gla.shape[0]
    nblk = seq // ts
    kern = functools.partial(_gla_kernel, n_sub=ts // GLA_CHUNK)
    rowmap = lambda col: (lambda b, i: (b * nblk + i, col))
    return pl.pallas_call(
        kern,
        grid=(batch, nblk),
        in_specs=[
            pl.BlockSpec((ts, GLA_QK_WIDTH), rowmap(0)),
            pl.BlockSpec((ts, GLA_QK_WIDTH), rowmap(1)),
            pl.BlockSpec((ts, GLA_V_WIDTH), rowmap(1)),
            pl.BlockSpec((ts, GLA_V_WIDTH), rowmap(2)),
            pl.BlockSpec((ts, GLA_QK_WIDTH), rowmap(0)),
            pl.BlockSpec((1, GLA_DV), lambda b, i: (0, 0)),
        ],
        out_specs=pl.BlockSpec((ts, GLA_V_WIDTH), rowmap(0)),
        out_shape=jax.ShapeDtypeStruct((t, GLA_V_WIDTH), BF16),
        scratch_shapes=[pltpu.VMEM((GLA_HEADS, GLA_DV, GLA_DK), F32)],
        compiler_params=_params("parallel", "arbitrary"),
        name="gla",
    )(gla, gla, gla, gla, la, nw)


def _rope(x, cos, sin, lane):
    half = MLA_ROPE // 2
    rot = jnp.where(lane < half, -pltpu.roll(x, LANES - half, axis=1), pltpu.roll(x, half, axis=1))
    return x * cos + rot * sin


def _mla_prep_kernel(mla_ref, pos_ref, wuq_ref, wukv_ref, qnw_ref, kvnw_ref, qw_ref, kw_ref,
                     freq_ref, q_ref, k_ref, vt_ref):
    tm = mla_ref.shape[0]
    cq = mla_ref[:, 0:MLA_Q_RANK].astype(F32)
    ckv = mla_ref[:, MLA_Q_RANK:MLA_Q_RANK + MLA_KV_RANK].astype(F32)
    kr = mla_ref[:, MLA_Q_RANK + MLA_KV_RANK:_MLA_W].astype(F32)

    def rms(a, w):
        ms = jnp.mean(a * a, axis=-1, keepdims=True)
        return (a * lax.rsqrt(ms + EPS)) * w

    q_all = _dot(rms(cq, qnw_ref[...]).astype(BF16), wuq_ref[...])
    kv_all = _dot(rms(ckv, kvnw_ref[...]).astype(BF16), wukv_ref[...])

    ang = pos_ref[...] * freq_ref[...]
    cos = jnp.cos(ang)
    sin = jnp.sin(ang)
    lane = lax.broadcasted_iota(I32, (tm, LANES), 1)
    qw_nope = qw_ref[:, 0:MLA_NOPE]
    qw_rope = qw_ref[:, MLA_NOPE:MLA_HEAD_PAD]
    kw_nope = kw_ref[:, 0:MLA_NOPE]
    kw_rope = kw_ref[:, MLA_NOPE:MLA_HEAD_PAD]
    kr_ss = jnp.sum(kr * kr, axis=-1, keepdims=True)
    kr_rot = _rope(kr * kw_rope, cos, sin, lane)
    scale = MLA_QK ** -0.5
    for h in range(MLA_HEADS):
        base = h * MLA_HEAD_PAD
        qn = q_all[:, base:base + MLA_NOPE]
        qr = q_all[:, base + MLA_NOPE:base + MLA_HEAD_PAD]
        ss = jnp.sum(qn * qn, axis=-1, keepdims=True) + jnp.sum(qr * qr, axis=-1, keepdims=True)
        r = lax.rsqrt(ss / MLA_QK + EPS)
        q_ref[:, base:base + MLA_NOPE] = ((qn * r) * qw_nope * scale).astype(BF16)
        q_ref[:, base + MLA_NOPE:base + MLA_HEAD_PAD] = (
            _rope((qr * r) * qw_rope, cos, sin, lane) * scale).astype(BF16)
        kn = kv_all[:, base:base + MLA_NOPE]
        vh = kv_all[:, base + MLA_NOPE:base + MLA_HEAD_PAD]
        ssk = jnp.sum(kn * kn, axis=-1, keepdims=True) + kr_ss
        rk = lax.rsqrt(ssk / MLA_QK + EPS)
        k_ref[:, base:base + MLA_NOPE] = ((kn * rk) * kw_nope).astype(BF16)
        k_ref[:, base + MLA_NOPE:base + MLA_HEAD_PAD] = (kr_rot * rk).astype(BF16)
        vt_ref[0, h * MLA_V:(h + 1) * MLA_V, :] = vh.T.astype(BF16)


def _mla_prep(mla, pos, wuq, wukv, qnw, kvnw, qw, kw, freq, batch, seq, tm):
    t = mla.shape[0]
    nblk = seq // tm
    const = lambda i: (0, 0)
    row = lambda i: (i, 0)
    hw = MLA_HEADS * MLA_HEAD_PAD
    return pl.pallas_call(
        _mla_prep_kernel,
        grid=(t // tm,),
        in_specs=[
            pl.BlockSpec((tm, _MLA_W), row),
            pl.BlockSpec((tm, 1), row),
            pl.BlockSpec((MLA_Q_RANK, hw), const),
            pl.BlockSpec((MLA_KV_RANK, hw), const),
            pl.BlockSpec((1, MLA_Q_RANK), const),
            pl.BlockSpec((1, MLA_KV_RANK), const),
            pl.BlockSpec((1, MLA_HEAD_PAD), const),
            pl.BlockSpec((1, MLA_HEAD_PAD), const),
            pl.BlockSpec((1, LANES), const),
        ],
        out_specs=[
            pl.BlockSpec((tm, hw), row),
            pl.BlockSpec((tm, hw), row),
            pl.BlockSpec((1, MLA_HEADS * MLA_V, tm), lambda i: (i // nblk, 0, i % nblk)),
        ],
        out_shape=[
            jax.ShapeDtypeStruct((t, hw), BF16),
            jax.ShapeDtypeStruct((t, hw), BF16),
            jax.ShapeDtypeStruct((batch, MLA_HEADS * MLA_V, seq), BF16),
        ],
        compiler_params=_params("parallel"),
        name="mla_prep",
    )(mla, pos, wuq, wukv, qnw, kvnw, qw, kw, freq)


def _flash_kernel(q_ref, k_ref, vt_ref, o_ref, acc_ref, m_ref, l_ref, *, tq, tk):
    seq = q_ref.shape[0]
    nq = seq // tq
    ratio = tq // tk
    krow = lax.broadcasted_iota(I32, (tk, tq), 0)
    qcol = lax.broadcasted_iota(I32, (tk, tq), 1)

    def step(qb, k0, masked, q0):
        kb = k_ref[pl.ds(k0, tk), :]
        s = _dot_nt(kb, qb)
        if masked:
            s = jnp.where(krow + k0 <= qcol + q0, s, -jnp.inf)
        m_prev = m_ref[...]
        m_new = jnp.maximum(m_prev, jnp.max(s, axis=0, keepdims=True))
        alpha = jnp.exp(m_prev - m_new)
        p = jnp.exp(s - m_new)
        l_ref[...] = alpha * l_ref[...] + jnp.sum(p, axis=0, keepdims=True)
        m_ref[...] = m_new
        vt = vt_ref[0, :, pl.ds(k0, tk)]
        acc_ref[...] = alpha * acc_ref[...] + _dot(vt, p.astype(BF16))

    for qi in range(nq):
        q0 = qi * tq
        qb = q_ref[q0:q0 + tq, :]
        m_ref[...] = jnp.full_like(m_ref, -jnp.inf)
        l_ref[...] = jnp.zeros_like(l_ref)
        acc_ref[...] = jnp.zeros_like(acc_ref)
        n_full = qi * ratio
        if n_full > 0:
            def body(ki, carry, qb=qb, q0=q0):
                step(qb, pl.multiple_of(ki * tk, tk), False, q0)
                return carry
            lax.fori_loop(0, n_full, body, 0)
        for d in range(ratio):
            step(qb, q0 + d * tk, True, q0)
        out_t = acc_ref[...] * (1.0 / l_ref[...])
        o_ref[q0:q0 + tq, :] = out_t.T.astype(BF16)


def _flash(q, k, vt, batch, seq, tq, tk):
    t = q.shape[0]
    kern = functools.partial(_flash_kernel, tq=tq, tk=tk)
    return pl.pallas_call(
        kern,
        grid=(batch, MLA_HEADS),
        in_specs=[
            pl.BlockSpec((seq, MLA_HEAD_PAD), lambda b, h: (b, h)),
            pl.BlockSpec((seq, MLA_HEAD_PAD), lambda b, h: (b, h)),
            pl.BlockSpec((1, MLA_V, seq), lambda b, h: (b, h, 0)),
        ],
        out_specs=pl.BlockSpec((seq, MLA_V), lambda b, h: (b, h)),
        out_shape=jax.ShapeDtypeStruct((t, MLA_HEADS * MLA_V), BF16),
        scratch_shapes=[
            pltpu.VMEM((MLA_V, tq), F32),
            pltpu.VMEM((1, tq), F32),
            pltpu.VMEM((1, tq), F32),
        ],
        compiler_params=_params("parallel", "parallel"),
        name="flash",
    )(q, k, vt)


def _merge_route_kernel(x_ref, oa_ref, ob_ref, gate_ref, wa_ref, wb_ref, wo_ref, nw_ref,
                        wr_hi_ref, wr_lo_ref, br_ref,
                        x1_ref, xn_ref, eid_ref, cw_ref, cnt_ref):
    @pl.when(pl.program_id(0) == 0)
    def _():
        cnt_ref[...] = jnp.zeros_like(cnt_ref)

    tm = x_ref.shape[0]
    ya = _dot(oa_ref[...], wa_ref[...])
    yb = _dot(ob_ref[...], wb_ref[...])
    ga = gate_ref[:, 0:D_MODEL].astype(F32)
    gb = gate_ref[:, D_MODEL:2 * D_MODEL].astype(F32)
    merged = jax.nn.sigmoid(ga) * ya + jax.nn.sigmoid(gb) * yb
    x1 = x_ref[...] + _dot(merged.astype(BF16), wo_ref[...])
    x1_ref[...] = x1
    ms = jnp.mean(x1 * x1, axis=-1, keepdims=True)
    xn = (x1 * lax.rsqrt(ms + EPS)) * nw_ref[...]
    xn_ref[...] = xn
    xn_hi, xn_lo = _split_bf16(xn)
    logits = (_dot(xn_hi, wr_hi_ref[...]) + _dot(xn_lo, wr_hi_ref[...])
              + _dot(xn_hi, wr_lo_ref[...])) + br_ref[...]
    lane = lax.broadcasted_iota(I32, (tm, LANES), 1)
    work = jnp.where(lane < N_EXPERTS, logits, -jnp.inf)
    vals, idxs = [], []
    for _ in range(TOP_K):
        m = jnp.max(work, axis=-1, keepdims=True)
        idx = jnp.min(jnp.where(work == m, lane, LANES), axis=-1, keepdims=True)
        vals.append(m)
        idxs.append(idx)
        work = jnp.where(lane == idx, -jnp.inf, work)
    exps = [jnp.exp(v - vals[0]) for v in vals]
    denom = exps[0] + exps[1] + exps[2] + exps[3]
    eid = jnp.zeros((tm, LANES), I32)
    cw = jnp.zeros((tm, LANES), F32)
    sel = jnp.zeros((tm, LANES), F32)
    for kk in range(TOP_K):
        eid = jnp.where(lane == kk, idxs[kk], eid)
        cw = jnp.where(lane == kk, exps[kk] / denom, cw)
        sel = sel + (lane == idxs[kk]).astype(F32)
    eid_ref[...] = eid
    cw_ref[...] = cw
    cnt_ref[0:1, :] = cnt_ref[0:1, :] + jnp.sum(sel, axis=0, keepdims=True)


def _merge_route(x2, oa, ob, gates, wa, wb, wo, nw, wr_hi, wr_lo, br, tm):
    t = x2.shape[0]
    const = lambda i: (0, 0)
    row = lambda i: (i, 0)
    return pl.pallas_call(
        _merge_route_kernel,
        grid=(t // tm,),
        in_specs=[
            pl.BlockSpec((tm, D_MODEL), row),
            pl.BlockSpec((tm, GLA_V_WIDTH), row),
            pl.BlockSpec((tm, MLA_HEADS * MLA_V), row),
            pl.BlockSpec((tm, _GATE_W), row),
            pl.BlockSpec((GLA_V_WIDTH, D_MODEL), const),
            pl.BlockSpec((MLA_HEADS * MLA_V, D_MODEL), const),
            pl.BlockSpec((D_MODEL, D_MODEL), const),
            pl.BlockSpec((1, D_MODEL), const),
            pl.BlockSpec((D_MODEL, LANES), const),
            pl.BlockSpec((D_MODEL, LANES), const),
            pl.BlockSpec((1, LANES), const),
        ],
        out_specs=[
            pl.BlockSpec((tm, D_MODEL), row),
            pl.BlockSpec((tm, D_MODEL), row),
            pl.BlockSpec((tm, LANES), row),
            pl.BlockSpec((tm, LANES), row),
            pl.BlockSpec((8, LANES), const),
        ],
        out_shape=[
            jax.ShapeDtypeStruct((t, D_MODEL), F32),
            jax.ShapeDtypeStruct((t, D_MODEL), F32),
            jax.ShapeDtypeStruct((t, LANES), I32),
            jax.ShapeDtypeStruct((t, LANES), F32),
            jax.ShapeDtypeStruct((8, LANES), F32),
        ],
        compiler_params=_params("arbitrary"),
        name="merge_route",
    )(x2, oa, ob, gates, wa, wb, wo, nw, wr_hi, wr_lo, br)


def _positions_kernel(eid_ref, cnt_ref, pos_ref, carry_ref, *, tile):
    @pl.when(pl.program_id(0) == 0)
    def _():
        carry_ref[...] = jnp.zeros_like(carry_ref)

    tb = eid_ref.shape[0]
    lane1 = lax.broadcasted_iota(I32, (1, LANES), 1)
    cnt = cnt_ref[0:1, :]
    padded = jnp.floor((cnt + (tile - 1)) / tile) * tile
    incl = padded
    shift = 1
    while shift < N_EXPERTS:
        incl = incl + jnp.where(lane1 >= shift, pltpu.roll(incl, shift, axis=1), 0.0)
        shift *= 2
    offs = incl - padded

    lane = lax.broadcasted_iota(I32, (tb, LANES), 1)
    eid = eid_ref[...]
    onehots = [lane == jnp.broadcast_to(eid[:, kk:kk + 1], (tb, LANES)) for kk in range(TOP_K)]
    sel = jnp.zeros((tb, LANES), F32)
    for oh in onehots:
        sel = sel + oh.astype(F32)
    row = lax.broadcasted_iota(I32, (tb, tb), 0)
    col = lax.broadcasted_iota(I32, (tb, tb), 1)
    strict = (row > col).astype(BF16)
    rank = _dot(strict, sel.astype(BF16)) + carry_ref[...] + offs
    pos = jnp.zeros((tb, LANES), I32)
    for kk in range(TOP_K):
        pk = jnp.sum(jnp.where(onehots[kk], rank, 0.0), axis=-1, keepdims=True)
        pos = jnp.where(lane == kk, pk.astype(I32), pos)
    pos_ref[...] = pos
    carry_ref[...] = carry_ref[...] + jnp.sum(sel, axis=0, keepdims=True)


def _positions(eid, cnt, tb, tile):
    t = eid.shape[0]
    return pl.pallas_call(
        functools.partial(_positions_kernel, tile=tile),
        grid=(t // tb,),
        in_specs=[
            pl.BlockSpec((tb, LANES), lambda i: (i, 0)),
            pl.BlockSpec((8, LANES), lambda i: (0, 0)),
        ],
        out_specs=pl.BlockSpec((tb, LANES), lambda i: (i, 0)),
        out_shape=jax.ShapeDtypeStruct((t, LANES), I32),
        scratch_shapes=[pltpu.VMEM((1, LANES), F32)],
        compiler_params=_params("arbitrary"),
        name="positions",
    )(eid, cnt)


def _dispatch_kernel(pos_ref, xn_ref, xs_in_ref, xs_ref, sem):
    del xs_in_ref
    tb = xn_ref.shape[0]

    def row_copy(t, kk):
        p = pos_ref[0, 0, t * TOP_K + kk]
        return pltpu.make_async_copy(xn_ref.at[pl.ds(t, 1)], xs_ref.at[pl.ds(p, 1)], sem)

    def issue(t, carry):
        for kk in range(TOP_K):
            row_copy(t, kk).start()
        return carry

    def drain(t, carry):
        for kk in range(TOP_K):
            row_copy(t, kk).wait()
        return carry

    lax.fori_loop(0, tb, issue, 0)
    lax.fori_loop(0, tb, drain, 0)


def _dispatch(pos3, xn, xs0, tb):
    t = xn.shape[0]
    return pl.pallas_call(
        _dispatch_kernel,
        grid=(t // tb,),
        in_specs=[
            pl.BlockSpec((1, 1, tb * TOP_K), lambda i: (i, 0, 0), memory_space=pltpu.SMEM),
            pl.BlockSpec((tb, D_MODEL), lambda i: (i, 0)),
            pl.BlockSpec(memory_space=pl.ANY),
        ],
        out_specs=pl.BlockSpec(memory_space=pl.ANY),
        out_shape=jax.ShapeDtypeStruct(xs0.shape, xs0.dtype),
        scratch_shapes=[pltpu.SemaphoreType.DMA],
        input_output_aliases={2: 0},
        compiler_params=_params("arbitrary"),
        name="dispatch",
    )(pos3, xn, xs0)


def _experts_kernel(te_ref, nv_ref, xs_ref, wg_ref, bg_ref, wu_ref, bu_ref, wd_ref, bd_ref, ys_ref):
    @pl.when(pl.program_id(0) < nv_ref[0])
    def _():
        xb = xs_ref[...].astype(BF16)
        g = _dot(xb, wg_ref[0]) + bg_ref[0]
        u = _dot(xb, wu_ref[0]) + bu_ref[0]
        g = jnp.minimum(g, SWIGLU_LIMIT)
        u = jnp.clip(u, -SWIGLU_LIMIT, SWIGLU_LIMIT)
        hidden = (u + 1.0) * g * jax.nn.sigmoid(SWIGLU_ALPHA * g)
        ys_ref[...] = _dot(hidden.astype(BF16), wd_ref[0]) + bd_ref[0]


def _experts(tile_expert, n_valid, xs, wg, bg, wu, bu, wd, bd, tile):
    rows = xs.shape[0]
    nt = rows // tile
    rmap = lambda i, te, nv: (jnp.minimum(i, nv[0] - 1), 0)
    wmap = lambda i, te, nv: (te[i], 0, 0)
    return pl.pallas_call(
        _experts_kernel,
        grid_spec=pltpu.PrefetchScalarGridSpec(
            num_scalar_prefetch=2,
            grid=(nt,),
            in_specs=[
                pl.BlockSpec((tile, D_MODEL), rmap),
                pl.BlockSpec((1, D_MODEL, D_FF), wmap),
                pl.BlockSpec((1, 1, D_FF), wmap),
                pl.BlockSpec((1, D_MODEL, D_FF), wmap),
                pl.BlockSpec((1, 1, D_FF), wmap),
                pl.BlockSpec((1, D_FF, D_MODEL), wmap),
                pl.BlockSpec((1, 1, D_MODEL), wmap),
            ],
            out_specs=pl.BlockSpec((tile, D_MODEL), rmap),
        ),
        out_shape=jax.ShapeDtypeStruct((rows, D_MODEL), F32),
        compiler_params=_params("arbitrary"),
        name="experts",
    )(tile_expert, n_valid, xs, wg, bg, wu, bu, wd, bd)


def _combine_kernel(pos_ref, x1_ref, cw_ref, ys_ref, o_ref, buf_ref, sem):
    tb = x1_ref.shape[0]

    def row_copy(t, kk):
        p = pos_ref[0, 0, t * TOP_K + kk]
        return pltpu.make_async_copy(ys_ref.at[pl.ds(p, 1)], buf_ref.at[kk, pl.ds(t, 1)], sem)

    def issue(t, carry):
        for kk in range(TOP_K):
            row_copy(t, kk).start()
        return carry

    def drain(t, carry):
        for kk in range(TOP_K):
            row_copy(t, kk).wait()
        return carry

    lax.fori_loop(0, tb, issue, 0)
    lax.fori_loop(0, tb, drain, 0)
    cw = cw_ref[...]
    acc = x1_ref[...]
    for kk in range(TOP_K):
        acc = acc + cw[:, kk:kk + 1] * buf_ref[kk]
    o_ref[...] = acc


def _combine(pos3, x1, cw, ys, tb):
    t = x1.shape[0]
    return pl.pallas_call(
        _combine_kernel,
        grid=(t // tb,),
        in_specs=[
            pl.BlockSpec((1, 1, tb * TOP_K), lambda i: (i, 0, 0), memory_space=pltpu.SMEM),
            pl.BlockSpec((tb, D_MODEL), lambda i: (i, 0)),
            pl.BlockSpec((tb, LANES), lambda i: (i, 0)),
            pl.BlockSpec(memory_space=pl.ANY),
        ],
        out_specs=pl.BlockSpec((tb, D_MODEL), lambda i: (i, 0)),
        out_shape=jax.ShapeDtypeStruct((t, D_MODEL), F32),
        scratch_shapes=[pltpu.VMEM((TOP_K, tb, D_MODEL), F32), pltpu.SemaphoreType.DMA],
        compiler_params=_params("arbitrary"),
        name="combine",
    )(pos3, x1, cw, ys)


def _pad_cols(a, width):
    return jnp.pad(a, ((0, 0), (0, width - a.shape[1])))


def _pack_in_proj(w_in):
    o = np.cumsum((GLA_QK_WIDTH, GLA_QK_WIDTH, GLA_V_WIDTH, GLA_V_WIDTH, GLA_GATE_RANK,
                   MLA_Q_RANK, MLA_KV_RANK, MLA_ROPE, D_MODEL, D_MODEL)).tolist()
    gla = w_in[:, 0:o[3]]
    lr = _pad_cols(w_in[:, o[3]:o[4]], _LR_W)
    mla = _pad_cols(w_in[:, o[4]:o[7]], _MLA_W)
    gates = w_in[:, o[7]:o[9]]
    return jnp.concatenate([gla, lr, mla, gates], axis=1).astype(BF16)


def _pad_heads(w, head_w):
    r = w.shape[0]
    w3 = w.reshape(r, MLA_HEADS, head_w)
    w3 = jnp.pad(w3, ((0, 0), (0, 0), (0, MLA_HEAD_PAD - head_w)))
    return w3.reshape(r, MLA_HEADS * MLA_HEAD_PAD)


def _tile_rows(n, pref):
    return pref if n % pref == 0 else n


def kernel(x, positions, attn_norm_w, w_in, w_gla_gk, b_gla_gk, gla_out_norm_w, w_gla_out,
           mla_q_norm_w, w_mla_uq, mla_kv_norm_w, w_mla_ukv, mla_qk_q_norm_w, mla_qk_k_norm_w,
           w_mla_out, w_out, moe_norm_w, w_router, b_router, w_exp_gate, b_exp_gate,
           w_exp_up, b_exp_up, w_exp_down, b_exp_down):
    batch, seq, _ = x.shape
    depth = w_in.shape[0]
    t = batch * seq
    x2 = x.reshape(t, D_MODEL)
    pos = positions.reshape(t, 1).astype(F32)
    half = MLA_ROPE // 2
    inv_freq = ROPE_BASE ** (-jnp.arange(half, dtype=F32) / half)
    freq = _pad_cols(jnp.concatenate([inv_freq, inv_freq])[None, :], LANES)

    tm = _tile_rows(t, 512)
    gla_ts = _tile_rows(seq, 256)
    flash_t = _tile_rows(seq, 512)
    moe_tile = 512
    n_rows = t * TOP_K + N_EXPERTS * moe_tile
    n_rows = (n_rows // moe_tile) * moe_tile
    disp_tb = _tile_rows(t, 512)
    comb_tb = _tile_rows(t, 256)
    pos_tb = _tile_rows(t, 256)

    for l in range(depth):
        w_pack = _pack_in_proj(w_in[l])
        wgk = jnp.pad(w_gla_gk[l], ((0, LANES - GLA_GATE_RANK), (0, 0)))
        wgk_hi, wgk_lo = _split_bf16(wgk)
        gla, la, mla, gates = _inproj(x2, attn_norm_w[l][None, :], w_pack, wgk_hi, wgk_lo,
                                      b_gla_gk[l][None, :], tm)

        o_a = _gla(gla, la, gla_out_norm_w[l][None, :], batch, seq, gla_ts)

        wuq = _pad_heads(w_mla_uq[l], MLA_QK).astype(BF16)
        wukv = w_mla_ukv[l].astype(BF16)
        qw = _pad_cols(mla_qk_q_norm_w[l][None, :], MLA_HEAD_PAD)
        kw = _pad_cols(mla_qk_k_norm_w[l][None, :], MLA_HEAD_PAD)
        q, k, vt = _mla_prep(mla, pos, wuq, wukv, mla_q_norm_w[l][None, :],
                             mla_kv_norm_w[l][None, :], qw, kw, freq, batch, seq,
                             _tile_rows(seq, 512))
        o_b = _flash(q, k, vt, batch, seq, flash_t, flash_t)

        wr = _pad_cols(w_router[l], LANES)
        wr_hi, wr_lo = _split_bf16(wr)
        br = _pad_cols(b_router[l][None, :], LANES)
        x1, xn, eid, cw, cnt = _merge_route(
            x2, o_a, o_b, gates, w_gla_out[l].astype(BF16), w_mla_out[l].astype(BF16),
            w_out[l].astype(BF16), moe_norm_w[l][None, :], wr_hi, wr_lo, br, tm)

        posn = _positions(eid, cnt, pos_tb, moe_tile)
        pos4 = posn[:, :TOP_K]
        counts = cnt[0, :N_EXPERTS].astype(I32)
        padded = ((counts + moe_tile - 1) // moe_tile) * moe_tile
        ends = jnp.cumsum(padded)
        n_tiles = n_rows // moe_tile
        n_valid = (ends[-1] // moe_tile).astype(I32)
        starts = jnp.arange(n_tiles, dtype=I32) * moe_tile
        starts = jnp.minimum(starts, ends[-1] - moe_tile)
        tile_expert = jnp.minimum(jnp.searchsorted(ends, starts, side="right"),
                                  N_EXPERTS - 1).astype(I32)

        xs0 = jnp.zeros((n_rows, D_MODEL), F32)
        xs = _dispatch(pos4.reshape(t // disp_tb, 1, disp_tb * TOP_K), xn, xs0, disp_tb)
        ys = _experts(tile_expert, n_valid.reshape(1), xs,
                      w_exp_gate[l].astype(BF16), b_exp_gate[l][:, None, :],
                      w_exp_up[l].astype(BF16), b_exp_up[l][:, None, :],
                      w_exp_down[l].astype(BF16), b_exp_down[l][:, None, :], moe_tile)
        x2 = _combine(pos4.reshape(t // comb_tb, 1, comb_tb * TOP_K), x1, cw, ys, comb_tb)
    return x2.reshape(batch, seq, D_MODEL)
```

```python
import functools

import jax
import jax.numpy as jnp
import numpy as np
from jax import lax
from jax.experimental import pallas as pl
from jax.experimental.pallas import tpu as pltpu

F32 = jnp.float32
BF16 = jnp.bfloat16
I32 = jnp.int32

D_MODEL = 1024
EPS = 1e-6
GLA_HEADS = 4
GLA_DK = 128
GLA_DV = 256
GLA_GATE_RANK = 16
GLA_GATE_NORMALIZER = 16.0
GLA_QK_WIDTH = GLA_HEADS * GLA_DK
GLA_V_WIDTH = GLA_HEADS * GLA_DV
MLA_HEADS = 8
MLA_Q_RANK = 384
MLA_KV_RANK = 256
MLA_NOPE = 128
MLA_ROPE = 64
MLA_V = 128
MLA_QK = MLA_NOPE + MLA_ROPE
ROPE_BASE = 10000.0
N_EXPERTS = 32
TOP_K = 4
D_FF = 1024
SWIGLU_LIMIT = 7.0
SWIGLU_ALPHA = 1.702

LANES = 128
MLA_HEAD_PAD = 256
GLA_CHUNK = 128
FLASH_V_ROWS = 144
LOG2_E = 1.4426950408889634
VMEM_LIMIT = 56 * 1024 * 1024

_GLA_W = 2 * GLA_QK_WIDTH + 2 * GLA_V_WIDTH
_LR_W = LANES
_MLA_W = 768
_GATE_W = 2 * D_MODEL
_PACK_W = _GLA_W + _LR_W + _MLA_W + _GATE_W


def _dot(a, b):
    return jnp.dot(a, b, preferred_element_type=F32)


def _dot_nt(a, b):
    return lax.dot_general(a, b, (((1,), (1,)), ((), ())), preferred_element_type=F32)


def _dot_tn(a, b):
    return lax.dot_general(a, b, (((0,), (0,)), ((), ())), preferred_element_type=F32)


def _split_bf16(a):
    hi = a.astype(BF16)
    lo = (a - hi.astype(F32)).astype(BF16)
    return hi, lo


def _params(*sem):
    return pltpu.CompilerParams(dimension_semantics=sem, vmem_limit_bytes=VMEM_LIMIT)


def _inproj_kernel(x_ref, nw_ref, w_ref, wgk_hi_ref, wgk_lo_ref, bgk_ref,
                   gla_ref, la_ref, mla_ref, gate_ref):
    x = x_ref[...]
    ms = jnp.mean(x * x, axis=-1, keepdims=True)
    h = (x * lax.rsqrt(ms + EPS)) * nw_ref[...]
    hb = h.astype(BF16)
    o0 = 0
    q = _dot(hb, w_ref[:, 0:GLA_QK_WIDTH]) * (GLA_DK ** -0.5)
    gla_ref[:, 0:GLA_QK_WIDTH] = q.astype(BF16)
    gla_ref[:, GLA_QK_WIDTH:_GLA_W] = _dot(hb, w_ref[:, GLA_QK_WIDTH:_GLA_W]).astype(BF16)
    o0 = _GLA_W
    lr = _dot(hb, w_ref[:, o0:o0 + _LR_W])
    lr_hi, lr_lo = _split_bf16(lr)
    a_logit = (_dot(lr_hi, wgk_hi_ref[...]) + _dot(lr_lo, wgk_hi_ref[...])
               + _dot(lr_hi, wgk_lo_ref[...])) + bgk_ref[...]
    log_sig = jnp.minimum(a_logit, 0.0) - jnp.log1p(jnp.exp(-jnp.abs(a_logit)))
    la_ref[...] = log_sig / GLA_GATE_NORMALIZER
    o0 += _LR_W
    mla_ref[...] = _dot(hb, w_ref[:, o0:o0 + _MLA_W]).astype(BF16)
    o0 += _MLA_W
    gate_ref[...] = _dot(hb, w_ref[:, o0:o0 + _GATE_W]).astype(BF16)


def _inproj(x2, nw, w_pack, wgk_hi, wgk_lo, bgk, tm):
    t = x2.shape[0]
    const = lambda i: (0, 0)
    row = lambda i: (i, 0)
    return pl.pallas_call(
        _inproj_kernel,
        grid=(t // tm,),
        in_specs=[
            pl.BlockSpec((tm, D_MODEL), row),
            pl.BlockSpec((1, D_MODEL), const),
            pl.BlockSpec((D_MODEL, _PACK_W), const, pipeline_mode=pl.Buffered(1)),
            pl.BlockSpec((LANES, GLA_QK_WIDTH), const),
            pl.BlockSpec((LANES, GLA_QK_WIDTH), const),
            pl.BlockSpec((1, GLA_QK_WIDTH), const),
        ],
        out_specs=[
            pl.BlockSpec((tm, _GLA_W), row),
            pl.BlockSpec((tm, GLA_QK_WIDTH), row),
            pl.BlockSpec((tm, _MLA_W), row),
            pl.BlockSpec((tm, _GATE_W), row),
        ],
        out_shape=[
            jax.ShapeDtypeStruct((t, _GLA_W), BF16),
            jax.ShapeDtypeStruct((t, GLA_QK_WIDTH), F32),
            jax.ShapeDtypeStruct((t, _MLA_W), BF16),
            jax.ShapeDtypeStruct((t, _GATE_W), BF16),
        ],
        compiler_params=_params("parallel"),
        name="inproj",
    )(x2, nw, w_pack, wgk_hi, wgk_lo, bgk)


def _gla_kernel(q_ref, k_ref, v_ref, g_ref, la_ref, nw_ref, o_ref, state_ref, *, n_sub):
    @pl.when(pl.program_id(1) == 0)
    def _():
        state_ref[...] = jnp.zeros_like(state_ref)

    c = GLA_CHUNK
    row = lax.broadcasted_iota(I32, (c, c), 0)
    col = lax.broadcasted_iota(I32, (c, c), 1)
    causal = row >= col
    tri = causal.astype(BF16)
    mid = c // 2 - 1
    for s in range(n_sub):
        r0 = s * c
        la = la_ref[r0:r0 + c, :]
        la_hi, la_lo = _split_bf16(la)
        cum_all = _dot(tri, la_hi) + _dot(tri, la_lo)
        for h in range(GLA_HEADS):
            ks = slice(h * GLA_DK, (h + 1) * GLA_DK)
            vs = slice(h * GLA_DV, (h + 1) * GLA_DV)
            cum = cum_all[:, ks]
            q = q_ref[r0:r0 + c, ks].astype(F32)
            k = k_ref[r0:r0 + c, ks].astype(F32)
            v = v_ref[r0:r0 + c, vs]
            ref_row = cum[mid:mid + 1, :]
            last = cum[c - 1:c, :]
            qg = (q * jnp.exp(cum - ref_row)).astype(BF16)
            kg = (k * jnp.exp(ref_row - cum)).astype(BF16)
            scores = jnp.where(causal, _dot_nt(qg, kg), 0.0).astype(BF16)
            o = _dot(scores, v)
            st = state_ref[h]
            qe = (q * jnp.exp(cum)).astype(BF16)
            o = o + _dot_nt(qe, st.astype(BF16))
            ko = (k * jnp.exp(last - cum)).astype(BF16)
            state_ref[h] = st * jnp.exp(last) + _dot_tn(v, ko)
            ms = jnp.mean(o * o, axis=-1, keepdims=True)
            on = (o * lax.rsqrt(ms + EPS)) * nw_ref[...]
            g = g_ref[r0:r0 + c, vs].astype(F32)
            o_ref[r0:r0 + c, vs] = (on * (g * jax.nn.sigmoid(g))).astype(BF16)


def _gla(gla, la, nw, batch, seq, ts):
    t = gla.shape[0]
    nblk = seq // ts
    kern = functools.partial(_gla_kernel, n_sub=ts // GLA_CHUNK)
    rowmap = lambda col: (lambda b, i: (b * nblk + i, col))
    return pl.pallas_call(
        kern,
        grid=(batch, nblk),
        in_specs=[
            pl.BlockSpec((ts, GLA_QK_WIDTH), rowmap(0)),
            pl.BlockSpec((ts, GLA_QK_WIDTH), rowmap(1)),
            pl.BlockSpec((ts, GLA_V_WIDTH), rowmap(1)),
            pl.BlockSpec((ts, GLA_V_WIDTH), rowmap(2)),
            pl.BlockSpec((ts, GLA_QK_WIDTH), rowmap(0)),
            pl.BlockSpec((1, GLA_DV), lambda b, i: (0, 0)),
        ],
        out_specs=pl.BlockSpec((ts, GLA_V_WIDTH), rowmap(0)),
        out_shape=jax.ShapeDtypeStruct((t, GLA_V_WIDTH), BF16),
        scratch_shapes=[pltpu.VMEM((GLA_HEADS, GLA_DV, GLA_DK), F32)],
        compiler_params=_params("parallel", "arbitrary"),
        name="gla",
    )(gla, gla, gla, gla, la, nw)


def _rope(x, cos, sin, lane):
    half = MLA_ROPE // 2
    rot = jnp.where(lane < half, -pltpu.roll(x, LANES - half, axis=1), pltpu.roll(x, half, axis=1))
    return x * cos + rot * sin


def _mla_prep_kernel(mla_ref, pos_ref, wuq_ref, wukv_ref, qnw_ref, kvnw_ref, qw_ref, kw_ref,
                     freq_ref, q_ref, k_ref, vt_ref):
    tm = mla_ref.shape[0]
    cq = mla_ref[:, 0:MLA_Q_RANK].astype(F32)
    ckv = mla_ref[:, MLA_Q_RANK:MLA_Q_RANK + MLA_KV_RANK].astype(F32)
    kr = mla_ref[:, MLA_Q_RANK + MLA_KV_RANK:_MLA_W].astype(F32)

    def rms(a, w):
        ms = jnp.mean(a * a, axis=-1, keepdims=True)
        return (a * lax.rsqrt(ms + EPS)) * w

    q_all = _dot(rms(cq, qnw_ref[...]).astype(BF16), wuq_ref[...])
    kv_all = _dot(rms(ckv, kvnw_ref[...]).astype(BF16), wukv_ref[...])

    ang = pos_ref[...] * freq_ref[...]
    cos = jnp.cos(ang)
    sin = jnp.sin(ang)
    lane = lax.broadcasted_iota(I32, (tm, LANES), 1)
    qw_nope = qw_ref[:, 0:MLA_NOPE]
    qw_rope = qw_ref[:, MLA_NOPE:MLA_HEAD_PAD]
    kw_nope = kw_ref[:, 0:MLA_NOPE]
    kw_rope = kw_ref[:, MLA_NOPE:MLA_HEAD_PAD]
    kr_ss = jnp.sum(kr * kr, axis=-1, keepdims=True)
    kr_rot = _rope(kr * kw_rope, cos, sin, lane)
    scale = (MLA_QK ** -0.5) * LOG2_E
    pad_rows = lax.broadcasted_iota(I32, (FLASH_V_ROWS - MLA_V, tm), 0)
    ones_pad = (pad_rows == 0).astype(BF16)
    for h in range(MLA_HEADS):
        base = h * MLA_HEAD_PAD
        qn = q_all[:, base:base + MLA_NOPE]
        qr = q_all[:, base + MLA_NOPE:base + MLA_HEAD_PAD]
        ss = jnp.sum(qn * qn, axis=-1, keepdims=True) + jnp.sum(qr * qr, axis=-1, keepdims=True)
        r = lax.rsqrt(ss / MLA_QK + EPS)
        q_ref[:, base:base + MLA_NOPE] = ((qn * r) * qw_nope * scale).astype(BF16)
        q_ref[:, base + MLA_NOPE:base + MLA_HEAD_PAD] = (
            _rope((qr * r) * qw_rope, cos, sin, lane) * scale).astype(BF16)
        kn = kv_all[:, base:base + MLA_NOPE]
        vh = kv_all[:, base + MLA_NOPE:base + MLA_HEAD_PAD]
        ssk = jnp.sum(kn * kn, axis=-1, keepdims=True) + kr_ss
        rk = lax.rsqrt(ssk / MLA_QK + EPS)
        k_ref[:, base:base + MLA_NOPE] = ((kn * rk) * kw_nope).astype(BF16)
        k_ref[:, base + MLA_NOPE:base + MLA_HEAD_PAD] = (kr_rot * rk).astype(BF16)
        vt_ref[0, h * FLASH_V_ROWS:h * FLASH_V_ROWS + MLA_V, :] = vh.T.astype(BF16)
        vt_ref[0, h * FLASH_V_ROWS + MLA_V:(h + 1) * FLASH_V_ROWS, :] = ones_pad


def _mla_prep(mla, pos, wuq, wukv, qnw, kvnw, qw, kw, freq, batch, seq, tm):
    t = mla.shape[0]
    nblk = seq // tm
    const = lambda i: (0, 0)
    row = lambda i: (i, 0)
    hw = MLA_HEADS * MLA_HEAD_PAD
    return pl.pallas_call(
        _mla_prep_kernel,
        grid=(t // tm,),
        in_specs=[
            pl.BlockSpec((tm, _MLA_W), row),
            pl.BlockSpec((tm, 1), row),
            pl.BlockSpec((MLA_Q_RANK, hw), const),
            pl.BlockSpec((MLA_KV_RANK, hw), const),
            pl.BlockSpec((1, MLA_Q_RANK), const),
            pl.BlockSpec((1, MLA_KV_RANK), const),
            pl.BlockSpec((1, MLA_HEAD_PAD), const),
            pl.BlockSpec((1, MLA_HEAD_PAD), const),
            pl.BlockSpec((1, LANES), const),
        ],
        out_specs=[
            pl.BlockSpec((tm, hw), row),
            pl.BlockSpec((tm, hw), row),
            pl.BlockSpec((1, MLA_HEADS * FLASH_V_ROWS, tm), lambda i: (i // nblk, 0, i % nblk)),
        ],
        out_shape=[
            jax.ShapeDtypeStruct((t, hw), BF16),
            jax.ShapeDtypeStruct((t, hw), BF16),
            jax.ShapeDtypeStruct((batch, MLA_HEADS * FLASH_V_ROWS, seq), BF16),
        ],
        compiler_params=_params("parallel"),
        name="mla_prep",
    )(mla, pos, wuq, wukv, qnw, kvnw, qw, kw, freq)


def _flash_kernel(q_ref, k_ref, vt_ref, o_ref, s_ref, cm_ref, acc_ref, m_ref, *, t):
    seq = q_ref.shape[0]
    nq = seq // t
    krow = lax.broadcasted_iota(I32, (t, t), 0)
    qcol = lax.broadcasted_iota(I32, (t, t), 1)
    items = [(qi, j) for qi in range(nq) for j in range(qi + 1)]

    def scores(idx):
        qi, j = items[idx]
        s = _dot_nt(k_ref[j * t:(j + 1) * t, :], q_ref[qi * t:(qi + 1) * t, :])
        s_ref[idx % 2] = s
        cm_ref[idx % 2] = jnp.max(s, axis=0, keepdims=True)

    def update(idx):
        qi, j = items[idx]
        if j == 0:
            m_ref[...] = jnp.full_like(m_ref, -jnp.inf)
            acc_ref[...] = jnp.zeros_like(acc_ref)
        s = s_ref[idx % 2]
        if j == qi:
            s = jnp.where(krow <= qcol, s, -jnp.inf)
            cm = jnp.max(s, axis=0, keepdims=True)
        else:
            cm = cm_ref[idx % 2]
        m_prev = m_ref[...]
        m_new = jnp.maximum(m_prev, cm)
        alpha = jnp.exp2(m_prev - m_new)
        p = jnp.exp2(s - m_new).astype(BF16)
        m_ref[...] = m_new
        acc_ref[...] = alpha * acc_ref[...] + _dot(vt_ref[0, :, j * t:(j + 1) * t], p)
        if j == qi:
            acc = acc_ref[...]
            out_t = acc[0:MLA_V, :] * (1.0 / acc[MLA_V:MLA_V + 1, :])
            o_ref[qi * t:(qi + 1) * t, :] = out_t.T.astype(BF16)

    scores(0)
    for idx in range(len(items)):
        if idx + 1 < len(items):
            scores(idx + 1)
        update(idx)


def _flash(q, k, vt, batch, seq, t):
    rows = q.shape[0]
    return pl.pallas_call(
        functools.partial(_flash_kernel, t=t),
        grid=(batch, MLA_HEADS),
        in_specs=[
            pl.BlockSpec((seq, MLA_HEAD_PAD), lambda b, h: (b, h)),
            pl.BlockSpec((seq, MLA_HEAD_PAD), lambda b, h: (b, h)),
            pl.BlockSpec((1, FLASH_V_ROWS, seq), lambda b, h: (b, h, 0)),
        ],
        out_specs=pl.BlockSpec((seq, MLA_V), lambda b, h: (b, h)),
        out_shape=jax.ShapeDtypeStruct((rows, MLA_HEADS * MLA_V), BF16),
        scratch_shapes=[
            pltpu.VMEM((2, t, t), F32),
            pltpu.VMEM((2, 1, t), F32),
            pltpu.VMEM((FLASH_V_ROWS, t), F32),
            pltpu.VMEM((1, t), F32),
        ],
        compiler_params=_params("parallel", "parallel"),
        name="flash",
    )(q, k, vt)


def _merge_route_kernel(x_ref, oa_ref, ob_ref, gate_ref, wa_ref, wb_ref, wo_ref, nw_ref,
                        wr_hi_ref, wr_lo_ref, br_ref,
                        x1_ref, xn_ref, eid_ref, cw_ref, cnt_ref):
    @pl.when(pl.program_id(0) == 0)
    def _():
        cnt_ref[...] = jnp.zeros_like(cnt_ref)

    tm = x_ref.shape[0]
    ya = _dot(oa_ref[...], wa_ref[...])
    yb = _dot(ob_ref[...], wb_ref[...])
    ga = gate_ref[:, 0:D_MODEL].astype(F32)
    gb = gate_ref[:, D_MODEL:2 * D_MODEL].astype(F32)
    merged = jax.nn.sigmoid(ga) * ya + jax.nn.sigmoid(gb) * yb
    x1 = x_ref[...] + _dot(merged.astype(BF16), wo_ref[...])
    x1_ref[...] = x1
    ms = jnp.mean(x1 * x1, axis=-1, keepdims=True)
    xn = (x1 * lax.rsqrt(ms + EPS)) * nw_ref[...]
    xn_ref[...] = xn
    xn_hi, xn_lo = _split_bf16(xn)
    logits = (_dot(xn_hi, wr_hi_ref[...]) + _dot(xn_lo, wr_hi_ref[...])
              + _dot(xn_hi, wr_lo_ref[...])) + br_ref[...]
    lane = lax.broadcasted_iota(I32, (tm, LANES), 1)
    work = jnp.where(lane < N_EXPERTS, logits, -jnp.inf)
    vals, idxs = [], []
    for _ in range(TOP_K):
        m = jnp.max(work, axis=-1, keepdims=True)
        idx = jnp.min(jnp.where(work == m, lane, LANES), axis=-1, keepdims=True)
        vals.append(m)
        idxs.append(idx)
        work = jnp.where(lane == idx, -jnp.inf, work)
    exps = [jnp.exp(v - vals[0]) for v in vals]
    denom = exps[0] + exps[1] + exps[2] + exps[3]
    eid = jnp.zeros((tm, LANES), I32)
    cw = jnp.zeros((tm, LANES), F32)
    sel = jnp.zeros((tm, LANES), F32)
    for kk in range(TOP_K):
        eid = jnp.where(lane == kk, idxs[kk], eid)
        cw = jnp.where(lane == kk, exps[kk] / denom, cw)
        sel = sel + (lane == idxs[kk]).astype(F32)
    eid_ref[...] = eid
    cw_ref[...] = cw
    cnt_ref[0:1, :] = cnt_ref[0:1, :] + jnp.sum(sel, axis=0, keepdims=True)


def _merge_route(x2, oa, ob, gates, wa, wb, wo, nw, wr_hi, wr_lo, br, tm):
    t = x2.shape[0]
    const = lambda i: (0, 0)
    row = lambda i: (i, 0)
    return pl.pallas_call(
        _merge_route_kernel,
        grid=(t // tm,),
        in_specs=[
            pl.BlockSpec((tm, D_MODEL), row),
            pl.BlockSpec((tm, GLA_V_WIDTH), row),
            pl.BlockSpec((tm, MLA_HEADS * MLA_V), row),
            pl.BlockSpec((tm, _GATE_W), row),
            pl.BlockSpec((GLA_V_WIDTH, D_MODEL), const),
            pl.BlockSpec((MLA_HEADS * MLA_V, D_MODEL), const),
            pl.BlockSpec((D_MODEL, D_MODEL), const),
            pl.BlockSpec((1, D_MODEL), const),
            pl.BlockSpec((D_MODEL, LANES), const),
            pl.BlockSpec((D_MODEL, LANES), const),
            pl.BlockSpec((1, LANES), const),
        ],
        out_specs=[
            pl.BlockSpec((tm, D_MODEL), row),
            pl.BlockSpec((tm, D_MODEL), row),
            pl.BlockSpec((tm, LANES), row),
            pl.BlockSpec((tm, LANES), row),
            pl.BlockSpec((8, LANES), const),
        ],
        out_shape=[
            jax.ShapeDtypeStruct((t, D_MODEL), F32),
            jax.ShapeDtypeStruct((t, D_MODEL), F32),
            jax.ShapeDtypeStruct((t, LANES), I32),
            jax.ShapeDtypeStruct((t, LANES), F32),
            jax.ShapeDtypeStruct((8, LANES), F32),
        ],
        compiler_params=_params("arbitrary"),
        name="merge_route",
    )(x2, oa, ob, gates, wa, wb, wo, nw, wr_hi, wr_lo, br)


def _positions_kernel(eid_ref, cnt_ref, pos_ref, carry_ref, *, tile):
    @pl.when(pl.program_id(0) == 0)
    def _():
        carry_ref[...] = jnp.zeros_like(carry_ref)

    tb = eid_ref.shape[0]
    lane1 = lax.broadcasted_iota(I32, (1, LANES), 1)
    cnt = cnt_ref[0:1, :]
    padded = jnp.floor((cnt + (tile - 1)) / tile) * tile
    incl = padded
    shift = 1
    while shift < N_EXPERTS:
        incl = incl + jnp.where(lane1 >= shift, pltpu.roll(incl, shift, axis=1), 0.0)
        shift *= 2
    offs = incl - padded

    lane = lax.broadcasted_iota(I32, (tb, LANES), 1)
    eid = eid_ref[...]
    onehots = [lane == jnp.broadcast_to(eid[:, kk:kk + 1], (tb, LANES)) for kk in range(TOP_K)]
    sel = jnp.zeros((tb, LANES), F32)
    for oh in onehots:
        sel = sel + oh.astype(F32)
    row = lax.broadcasted_iota(I32, (tb, tb), 0)
    col = lax.broadcasted_iota(I32, (tb, tb), 1)
    strict = (row > col).astype(BF16)
    rank = _dot(strict, sel.astype(BF16)) + carry_ref[...] + offs
    pos = jnp.zeros((tb, LANES), I32)
    for kk in range(TOP_K):
        pk = jnp.sum(jnp.where(onehots[kk], rank, 0.0), axis=-1, keepdims=True)
        pos = jnp.where(lane == kk, pk.astype(I32), pos)
    pos_ref[...] = pos
    carry_ref[...] = carry_ref[...] + jnp.sum(sel, axis=0, keepdims=True)


def _positions(eid, cnt, tb, tile):
    t = eid.shape[0]
    return pl.pallas_call(
        functools.partial(_positions_kernel, tile=tile),
        grid=(t // tb,),
        in_specs=[
            pl.BlockSpec((tb, LANES), lambda i: (i, 0)),
            pl.BlockSpec((8, LANES), lambda i: (0, 0)),
        ],
        out_specs=pl.BlockSpec((tb, LANES), lambda i: (i, 0)),
        out_shape=jax.ShapeDtypeStruct((t, LANES), I32),
        scratch_shapes=[pltpu.VMEM((1, LANES), F32)],
        compiler_params=_params("arbitrary"),
        name="positions",
    )(eid, cnt)


ROW_DMA_UNROLL = 8


def _dispatch_kernel(tz_ref, pos_ref, xn_ref, xs_ref, zero_ref, sem, zsem, *, tile):
    i = pl.program_id(0)
    n = pl.num_programs(0)
    tb = pos_ref.shape[2] // TOP_K

    def zero_copy(e):
        start = pl.multiple_of(tz_ref[e], tile)
        return pltpu.make_async_copy(zero_ref, xs_ref.at[pl.ds(start, tile)], zsem)

    @pl.when(i == 0)
    def _():
        zero_ref[...] = jnp.zeros_like(zero_ref)
        for e in range(N_EXPERTS):
            @pl.when(tz_ref[e] >= 0)
            def _():
                zero_copy(e).start()
        for e in range(N_EXPERTS):
            @pl.when(tz_ref[e] >= 0)
            def _():
                zero_copy(e).wait()

    def issue(t, carry):
        for kk in range(TOP_K):
            p = pos_ref[0, 0, t * TOP_K + kk]
            pltpu.make_async_copy(xn_ref.at[pl.ds(i * tb + t, 1)], xs_ref.at[pl.ds(p, 1)],
                                  sem).start(priority=kk % 2)
        return carry

    lax.fori_loop(0, tb, issue, 0, unroll=ROW_DMA_UNROLL)

    def drain_block():
        for _ in range(TOP_K):
            pltpu.make_async_copy(xn_ref.at[pl.ds(0, tb)], xs_ref.at[pl.ds(0, tb)], sem).wait()

    @pl.when(i > 0)
    def _():
        drain_block()

    @pl.when(i == n - 1)
    def _():
        drain_block()


def _dispatch(tile_zero, pos3, xn, n_rows, tb, tile):
    t = xn.shape[0]
    return pl.pallas_call(
        functools.partial(_dispatch_kernel, tile=tile),
        grid_spec=pltpu.PrefetchScalarGridSpec(
            num_scalar_prefetch=1,
            grid=(t // tb,),
            in_specs=[
                pl.BlockSpec((1, 1, tb * TOP_K), lambda i, tz: (i, 0, 0), memory_space=pltpu.SMEM),
                pl.BlockSpec(memory_space=pl.ANY),
            ],
            out_specs=pl.BlockSpec(memory_space=pl.ANY),
            scratch_shapes=[pltpu.VMEM((tile, D_MODEL), F32), pltpu.SemaphoreType.DMA,
                            pltpu.SemaphoreType.DMA],
        ),
        out_shape=jax.ShapeDtypeStruct((n_rows, D_MODEL), F32),
        compiler_params=_params("arbitrary"),
        name="dispatch",
    )(tile_zero, pos3, xn)


def _experts_kernel(te_ref, nv_ref, xs_ref, wg_ref, bg_ref, wu_ref, bu_ref, wd_ref, bd_ref, ys_ref,
                    wgb_ref, wub_ref, wdb_ref):
    i = pl.program_id(0)
    valid = i < nv_ref[0]
    new_expert = jnp.logical_or(i == 0, te_ref[i] != te_ref[jnp.maximum(i - 1, 0)])

    @pl.when(jnp.logical_and(valid, new_expert))
    def _():
        wgb_ref[...] = wg_ref[0].astype(BF16)
        wub_ref[...] = wu_ref[0].astype(BF16)
        wdb_ref[...] = wd_ref[0].astype(BF16)

    @pl.when(valid)
    def _():
        xb = xs_ref[...].astype(BF16)
        g = _dot(xb, wgb_ref[...]) + bg_ref[0]
        u = _dot(xb, wub_ref[...]) + bu_ref[0]
        g = jnp.minimum(g, SWIGLU_LIMIT)
        u = jnp.clip(u, -SWIGLU_LIMIT, SWIGLU_LIMIT)
        hidden = (u + 1.0) * g * jax.nn.sigmoid(SWIGLU_ALPHA * g)
        ys_ref[...] = _dot(hidden.astype(BF16), wdb_ref[...]) + bd_ref[0]


def _experts(tile_expert, n_valid, xs, wg, bg, wu, bu, wd, bd, tile):
    rows = xs.shape[0]
    nt = rows // tile
    rmap = lambda i, te, nv: (jnp.minimum(i, nv[0] - 1), 0)
    wmap = lambda i, te, nv: (te[i], 0, 0)
    return pl.pallas_call(
        _experts_kernel,
        grid_spec=pltpu.PrefetchScalarGridSpec(
            num_scalar_prefetch=2,
            grid=(nt,),
            in_specs=[
                pl.BlockSpec((tile, D_MODEL), rmap),
                pl.BlockSpec((1, D_MODEL, D_FF), wmap),
                pl.BlockSpec((1, 1, D_FF), wmap),
                pl.BlockSpec((1, D_MODEL, D_FF), wmap),
                pl.BlockSpec((1, 1, D_FF), wmap),
                pl.BlockSpec((1, D_FF, D_MODEL), wmap),
                pl.BlockSpec((1, 1, D_MODEL), wmap),
            ],
            out_specs=pl.BlockSpec((tile, D_MODEL), rmap),
            scratch_shapes=[pltpu.VMEM((D_MODEL, D_FF), BF16), pltpu.VMEM((D_MODEL, D_FF), BF16),
                            pltpu.VMEM((D_FF, D_MODEL), BF16)],
        ),
        out_shape=jax.ShapeDtypeStruct((rows, D_MODEL), F32),
        compiler_params=_params("arbitrary"),
        name="experts",
    )(tile_expert, n_valid, xs, wg, bg, wu, bu, wd, bd)


def _combine_kernel(pos_ref, posn_ref, x1_ref, cw_ref, ys_ref, o_ref, buf_ref, sem0, sem1):
    i = pl.program_id(0)
    n = pl.num_programs(0)
    tb = x1_ref.shape[0] // 2

    def issue(p_ref, sub, slot, sem):
        def body(t, carry):
            for kk in range(TOP_K):
                p = p_ref[0, 0, (sub * tb + t) * TOP_K + kk]
                pltpu.make_async_copy(ys_ref.at[pl.ds(p, 1)], buf_ref.at[slot, kk, pl.ds(t, 1)],
                                      sem).start(priority=kk % 2)
            return carry
        lax.fori_loop(0, tb, body, 0, unroll=ROW_DMA_UNROLL)

    def drain(slot, sem):
        for kk in range(TOP_K):
            pltpu.make_async_copy(ys_ref.at[pl.ds(0, tb)], buf_ref.at[slot, kk], sem).wait()

    def reduce(sub, slot):
        rows = slice(sub * tb, (sub + 1) * tb)
        cw = cw_ref[rows, :]
        acc = x1_ref[rows, :]
        for kk in range(TOP_K):
            acc = acc + cw[:, kk:kk + 1] * buf_ref[slot, kk]
        o_ref[rows, :] = acc

    @pl.when(i == 0)
    def _():
        issue(pos_ref, 0, 0, sem0)

    issue(pos_ref, 1, 1, sem1)
    drain(0, sem0)
    reduce(0, 0)

    @pl.when(i + 1 < n)
    def _():
        issue(posn_ref, 0, 0, sem0)

    drain(1, sem1)
    reduce(1, 1)


def _combine(pos3, x1, cw, ys, tb):
    t = x1.shape[0]
    n = t // (2 * tb)
    return pl.pallas_call(
        _combine_kernel,
        grid=(n,),
        in_specs=[
            pl.BlockSpec((1, 1, 2 * tb * TOP_K), lambda i: (i, 0, 0), memory_space=pltpu.SMEM),
            pl.BlockSpec((1, 1, 2 * tb * TOP_K), lambda i: (jnp.minimum(i + 1, n - 1), 0, 0),
                         memory_space=pltpu.SMEM),
            pl.BlockSpec((2 * tb, D_MODEL), lambda i: (i, 0)),
            pl.BlockSpec((2 * tb, LANES), lambda i: (i, 0)),
            pl.BlockSpec(memory_space=pl.ANY),
        ],
        out_specs=pl.BlockSpec((2 * tb, D_MODEL), lambda i: (i, 0)),
        out_shape=jax.ShapeDtypeStruct((t, D_MODEL), F32),
        scratch_shapes=[pltpu.VMEM((2, TOP_K, tb, D_MODEL), F32), pltpu.SemaphoreType.DMA,
                        pltpu.SemaphoreType.DMA],
        compiler_params=_params("arbitrary"),
        name="combine",
    )(pos3, pos3, x1, cw, ys)


def _pad_cols(a, width):
    return jnp.pad(a, ((0, 0), (0, width - a.shape[1])))


def _pack_in_proj(w_in):
    o = np.cumsum((GLA_QK_WIDTH, GLA_QK_WIDTH, GLA_V_WIDTH, GLA_V_WIDTH, GLA_GATE_RANK,
                   MLA_Q_RANK, MLA_KV_RANK, MLA_ROPE, D_MODEL, D_MODEL)).tolist()
    gla = w_in[:, 0:o[3]]
    lr = _pad_cols(w_in[:, o[3]:o[4]], _LR_W)
    mla = _pad_cols(w_in[:, o[4]:o[7]], _MLA_W)
    gates = w_in[:, o[7]:o[9]]
    return jnp.concatenate([gla, lr, mla, gates], axis=1).astype(BF16)


def _pad_heads(w, head_w):
    r = w.shape[0]
    w3 = w.reshape(r, MLA_HEADS, head_w)
    w3 = jnp.pad(w3, ((0, 0), (0, 0), (0, MLA_HEAD_PAD - head_w)))
    return w3.reshape(r, MLA_HEADS * MLA_HEAD_PAD)


def _tile_rows(n, pref):
    return pref if n % pref == 0 else n


def kernel(x, positions, attn_norm_w, w_in, w_gla_gk, b_gla_gk, gla_out_norm_w, w_gla_out,
           mla_q_norm_w, w_mla_uq, mla_kv_norm_w, w_mla_ukv, mla_qk_q_norm_w, mla_qk_k_norm_w,
           w_mla_out, w_out, moe_norm_w, w_router, b_router, w_exp_gate, b_exp_gate,
           w_exp_up, b_exp_up, w_exp_down, b_exp_down):
    batch, seq, _ = x.shape
    depth = w_in.shape[0]
    t = batch * seq
    x2 = x.reshape(t, D_MODEL)
    pos = positions.reshape(t, 1).astype(F32)
    half = MLA_ROPE // 2
    inv_freq = ROPE_BASE ** (-jnp.arange(half, dtype=F32) / half)
    freq = _pad_cols(jnp.concatenate([inv_freq, inv_freq])[None, :], LANES)

    tm = _tile_rows(t, 512)
    gla_ts = _tile_rows(seq, 256)
    flash_t = _tile_rows(seq, 512)
    moe_tile = 512
    n_rows = t * TOP_K + N_EXPERTS * moe_tile
    n_rows = (n_rows // moe_tile) * moe_tile
    disp_tb = _tile_rows(t, 512)
    comb_tb = _tile_rows(t, 256)
    pos_tb = _tile_rows(t, 256)

    for l in range(depth):
        w_pack = _pack_in_proj(w_in[l])
        wgk = jnp.pad(w_gla_gk[l], ((0, LANES - GLA_GATE_RANK), (0, 0)))
        wgk_hi, wgk_lo = _split_bf16(wgk)
        gla, la, mla, gates = _inproj(x2, attn_norm_w[l][None, :], w_pack, wgk_hi, wgk_lo,
                                      b_gla_gk[l][None, :], tm)

        o_a = _gla(gla, la, gla_out_norm_w[l][None, :], batch, seq, gla_ts)

        wuq = _pad_heads(w_mla_uq[l], MLA_QK).astype(BF16)
        wukv = w_mla_ukv[l].astype(BF16)
        qw = _pad_cols(mla_qk_q_norm_w[l][None, :], MLA_HEAD_PAD)
        kw = _pad_cols(mla_qk_k_norm_w[l][None, :], MLA_HEAD_PAD)
        q, k, vt = _mla_prep(mla, pos, wuq, wukv, mla_q_norm_w[l][None, :],
                             mla_kv_norm_w[l][None, :], qw, kw, freq, batch, seq,
                             _tile_rows(seq, 512))
        o_b = _flash(q, k, vt, batch, seq, flash_t)

        wr = _pad_cols(w_router[l], LANES)
        wr_hi, wr_lo = _split_bf16(wr)
        br = _pad_cols(b_router[l][None, :], LANES)
        x1, xn, eid, cw, cnt = _merge_route(
            x2, o_a, o_b, gates, w_gla_out[l].astype(BF16), w_mla_out[l].astype(BF16),
            w_out[l].astype(BF16), moe_norm_w[l][None, :], wr_hi, wr_lo, br, tm)

        posn = _positions(eid, cnt, pos_tb, moe_tile)
        pos4 = posn[:, :TOP_K]
        counts = cnt[0, :N_EXPERTS].astype(I32)
        padded = ((counts + moe_tile - 1) // moe_tile) * moe_tile
        ends = jnp.cumsum(padded)
        n_tiles = n_rows // moe_tile
        n_valid = (ends[-1] // moe_tile).astype(I32)
        starts = jnp.arange(n_tiles, dtype=I32) * moe_tile
        starts = jnp.minimum(starts, ends[-1] - moe_tile)
        tile_expert = jnp.minimum(jnp.sum((starts[:, None] >= ends[None, :]).astype(I32), axis=1),
                                  N_EXPERTS - 1)
        tile_zero = jnp.where(padded > 0, ends - moe_tile, -1).astype(I32)

        xs = _dispatch(tile_zero, pos4.reshape(t // disp_tb, 1, disp_tb * TOP_K), xn, n_rows,
                       disp_tb, moe_tile)
        ys = _experts(tile_expert, n_valid.reshape(1), xs,
                      w_exp_gate[l], b_exp_gate[l][:, None, :],
                      w_exp_up[l], b_exp_up[l][:, None, :],
                      w_exp_down[l], b_exp_down[l][:, None, :], moe_tile)
        x2 = _combine(pos4.reshape(t // (2 * comb_tb), 1, 2 * comb_tb * TOP_K), x1, cw, ys, comb_tb)
    return x2.reshape(batch, seq, D_MODEL)
```

```python
import functools

import jax
import jax.numpy as jnp
import numpy as np
from jax import lax
from jax.experimental import pallas as pl
from jax.experimental.pallas import tpu as pltpu

F32 = jnp.float32
BF16 = jnp.bfloat16
I32 = jnp.int32

D_MODEL = 1024
EPS = 1e-6
GLA_HEADS = 4
GLA_DK = 128
GLA_DV = 256
GLA_GATE_RANK = 16
GLA_GATE_NORMALIZER = 16.0
GLA_QK_WIDTH = GLA_HEADS * GLA_DK
GLA_V_WIDTH = GLA_HEADS * GLA_DV
MLA_HEADS = 8
MLA_Q_RANK = 384
MLA_KV_RANK = 256
MLA_NOPE = 128
MLA_ROPE = 64
MLA_V = 128
MLA_QK = MLA_NOPE + MLA_ROPE
ROPE_BASE = 10000.0
N_EXPERTS = 32
TOP_K = 4
D_FF = 1024
SWIGLU_LIMIT = 7.0
SWIGLU_ALPHA = 1.702

LANES = 128
MLA_HEAD_PAD = 256
GLA_CHUNK = 128
FLASH_V_ROWS = 144
LOG2_E = 1.4426950408889634
VMEM_LIMIT = 56 * 1024 * 1024

_GLA_W = 2 * GLA_QK_WIDTH + 2 * GLA_V_WIDTH
_LR_W = LANES
_MLA_W = 768
_GATE_W = 2 * D_MODEL
_PACK_W = _GLA_W + _LR_W + _MLA_W + _GATE_W


def _dot(a, b):
    return jnp.dot(a, b, preferred_element_type=F32)


def _dot_nt(a, b):
    return lax.dot_general(a, b, (((1,), (1,)), ((), ())), preferred_element_type=F32)


def _dot_tn(a, b):
    return lax.dot_general(a, b, (((0,), (0,)), ((), ())), preferred_element_type=F32)


def _split_bf16(a):
    hi = a.astype(BF16)
    lo = (a - hi.astype(F32)).astype(BF16)
    return hi, lo


def _params(*sem):
    return pltpu.CompilerParams(dimension_semantics=sem, vmem_limit_bytes=VMEM_LIMIT)


def _inproj_kernel(x_ref, nw_ref, w_ref, wgk_hi_ref, wgk_lo_ref, bgk_ref,
                   gla_ref, la_ref, mla_ref, gate_ref):
    x = x_ref[...]
    ms = jnp.mean(x * x, axis=-1, keepdims=True)
    h = (x * lax.rsqrt(ms + EPS)) * nw_ref[...]
    hb = h.astype(BF16)
    o0 = 0
    q = _dot(hb, w_ref[:, 0:GLA_QK_WIDTH]) * (GLA_DK ** -0.5)
    gla_ref[:, 0:GLA_QK_WIDTH] = q.astype(BF16)
    gla_ref[:, GLA_QK_WIDTH:_GLA_W] = _dot(hb, w_ref[:, GLA_QK_WIDTH:_GLA_W]).astype(BF16)
    o0 = _GLA_W
    lr = _dot(hb, w_ref[:, o0:o0 + _LR_W])
    lr_hi, lr_lo = _split_bf16(lr)
    a_logit = (_dot(lr_hi, wgk_hi_ref[...]) + _dot(lr_lo, wgk_hi_ref[...])
               + _dot(lr_hi, wgk_lo_ref[...])) + bgk_ref[...]
    log_sig = jnp.minimum(a_logit, 0.0) - jnp.log1p(jnp.exp(-jnp.abs(a_logit)))
    la_ref[...] = log_sig / GLA_GATE_NORMALIZER
    o0 += _LR_W
    mla_ref[...] = _dot(hb, w_ref[:, o0:o0 + _MLA_W]).astype(BF16)
    o0 += _MLA_W
    gate_ref[...] = _dot(hb, w_ref[:, o0:o0 + _GATE_W]).astype(BF16)


def _inproj(x2, nw, w_pack, wgk_hi, wgk_lo, bgk, tm):
    t = x2.shape[0]
    const = lambda i: (0, 0)
    row = lambda i: (i, 0)
    return pl.pallas_call(
        _inproj_kernel,
        grid=(t // tm,),
        in_specs=[
            pl.BlockSpec((tm, D_MODEL), row),
            pl.BlockSpec((1, D_MODEL), const),
            pl.BlockSpec((D_MODEL, _PACK_W), const, pipeline_mode=pl.Buffered(1)),
            pl.BlockSpec((LANES, GLA_QK_WIDTH), const),
            pl.BlockSpec((LANES, GLA_QK_WIDTH), const),
            pl.BlockSpec((1, GLA_QK_WIDTH), const),
        ],
        out_specs=[
            pl.BlockSpec((tm, _GLA_W), row),
            pl.BlockSpec((tm, GLA_QK_WIDTH), row),
            pl.BlockSpec((tm, _MLA_W), row),
            pl.BlockSpec((tm, _GATE_W), row),
        ],
        out_shape=[
            jax.ShapeDtypeStruct((t, _GLA_W), BF16),
            jax.ShapeDtypeStruct((t, GLA_QK_WIDTH), F32),
            jax.ShapeDtypeStruct((t, _MLA_W), BF16),
            jax.ShapeDtypeStruct((t, _GATE_W), BF16),
        ],
        compiler_params=_params("parallel"),
        name="inproj",
    )(x2, nw, w_pack, wgk_hi, wgk_lo, bgk)


def _gla_kernel(q_ref, k_ref, v_ref, g_ref, la_ref, nw_ref, o_ref, state_ref, *, n_sub):
    @pl.when(pl.program_id(1) == 0)
    def _():
        state_ref[...] = jnp.zeros_like(state_ref)

    c = GLA_CHUNK
    row = lax.broadcasted_iota(I32, (c, c), 0)
    col = lax.broadcasted_iota(I32, (c, c), 1)
    causal = row >= col
    tri = causal.astype(BF16)
    mid = c // 2 - 1
    for s in range(n_sub):
        r0 = s * c
        la = la_ref[r0:r0 + c, :]
        la_hi, la_lo = _split_bf16(la)
        cum_all = _dot(tri, la_hi) + _dot(tri, la_lo)
        for h in range(GLA_HEADS):
            ks = slice(h * GLA_DK, (h + 1) * GLA_DK)
            vs = slice(h * GLA_DV, (h + 1) * GLA_DV)
            cum = cum_all[:, ks]
            q = q_ref[r0:r0 + c, ks].astype(F32)
            k = k_ref[r0:r0 + c, ks].astype(F32)
            v = v_ref[r0:r0 + c, vs]
            ref_row = cum[mid:mid + 1, :]
            last = cum[c - 1:c, :]
            qg = (q * jnp.exp(cum - ref_row)).astype(BF16)
            kg = (k * jnp.exp(ref_row - cum)).astype(BF16)
            scores = jnp.where(causal, _dot_nt(qg, kg), 0.0).astype(BF16)
            o = _dot(scores, v)
            st = state_ref[h]
            qe = (q * jnp.exp(cum)).astype(BF16)
            o = o + _dot_nt(qe, st.astype(BF16))
            ko = (k * jnp.exp(last - cum)).astype(BF16)
            state_ref[h] = st * jnp.exp(last) + _dot_tn(v, ko)
            ms = jnp.mean(o * o, axis=-1, keepdims=True)
            on = (o * lax.rsqrt(ms + EPS)) * nw_ref[...]
            g = g_ref[r0:r0 + c, vs].astype(F32)
            o_ref[r0:r0 + c, vs] = (on * (g * jax.nn.sigmoid(g))).astype(BF16)


def _gla(gla, la, nw, batch, seq, ts):
    t = gla.shape[0]
    nblk = seq // ts
    kern = functools.partial(_gla_kernel, n_sub=ts // GLA_CHUNK)
    rowmap = lambda col: (lambda b, i: (b * nblk + i, col))
    return pl.pallas_call(
        kern,
        grid=(batch, nblk),
        in_specs=[
            pl.BlockSpec((ts, GLA_QK_WIDTH), rowmap(0)),
            pl.BlockSpec((ts, GLA_QK_WIDTH), rowmap(1)),
            pl.BlockSpec((ts, GLA_V_WIDTH), rowmap(1)),
            pl.BlockSpec((ts, GLA_V_WIDTH), rowmap(2)),
            pl.BlockSpec((ts, GLA_QK_WIDTH), rowmap(0)),
            pl.BlockSpec((1, GLA_DV), lambda b, i: (0, 0)),
        ],
        out_specs=pl.BlockSpec((ts, GLA_V_WIDTH), rowmap(0)),
        out_shape=jax.ShapeDtypeStruct((t, GLA_V_WIDTH), BF16),
        scratch_shapes=[pltpu.VMEM((GLA_HEADS, GLA_DV, GLA_DK), F32)],
        compiler_params=_params("parallel", "arbitrary"),
        name="gla",
    )(gla, gla, gla, gla, la, nw)


def _rope(x, cos, sin, lane):
    half = MLA_ROPE // 2
    rot = jnp.where(lane < half, -pltpu.roll(x, LANES - half, axis=1), pltpu.roll(x, half, axis=1))
    return x * cos + rot * sin


def _mla_prep_kernel(mla_ref, pos_ref, wuq_ref, wuk_ref, wvt_ref, vone_ref, qnw_ref, kvnw_ref,
                     qw_ref, kw_ref, freq_ref, q_ref, k_ref, vt_ref):
    tm = mla_ref.shape[0]
    cq = mla_ref[:, 0:MLA_Q_RANK].astype(F32)
    ckv = mla_ref[:, MLA_Q_RANK:MLA_Q_RANK + MLA_KV_RANK].astype(F32)
    kr = mla_ref[:, MLA_Q_RANK + MLA_KV_RANK:_MLA_W].astype(F32)

    def rms(a, w):
        ms = jnp.mean(a * a, axis=-1, keepdims=True)
        return (a * lax.rsqrt(ms + EPS)) * w

    q_all = _dot(rms(cq, qnw_ref[...]).astype(BF16), wuq_ref[...])
    ckvn = rms(ckv, kvnw_ref[...]).astype(BF16)
    k_all = _dot(ckvn, wuk_ref[...])
    vt_ref[0] = (_dot_nt(wvt_ref[...], ckvn) + vone_ref[:, 0:1]).astype(BF16)

    half = MLA_ROPE // 2
    ang_t = freq_ref[...] * pos_ref[...]
    cos_t = jnp.cos(ang_t)
    sin_t = jnp.sin(ang_t)
    cos = jnp.concatenate([cos_t, cos_t, jnp.ones((LANES - 2 * half, tm), F32)], axis=0).T
    sin = jnp.concatenate([sin_t, sin_t, jnp.zeros((LANES - 2 * half, tm), F32)], axis=0).T
    lane = lax.broadcasted_iota(I32, (tm, LANES), 1)
    qw_nope = qw_ref[:, 0:MLA_NOPE]
    qw_rope = qw_ref[:, MLA_NOPE:MLA_HEAD_PAD]
    kw_nope = kw_ref[:, 0:MLA_NOPE]
    kw_rope = kw_ref[:, MLA_NOPE:MLA_HEAD_PAD]
    kr_sq = kr * kr
    kr_rot = _rope(kr * kw_rope, cos, sin, lane)
    scale = (MLA_QK ** -0.5) * LOG2_E
    for h in range(MLA_HEADS):
        base = h * MLA_HEAD_PAD
        qn = q_all[:, base:base + MLA_NOPE]
        qr = q_all[:, base + MLA_NOPE:base + MLA_HEAD_PAD]
        ss = jnp.sum(qn * qn + qr * qr, axis=-1, keepdims=True)
        r = lax.rsqrt(ss / MLA_QK + EPS)
        q_ref[:, base:base + MLA_NOPE] = ((qn * r) * qw_nope * scale).astype(BF16)
        q_ref[:, base + MLA_NOPE:base + MLA_HEAD_PAD] = (
            _rope((qr * r) * qw_rope, cos, sin, lane) * scale).astype(BF16)
        kn = k_all[:, h * MLA_NOPE:(h + 1) * MLA_NOPE]
        ssk = jnp.sum(kn * kn + kr_sq, axis=-1, keepdims=True)
        rk = lax.rsqrt(ssk / MLA_QK + EPS)
        k_ref[:, base:base + MLA_NOPE] = ((kn * rk) * kw_nope).astype(BF16)
        k_ref[:, base + MLA_NOPE:base + MLA_HEAD_PAD] = (kr_rot * rk).astype(BF16)


def _mla_prep(mla, pos, wuq, wuk, wvt, vone, qnw, kvnw, qw, kw, freq, batch, seq, tm):
    t = mla.shape[0]
    nblk = seq // tm
    const = lambda i: (0, 0)
    row = lambda i: (i, 0)
    hw = MLA_HEADS * MLA_HEAD_PAD
    return pl.pallas_call(
        _mla_prep_kernel,
        grid=(t // tm,),
        in_specs=[
            pl.BlockSpec((tm, _MLA_W), row),
            pl.BlockSpec((1, tm), lambda i: (0, i)),
            pl.BlockSpec((MLA_Q_RANK, hw), const),
            pl.BlockSpec((MLA_KV_RANK, MLA_HEADS * MLA_NOPE), const),
            pl.BlockSpec((MLA_HEADS * FLASH_V_ROWS, MLA_KV_RANK), const),
            pl.BlockSpec((MLA_HEADS * FLASH_V_ROWS, 1), const),
            pl.BlockSpec((1, MLA_Q_RANK), const),
            pl.BlockSpec((1, MLA_KV_RANK), const),
            pl.BlockSpec((1, MLA_HEAD_PAD), const),
            pl.BlockSpec((1, MLA_HEAD_PAD), const),
            pl.BlockSpec((MLA_ROPE // 2, 1), const),
        ],
        out_specs=[
            pl.BlockSpec((tm, hw), row),
            pl.BlockSpec((tm, hw), row),
            pl.BlockSpec((1, MLA_HEADS * FLASH_V_ROWS, tm), lambda i: (i // nblk, 0, i % nblk)),
        ],
        out_shape=[
            jax.ShapeDtypeStruct((t, hw), BF16),
            jax.ShapeDtypeStruct((t, hw), BF16),
            jax.ShapeDtypeStruct((batch, MLA_HEADS * FLASH_V_ROWS, seq), BF16),
        ],
        compiler_params=_params("parallel"),
        name="mla_prep",
    )(mla, pos, wuq, wuk, wvt, vone, qnw, kvnw, qw, kw, freq)


def _flash_kernel(q_ref, k_ref, vt_ref, o_ref, s_ref, cm_ref, acc_ref, m_ref, *, t):
    seq = q_ref.shape[0]
    nq = seq // t
    krow = lax.broadcasted_iota(I32, (t, t), 0)
    qcol = lax.broadcasted_iota(I32, (t, t), 1)
    items = [(qi, j) for qi in range(nq) for j in range(qi + 1)]

    def scores(idx):
        qi, j = items[idx]
        s = _dot_nt(k_ref[j * t:(j + 1) * t, :], q_ref[qi * t:(qi + 1) * t, :])
        s_ref[idx % 2] = s
        cm_ref[idx % 2] = jnp.max(s, axis=0, keepdims=True)

    def update(idx):
        qi, j = items[idx]
        if j == 0:
            m_ref[...] = jnp.full_like(m_ref, -jnp.inf)
            acc_ref[...] = jnp.zeros_like(acc_ref)
        s = s_ref[idx % 2]
        if j == qi:
            s = jnp.where(krow <= qcol, s, -jnp.inf)
            cm = jnp.max(s, axis=0, keepdims=True)
        else:
            cm = cm_ref[idx % 2]
        m_prev = m_ref[...]
        m_new = jnp.maximum(m_prev, cm)
        alpha = jnp.exp2(m_prev - m_new)
        p = jnp.exp2(s - m_new).astype(BF16)
        m_ref[...] = m_new
        acc_ref[...] = alpha * acc_ref[...] + _dot(vt_ref[0, :, j * t:(j + 1) * t], p)
        if j == qi:
            acc = acc_ref[...]
            out_t = acc[0:MLA_V, :] * (1.0 / acc[MLA_V:MLA_V + 1, :])
            o_ref[qi * t:(qi + 1) * t, :] = out_t.T.astype(BF16)

    scores(0)
    for idx in range(len(items)):
        if idx + 1 < len(items):
            scores(idx + 1)
        update(idx)


def _flash(q, k, vt, batch, seq, t):
    rows = q.shape[0]
    return pl.pallas_call(
        functools.partial(_flash_kernel, t=t),
        grid=(batch, MLA_HEADS),
        in_specs=[
            pl.BlockSpec((seq, MLA_HEAD_PAD), lambda b, h: (b, h)),
            pl.BlockSpec((seq, MLA_HEAD_PAD), lambda b, h: (b, h)),
            pl.BlockSpec((1, FLASH_V_ROWS, seq), lambda b, h: (b, h, 0)),
        ],
        out_specs=pl.BlockSpec((seq, MLA_V), lambda b, h: (b, h)),
        out_shape=jax.ShapeDtypeStruct((rows, MLA_HEADS * MLA_V), BF16),
        scratch_shapes=[
            pltpu.VMEM((2, t, t), F32),
            pltpu.VMEM((2, 1, t), F32),
            pltpu.VMEM((FLASH_V_ROWS, t), F32),
            pltpu.VMEM((1, t), F32),
        ],
        compiler_params=_params("parallel", "parallel"),
        name="flash",
    )(q, k, vt)


def _merge_route_kernel(x_ref, oa_ref, ob_ref, gate_ref, wa_ref, wb_ref, wo_ref, nw_ref,
                        wr_hi_ref, wr_lo_ref, br_ref,
                        x1_ref, xn_ref, eid_ref, cw_ref, cnt_ref):
    @pl.when(pl.program_id(0) == 0)
    def _():
        cnt_ref[...] = jnp.zeros_like(cnt_ref)

    tm = x_ref.shape[0]
    ya = _dot(oa_ref[...], wa_ref[...])
    yb = _dot(ob_ref[...], wb_ref[...])
    ga = gate_ref[:, 0:D_MODEL].astype(F32)
    gb = gate_ref[:, D_MODEL:2 * D_MODEL].astype(F32)
    merged = jax.nn.sigmoid(ga) * ya + jax.nn.sigmoid(gb) * yb
    x1 = x_ref[...] + _dot(merged.astype(BF16), wo_ref[...])
    x1_ref[...] = x1
    ms = jnp.mean(x1 * x1, axis=-1, keepdims=True)
    xn = (x1 * lax.rsqrt(ms + EPS)) * nw_ref[...]
    xn_ref[...] = xn
    xn_hi, xn_lo = _split_bf16(xn)
    logits = (_dot(xn_hi, wr_hi_ref[...]) + _dot(xn_lo, wr_hi_ref[...])
              + _dot(xn_hi, wr_lo_ref[...])) + br_ref[...]
    lane = lax.broadcasted_iota(I32, (tm, LANES), 1)
    work = jnp.where(lane < N_EXPERTS, logits, -jnp.inf)
    vals, idxs = [], []
    for _ in range(TOP_K):
        m = jnp.max(work, axis=-1, keepdims=True)
        idx = jnp.min(jnp.where(work == m, lane, LANES), axis=-1, keepdims=True)
        vals.append(m)
        idxs.append(idx)
        work = jnp.where(lane == idx, -jnp.inf, work)
    exps = [jnp.exp(v - vals[0]) for v in vals]
    denom = exps[0] + exps[1] + exps[2] + exps[3]
    eid = jnp.zeros((tm, LANES), I32)
    cw = jnp.zeros((tm, LANES), F32)
    sel = jnp.zeros((tm, LANES), F32)
    for kk in range(TOP_K):
        eid = jnp.where(lane == kk, idxs[kk], eid)
        cw = jnp.where(lane == kk, exps[kk] / denom, cw)
        sel = sel + (lane == idxs[kk]).astype(F32)
    eid_ref[...] = eid
    cw_ref[...] = cw
    cnt_ref[0:1, :] = cnt_ref[0:1, :] + jnp.sum(sel, axis=0, keepdims=True)


def _merge_route(x2, oa, ob, gates, wa, wb, wo, nw, wr_hi, wr_lo, br, tm):
    t = x2.shape[0]
    const = lambda i: (0, 0)
    row = lambda i: (i, 0)
    return pl.pallas_call(
        _merge_route_kernel,
        grid=(t // tm,),
        in_specs=[
            pl.BlockSpec((tm, D_MODEL), row),
            pl.BlockSpec((tm, GLA_V_WIDTH), row),
            pl.BlockSpec((tm, MLA_HEADS * MLA_V), row),
            pl.BlockSpec((tm, _GATE_W), row),
            pl.BlockSpec((GLA_V_WIDTH, D_MODEL), const),
            pl.BlockSpec((MLA_HEADS * MLA_V, D_MODEL), const),
            pl.BlockSpec((D_MODEL, D_MODEL), const),
            pl.BlockSpec((1, D_MODEL), const),
            pl.BlockSpec((D_MODEL, LANES), const),
            pl.BlockSpec((D_MODEL, LANES), const),
            pl.BlockSpec((1, LANES), const),
        ],
        out_specs=[
            pl.BlockSpec((tm, D_MODEL), row),
            pl.BlockSpec((tm, D_MODEL), row),
            pl.BlockSpec((tm, LANES), row),
            pl.BlockSpec((tm, LANES), row),
            pl.BlockSpec((8, LANES), const),
        ],
        out_shape=[
            jax.ShapeDtypeStruct((t, D_MODEL), F32),
            jax.ShapeDtypeStruct((t, D_MODEL), F32),
            jax.ShapeDtypeStruct((t, LANES), I32),
            jax.ShapeDtypeStruct((t, LANES), F32),
            jax.ShapeDtypeStruct((8, LANES), F32),
        ],
        compiler_params=_params("arbitrary"),
        name="merge_route",
    )(x2, oa, ob, gates, wa, wb, wo, nw, wr_hi, wr_lo, br)


def _positions_kernel(eid_ref, cnt_ref, pos_ref, carry_ref, *, tile):
    @pl.when(pl.program_id(0) == 0)
    def _():
        carry_ref[...] = jnp.zeros_like(carry_ref)

    tb = eid_ref.shape[0]
    lane1 = lax.broadcasted_iota(I32, (1, LANES), 1)
    cnt = cnt_ref[0:1, :]
    padded = jnp.floor((cnt + (tile - 1)) / tile) * tile
    incl = padded
    shift = 1
    while shift < N_EXPERTS:
        incl = incl + jnp.where(lane1 >= shift, pltpu.roll(incl, shift, axis=1), 0.0)
        shift *= 2
    offs = incl - padded

    lane = lax.broadcasted_iota(I32, (tb, LANES), 1)
    eid = eid_ref[...]
    onehots = [lane == jnp.broadcast_to(eid[:, kk:kk + 1], (tb, LANES)) for kk in range(TOP_K)]
    sel = jnp.zeros((tb, LANES), F32)
    for oh in onehots:
        sel = sel + oh.astype(F32)
    row = lax.broadcasted_iota(I32, (tb, tb), 0)
    col = lax.broadcasted_iota(I32, (tb, tb), 1)
    strict = (row > col).astype(BF16)
    rank = _dot(strict, sel.astype(BF16)) + carry_ref[...] + offs
    pos = jnp.zeros((tb, LANES), I32)
    for kk in range(TOP_K):
        pk = jnp.sum(jnp.where(onehots[kk], rank, 0.0), axis=-1, keepdims=True)
        pos = jnp.where(lane == kk, pk.astype(I32), pos)
    pos_ref[...] = pos
    carry_ref[...] = carry_ref[...] + jnp.sum(sel, axis=0, keepdims=True)


def _positions(eid, cnt, tb, tile):
    t = eid.shape[0]
    return pl.pallas_call(
        functools.partial(_positions_kernel, tile=tile),
        grid=(t // tb,),
        in_specs=[
            pl.BlockSpec((tb, LANES), lambda i: (i, 0)),
            pl.BlockSpec((8, LANES), lambda i: (0, 0)),
        ],
        out_specs=pl.BlockSpec((tb, LANES), lambda i: (i, 0)),
        out_shape=jax.ShapeDtypeStruct((t, LANES), I32),
        scratch_shapes=[pltpu.VMEM((1, LANES), F32)],
        compiler_params=_params("arbitrary"),
        name="positions",
    )(eid, cnt)


ROW_DMA_UNROLL = 8


def _dispatch_kernel(tz_ref, pos_ref, xn_ref, xs_ref, zero_ref, sem, zsem, *, tile):
    i = pl.program_id(0)
    tb = xn_ref.shape[0]

    def zero_copy(e):
        start = pl.multiple_of(tz_ref[e], tile)
        return pltpu.make_async_copy(zero_ref, xs_ref.at[pl.ds(start, tile)], zsem)

    @pl.when(i == 0)
    def _():
        zero_ref[...] = jnp.zeros_like(zero_ref)
        for e in range(N_EXPERTS):
            @pl.when(tz_ref[e] >= 0)
            def _():
                zero_copy(e).start()
        for e in range(N_EXPERTS):
            @pl.when(tz_ref[e] >= 0)
            def _():
                zero_copy(e).wait()

    def issue(t, carry):
        for kk in range(TOP_K):
            p = pos_ref[0, 0, t * TOP_K + kk]
            pltpu.make_async_copy(xn_ref.at[pl.ds(t, 1)], xs_ref.at[pl.ds(p, 1)],
                                  sem).start(priority=kk % 2)
        return carry

    lax.fori_loop(0, tb, issue, 0, unroll=ROW_DMA_UNROLL)
    for _ in range(TOP_K):
        pltpu.make_async_copy(xn_ref, xs_ref.at[pl.ds(0, tb)], sem).wait()


def _dispatch(tile_zero, pos3, xn, n_rows, tb, tile):
    t = xn.shape[0]
    return pl.pallas_call(
        functools.partial(_dispatch_kernel, tile=tile),
        grid_spec=pltpu.PrefetchScalarGridSpec(
            num_scalar_prefetch=1,
            grid=(t // tb,),
            in_specs=[
                pl.BlockSpec((1, 1, tb * TOP_K), lambda i, tz: (i, 0, 0), memory_space=pltpu.SMEM),
                pl.BlockSpec((tb, D_MODEL), lambda i, tz: (i, 0)),
            ],
            out_specs=pl.BlockSpec(memory_space=pl.ANY),
            scratch_shapes=[pltpu.VMEM((tile, D_MODEL), F32), pltpu.SemaphoreType.DMA,
                            pltpu.SemaphoreType.DMA],
        ),
        out_shape=jax.ShapeDtypeStruct((n_rows, D_MODEL), F32),
        compiler_params=_params("arbitrary"),
        name="dispatch",
    )(tile_zero, pos3, xn)


def _experts_kernel(te_ref, nv_ref, xs_ref, wg_ref, bg_ref, wu_ref, bu_ref, wd_ref, bd_ref, ys_ref,
                    wgb_ref, wub_ref, wdb_ref):
    i = pl.program_id(0)
    valid = i < nv_ref[0]
    new_expert = jnp.logical_or(i == 0, te_ref[i] != te_ref[jnp.maximum(i - 1, 0)])

    @pl.when(jnp.logical_and(valid, new_expert))
    def _():
        wgb_ref[...] = wg_ref[0].astype(BF16)
        wub_ref[...] = wu_ref[0].astype(BF16)
        wdb_ref[...] = wd_ref[0].astype(BF16)

    @pl.when(valid)
    def _():
        xb = xs_ref[...].astype(BF16)
        g = _dot(xb, wgb_ref[...]) + bg_ref[0]
        u = _dot(xb, wub_ref[...]) + bu_ref[0]
        g = jnp.minimum(g, SWIGLU_LIMIT)
        u = jnp.clip(u, -SWIGLU_LIMIT, SWIGLU_LIMIT)
        hidden = (u + 1.0) * g * jax.nn.sigmoid(SWIGLU_ALPHA * g)
        ys_ref[...] = _dot(hidden.astype(BF16), wdb_ref[...]) + bd_ref[0]


def _experts(tile_expert, n_valid, xs, wg, bg, wu, bu, wd, bd, tile):
    rows = xs.shape[0]
    nt = rows // tile
    rmap = lambda i, te, nv: (jnp.minimum(i, nv[0] - 1), 0)
    wmap = lambda i, te, nv: (te[i], 0, 0)
    return pl.pallas_call(
        _experts_kernel,
        grid_spec=pltpu.PrefetchScalarGridSpec(
            num_scalar_prefetch=2,
            grid=(nt,),
            in_specs=[
                pl.BlockSpec((tile, D_MODEL), rmap),
                pl.BlockSpec((1, D_MODEL, D_FF), wmap),
                pl.BlockSpec((1, 1, D_FF), wmap),
                pl.BlockSpec((1, D_MODEL, D_FF), wmap),
                pl.BlockSpec((1, 1, D_FF), wmap),
                pl.BlockSpec((1, D_FF, D_MODEL), wmap),
                pl.BlockSpec((1, 1, D_MODEL), wmap),
            ],
            out_specs=pl.BlockSpec((tile, D_MODEL), rmap),
            scratch_shapes=[pltpu.VMEM((D_MODEL, D_FF), BF16), pltpu.VMEM((D_MODEL, D_FF), BF16),
                            pltpu.VMEM((D_FF, D_MODEL), BF16)],
        ),
        out_shape=jax.ShapeDtypeStruct((rows, D_MODEL), F32),
        compiler_params=_params("arbitrary"),
        name="experts",
    )(tile_expert, n_valid, xs, wg, bg, wu, bu, wd, bd)


def _combine_kernel(pos_ref, posn_ref, x1_ref, cw_ref, ys_ref, o_ref, buf_ref, sem0, sem1):
    i = pl.program_id(0)
    n = pl.num_programs(0)
    tb = x1_ref.shape[0] // 2

    def issue(p_ref, sub, slot, sem):
        def body(t, carry):
            for kk in range(TOP_K):
                p = p_ref[0, 0, (sub * tb + t) * TOP_K + kk]
                pltpu.make_async_copy(ys_ref.at[pl.ds(p, 1)], buf_ref.at[slot, kk, pl.ds(t, 1)],
                                      sem).start(priority=kk % 2)
            return carry
        lax.fori_loop(0, tb, body, 0, unroll=ROW_DMA_UNROLL)

    def drain(slot, sem):
        for kk in range(TOP_K):
            pltpu.make_async_copy(ys_ref.at[pl.ds(0, tb)], buf_ref.at[slot, kk], sem).wait()

    def reduce(sub, slot):
        rows = slice(sub * tb, (sub + 1) * tb)
        cw = cw_ref[rows, :]
        acc = x1_ref[rows, :]
        for kk in range(TOP_K):
            acc = acc + cw[:, kk:kk + 1] * buf_ref[slot, kk]
        o_ref[rows, :] = acc

    @pl.when(i == 0)
    def _():
        issue(pos_ref, 0, 0, sem0)

    issue(pos_ref, 1, 1, sem1)
    drain(0, sem0)
    reduce(0, 0)

    @pl.when(i + 1 < n)
    def _():
        issue(posn_ref, 0, 0, sem0)

    drain(1, sem1)
    reduce(1, 1)


def _combine(pos3, x1, cw, ys, tb):
    t = x1.shape[0]
    n = t // (2 * tb)
    return pl.pallas_call(
        _combine_kernel,
        grid=(n,),
        in_specs=[
            pl.BlockSpec((1, 1, 2 * tb * TOP_K), lambda i: (i, 0, 0), memory_space=pltpu.SMEM),
            pl.BlockSpec((1, 1, 2 * tb * TOP_K), lambda i: (jnp.minimum(i + 1, n - 1), 0, 0),
                         memory_space=pltpu.SMEM),
            pl.BlockSpec((2 * tb, D_MODEL), lambda i: (i, 0)),
            pl.BlockSpec((2 * tb, LANES), lambda i: (i, 0)),
            pl.BlockSpec(memory_space=pl.ANY),
        ],
        out_specs=pl.BlockSpec((2 * tb, D_MODEL), lambda i: (i, 0)),
        out_shape=jax.ShapeDtypeStruct((t, D_MODEL), F32),
        scratch_shapes=[pltpu.VMEM((2, TOP_K, tb, D_MODEL), F32), pltpu.SemaphoreType.DMA,
                        pltpu.SemaphoreType.DMA],
        compiler_params=_params("arbitrary"),
        name="combine",
    )(pos3, pos3, x1, cw, ys)


def _pad_cols(a, width):
    return jnp.pad(a, ((0, 0), (0, width - a.shape[1])))


def _pack_in_proj(w_in):
    o = np.cumsum((GLA_QK_WIDTH, GLA_QK_WIDTH, GLA_V_WIDTH, GLA_V_WIDTH, GLA_GATE_RANK,
                   MLA_Q_RANK, MLA_KV_RANK, MLA_ROPE, D_MODEL, D_MODEL)).tolist()
    gla = w_in[:, 0:o[3]]
    lr = _pad_cols(w_in[:, o[3]:o[4]], _LR_W)
    mla = _pad_cols(w_in[:, o[4]:o[7]], _MLA_W)
    gates = w_in[:, o[7]:o[9]]
    return jnp.concatenate([gla, lr, mla, gates], axis=1).astype(BF16)


def _pad_heads(w, head_w):
    r = w.shape[0]
    w3 = w.reshape(r, MLA_HEADS, head_w)
    w3 = jnp.pad(w3, ((0, 0), (0, 0), (0, MLA_HEAD_PAD - head_w)))
    return w3.reshape(r, MLA_HEADS * MLA_HEAD_PAD)


def _tile_rows(n, pref):
    return pref if n % pref == 0 else n


def kernel(x, positions, attn_norm_w, w_in, w_gla_gk, b_gla_gk, gla_out_norm_w, w_gla_out,
           mla_q_norm_w, w_mla_uq, mla_kv_norm_w, w_mla_ukv, mla_qk_q_norm_w, mla_qk_k_norm_w,
           w_mla_out, w_out, moe_norm_w, w_router, b_router, w_exp_gate, b_exp_gate,
           w_exp_up, b_exp_up, w_exp_down, b_exp_down):
    batch, seq, _ = x.shape
    depth = w_in.shape[0]
    t = batch * seq
    x2 = x.reshape(t, D_MODEL)
    pos = positions.reshape(1, t).astype(F32)
    half = MLA_ROPE // 2
    freq = (ROPE_BASE ** (-jnp.arange(half, dtype=F32) / half))[:, None]

    tm = _tile_rows(t, 512)
    gla_ts = _tile_rows(seq, 256)
    flash_t = _tile_rows(seq, 512)
    moe_tile = 512
    n_rows = t * TOP_K + N_EXPERTS * moe_tile
    n_rows = (n_rows // moe_tile) * moe_tile
    disp_tb = _tile_rows(t, 512)
    comb_tb = _tile_rows(t, 256)
    pos_tb = _tile_rows(t, 256)

    for l in range(depth):
        w_pack = _pack_in_proj(w_in[l])
        wgk = jnp.pad(w_gla_gk[l], ((0, LANES - GLA_GATE_RANK), (0, 0)))
        wgk_hi, wgk_lo = _split_bf16(wgk)
        gla, la, mla, gates = _inproj(x2, attn_norm_w[l][None, :], w_pack, wgk_hi, wgk_lo,
                                      b_gla_gk[l][None, :], tm)

        o_a = _gla(gla, la, gla_out_norm_w[l][None, :], batch, seq, gla_ts)

        wuq = _pad_heads(w_mla_uq[l], MLA_QK).astype(BF16)
        wukv3 = w_mla_ukv[l].reshape(MLA_KV_RANK, MLA_HEADS, MLA_NOPE + MLA_V)
        wuk = wukv3[:, :, :MLA_NOPE].reshape(MLA_KV_RANK, MLA_HEADS * MLA_NOPE).astype(BF16)
        wvt = jnp.pad(wukv3[:, :, MLA_NOPE:].transpose(1, 2, 0),
                      ((0, 0), (0, FLASH_V_ROWS - MLA_V), (0, 0)))
        wvt = wvt.reshape(MLA_HEADS * FLASH_V_ROWS, MLA_KV_RANK).astype(BF16)
        vone = (jnp.arange(MLA_HEADS * FLASH_V_ROWS) % FLASH_V_ROWS == MLA_V).astype(F32)[:, None]
        qw = _pad_cols(mla_qk_q_norm_w[l][None, :], MLA_HEAD_PAD)
        kw = _pad_cols(mla_qk_k_norm_w[l][None, :], MLA_HEAD_PAD)
        q, k, vt = _mla_prep(mla, pos, wuq, wuk, wvt, vone, mla_q_norm_w[l][None, :],
                             mla_kv_norm_w[l][None, :], qw, kw, freq, batch, seq,
                             _tile_rows(seq, 512))
        o_b = _flash(q, k, vt, batch, seq, flash_t)

        wr = _pad_cols(w_router[l], LANES)
        wr_hi, wr_lo = _split_bf16(wr)
        br = _pad_cols(b_router[l][None, :], LANES)
        x1, xn, eid, cw, cnt = _merge_route(
            x2, o_a, o_b, gates, w_gla_out[l].astype(BF16), w_mla_out[l].astype(BF16),
            w_out[l].astype(BF16), moe_norm_w[l][None, :], wr_hi, wr_lo, br, tm)

        posn = _positions(eid, cnt, pos_tb, moe_tile)
        pos4 = posn[:, :TOP_K]
        counts = cnt[0, :N_EXPERTS].astype(I32)
        padded = ((counts + moe_tile - 1) // moe_tile) * moe_tile
        ends = jnp.cumsum(padded)
        n_tiles = n_rows // moe_tile
        n_valid = (ends[-1] // moe_tile).astype(I32)
        starts = jnp.arange(n_tiles, dtype=I32) * moe_tile
        starts = jnp.minimum(starts, ends[-1] - moe_tile)
        tile_expert = jnp.minimum(jnp.sum((starts[:, None] >= ends[None, :]).astype(I32), axis=1),
                                  N_EXPERTS - 1)
        tile_zero = jnp.where(padded > 0, ends - moe_tile, -1).astype(I32)

        xs = _dispatch(tile_zero, pos4.reshape(t // disp_tb, 1, disp_tb * TOP_K), xn, n_rows,
                       disp_tb, moe_tile)
        ys = _experts(tile_expert, n_valid.reshape(1), xs,
                      w_exp_gate[l], b_exp_gate[l][:, None, :],
                      w_exp_up[l], b_exp_up[l][:, None, :],
                      w_exp_down[l], b_exp_down[l][:, None, :], moe_tile)
        x2 = _combine(pos4.reshape(t // (2 * comb_tb), 1, 2 * comb_tb * TOP_K), x1, cw, ys, comb_tb)
    return x2.reshape(batch, seq, D_MODEL)
```

```python
import functools

import jax
import jax.numpy as jnp
import numpy as np
from jax import lax
from jax.experimental import pallas as pl
from jax.experimental.pallas import tpu as pltpu

F32 = jnp.float32
BF16 = jnp.bfloat16
I32 = jnp.int32

D_MODEL = 1024
EPS = 1e-6
GLA_HEADS = 4
GLA_DK = 128
GLA_DV = 256
GLA_GATE_RANK = 16
GLA_GATE_NORMALIZER = 16.0
GLA_QK_WIDTH = GLA_HEADS * GLA_DK
GLA_V_WIDTH = GLA_HEADS * GLA_DV
MLA_HEADS = 8
MLA_Q_RANK = 384
MLA_KV_RANK = 256
MLA_NOPE = 128
MLA_ROPE = 64
MLA_V = 128
MLA_QK = MLA_NOPE + MLA_ROPE
ROPE_BASE = 10000.0
N_EXPERTS = 32
TOP_K = 4
D_FF = 1024
SWIGLU_LIMIT = 7.0
SWIGLU_ALPHA = 1.702

LANES = 128
MLA_HEAD_PAD = 256
GLA_CHUNK = 128
FLASH_V_ROWS = 144
LOG2_E = 1.4426950408889634
VMEM_LIMIT = 56 * 1024 * 1024

_GLA_W = 2 * GLA_QK_WIDTH + 2 * GLA_V_WIDTH
_LR_W = LANES
_MLA_W = 768
_GATE_W = 2 * D_MODEL
_PACK_W = _GLA_W + _LR_W + _MLA_W + _GATE_W


def _dot(a, b):
    return jnp.dot(a, b, preferred_element_type=F32)


def _dot_nt(a, b):
    return lax.dot_general(a, b, (((1,), (1,)), ((), ())), preferred_element_type=F32)


def _dot_tn(a, b):
    return lax.dot_general(a, b, (((0,), (0,)), ((), ())), preferred_element_type=F32)


def _split_bf16(a):
    hi = a.astype(BF16)
    lo = (a - hi.astype(F32)).astype(BF16)
    return hi, lo


def _params(*sem):
    return pltpu.CompilerParams(dimension_semantics=sem, vmem_limit_bytes=VMEM_LIMIT)


LANE_CHUNKS = D_MODEL // LANES


def _inproj_kernel(x_ref, nw_ref, w_ref, wgk_hi_ref, wgk_lo_ref, bgk_ref,
                   gla_ref, la_ref, mla_ref, gate_ref):
    x = x_ref[...]
    ms = jnp.mean(x * x, axis=-1, keepdims=True)
    h = (x * lax.rsqrt(ms + EPS)) * nw_ref[...]
    hb = h.astype(BF16)
    o0 = 0
    q = _dot(hb, w_ref[:, 0:GLA_QK_WIDTH]) * (GLA_DK ** -0.5)
    gla_ref[:, 0:GLA_QK_WIDTH] = q.astype(BF16)
    gla_ref[:, GLA_QK_WIDTH:_GLA_W] = _dot(hb, w_ref[:, GLA_QK_WIDTH:_GLA_W]).astype(BF16)
    o0 = _GLA_W
    lr = _dot(hb, w_ref[:, o0:o0 + _LR_W])
    lr_hi, lr_lo = _split_bf16(lr)
    a_logit = (_dot(lr_hi, wgk_hi_ref[...]) + _dot(lr_lo, wgk_hi_ref[...])
               + _dot(lr_hi, wgk_lo_ref[...])) + bgk_ref[...]
    log_sig = jnp.minimum(a_logit, 0.0) - jnp.log1p(jnp.exp(-jnp.abs(a_logit)))
    la_ref[...] = log_sig / GLA_GATE_NORMALIZER
    o0 += _LR_W
    mla_ref[...] = _dot(hb, w_ref[:, o0:o0 + _MLA_W]).astype(BF16)
    o0 += _MLA_W
    gate_ref[...] = _dot(hb, w_ref[:, o0:o0 + _GATE_W]).astype(BF16)


def _inproj(x2, nw, w_pack, wgk_hi, wgk_lo, bgk, tm):
    t = x2.shape[0]
    const = lambda i: (0, 0)
    row = lambda i: (i, 0)
    return pl.pallas_call(
        _inproj_kernel,
        grid=(t // tm,),
        in_specs=[
            pl.BlockSpec((tm, D_MODEL), row),
            pl.BlockSpec((1, D_MODEL), const),
            pl.BlockSpec((D_MODEL, _PACK_W), const, pipeline_mode=pl.Buffered(1)),
            pl.BlockSpec((LANES, GLA_QK_WIDTH), const),
            pl.BlockSpec((LANES, GLA_QK_WIDTH), const),
            pl.BlockSpec((1, GLA_QK_WIDTH), const),
        ],
        out_specs=[
            pl.BlockSpec((tm, _GLA_W), row),
            pl.BlockSpec((tm, GLA_QK_WIDTH), row),
            pl.BlockSpec((tm, _MLA_W), row),
            pl.BlockSpec((tm, _GATE_W), row),
        ],
        out_shape=[
            jax.ShapeDtypeStruct((t, _GLA_W), BF16),
            jax.ShapeDtypeStruct((t, GLA_QK_WIDTH), F32),
            jax.ShapeDtypeStruct((t, _MLA_W), BF16),
            jax.ShapeDtypeStruct((t, _GATE_W), BF16),
        ],
        compiler_params=_params("parallel"),
        name="inproj",
    )(x2, nw, w_pack, wgk_hi, wgk_lo, bgk)


def _gla_kernel(q_ref, k_ref, v_ref, g_ref, la_ref, nw_ref, o_ref, state_ref, *, n_sub):
    @pl.when(pl.program_id(1) == 0)
    def _():
        state_ref[...] = jnp.zeros_like(state_ref)

    c = GLA_CHUNK
    row = lax.broadcasted_iota(I32, (c, c), 0)
    col = lax.broadcasted_iota(I32, (c, c), 1)
    causal = row >= col
    tri = causal.astype(BF16)
    mid = c // 2 - 1
    for s in range(n_sub):
        r0 = s * c
        la = la_ref[r0:r0 + c, :]
        la_hi, la_lo = _split_bf16(la)
        cum_all = _dot(tri, la_hi) + _dot(tri, la_lo)
        for h in range(GLA_HEADS):
            ks = slice(h * GLA_DK, (h + 1) * GLA_DK)
            vs = slice(h * GLA_DV, (h + 1) * GLA_DV)
            cum = cum_all[:, ks]
            q = q_ref[r0:r0 + c, ks].astype(F32)
            k = k_ref[r0:r0 + c, ks].astype(F32)
            v = v_ref[r0:r0 + c, vs]
            ref_row = cum[mid:mid + 1, :]
            last = cum[c - 1:c, :]
            qg = (q * jnp.exp(cum - ref_row)).astype(BF16)
            kg = (k * jnp.exp(ref_row - cum)).astype(BF16)
            scores = jnp.where(causal, _dot_nt(qg, kg), 0.0).astype(BF16)
            o = _dot(scores, v)
            st = state_ref[h]
            qe = (q * jnp.exp(cum)).astype(BF16)
            o = o + _dot_nt(qe, st.astype(BF16))
            ko = (k * jnp.exp(last - cum)).astype(BF16)
            state_ref[h] = st * jnp.exp(last) + _dot_tn(v, ko)
            ms = jnp.mean(o * o, axis=-1, keepdims=True)
            on = (o * lax.rsqrt(ms + EPS)) * nw_ref[...]
            g = g_ref[r0:r0 + c, vs].astype(F32)
            o_ref[r0:r0 + c, vs] = (on * (g * jax.nn.sigmoid(g))).astype(BF16)


def _gla(gla, la, nw, batch, seq, ts):
    t = gla.shape[0]
    nblk = seq // ts
    kern = functools.partial(_gla_kernel, n_sub=ts // GLA_CHUNK)
    rowmap = lambda col: (lambda b, i: (b * nblk + i, col))
    return pl.pallas_call(
        kern,
        grid=(batch, nblk),
        in_specs=[
            pl.BlockSpec((ts, GLA_QK_WIDTH), rowmap(0)),
            pl.BlockSpec((ts, GLA_QK_WIDTH), rowmap(1)),
            pl.BlockSpec((ts, GLA_V_WIDTH), rowmap(1)),
            pl.BlockSpec((ts, GLA_V_WIDTH), rowmap(2)),
            pl.BlockSpec((ts, GLA_QK_WIDTH), rowmap(0)),
            pl.BlockSpec((1, GLA_DV), lambda b, i: (0, 0)),
        ],
        out_specs=pl.BlockSpec((ts, GLA_V_WIDTH), rowmap(0)),
        out_shape=jax.ShapeDtypeStruct((t, GLA_V_WIDTH), BF16),
        scratch_shapes=[pltpu.VMEM((GLA_HEADS, GLA_DV, GLA_DK), F32)],
        compiler_params=_params("parallel", "arbitrary"),
        name="gla",
    )(gla, gla, gla, gla, la, nw)


def _rope(x, cos, sin, lane):
    half = MLA_ROPE // 2
    rot = jnp.where(lane < half, -pltpu.roll(x, LANES - half, axis=1), pltpu.roll(x, half, axis=1))
    return x * cos + rot * sin


def _mla_prep_kernel(mla_ref, pos_ref, wuq_ref, wuk_ref, wvt_ref, vone_ref, qnw_ref, kvnw_ref,
                     qw_ref, kw_ref, freq_ref, q_ref, k_ref, vt_ref):
    tm = mla_ref.shape[0]
    cq = mla_ref[:, 0:MLA_Q_RANK].astype(F32)
    ckv = mla_ref[:, MLA_Q_RANK:MLA_Q_RANK + MLA_KV_RANK].astype(F32)
    kr = mla_ref[:, MLA_Q_RANK + MLA_KV_RANK:_MLA_W].astype(F32)

    def rms(a, w):
        ms = jnp.mean(a * a, axis=-1, keepdims=True)
        return (a * lax.rsqrt(ms + EPS)) * w

    q_all = _dot(rms(cq, qnw_ref[...]).astype(BF16), wuq_ref[...])
    ckvn = rms(ckv, kvnw_ref[...]).astype(BF16)
    k_all = _dot(ckvn, wuk_ref[...])
    vt_ref[0] = (_dot_nt(wvt_ref[...], ckvn) + vone_ref[:, 0:1]).astype(BF16)

    half = MLA_ROPE // 2
    ang_t = freq_ref[...] * pos_ref[...]
    cos_t = jnp.cos(ang_t)
    sin_t = jnp.sin(ang_t)
    cos = jnp.concatenate([cos_t, cos_t, jnp.ones((LANES - 2 * half, tm), F32)], axis=0).T
    sin = jnp.concatenate([sin_t, sin_t, jnp.zeros((LANES - 2 * half, tm), F32)], axis=0).T
    lane = lax.broadcasted_iota(I32, (tm, LANES), 1)
    qw_nope = qw_ref[:, 0:MLA_NOPE]
    qw_rope = qw_ref[:, MLA_NOPE:MLA_HEAD_PAD]
    kw_nope = kw_ref[:, 0:MLA_NOPE]
    kw_rope = kw_ref[:, MLA_NOPE:MLA_HEAD_PAD]
    kr_sq = kr * kr
    kr_rot = _rope(kr * kw_rope, cos, sin, lane)
    scale = (MLA_QK ** -0.5) * LOG2_E
    for h in range(MLA_HEADS):
        base = h * MLA_HEAD_PAD
        qn = q_all[:, base:base + MLA_NOPE]
        qr = q_all[:, base + MLA_NOPE:base + MLA_HEAD_PAD]
        ss = jnp.sum(qn * qn + qr * qr, axis=-1, keepdims=True)
        r = lax.rsqrt(ss / MLA_QK + EPS)
        q_ref[:, base:base + MLA_NOPE] = ((qn * r) * qw_nope * scale).astype(BF16)
        q_ref[:, base + MLA_NOPE:base + MLA_HEAD_PAD] = (
            _rope((qr * r) * qw_rope, cos, sin, lane) * scale).astype(BF16)
        kn = k_all[:, h * MLA_NOPE:(h + 1) * MLA_NOPE]
        ssk = jnp.sum(kn * kn + kr_sq, axis=-1, keepdims=True)
        rk = lax.rsqrt(ssk / MLA_QK + EPS)
        k_ref[:, base:base + MLA_NOPE] = ((kn * rk) * kw_nope).astype(BF16)
        k_ref[:, base + MLA_NOPE:base + MLA_HEAD_PAD] = (kr_rot * rk).astype(BF16)


def _mla_prep(mla, pos, wuq, wuk, wvt, vone, qnw, kvnw, qw, kw, freq, batch, seq, tm):
    t = mla.shape[0]
    nblk = seq // tm
    const = lambda i: (0, 0)
    row = lambda i: (i, 0)
    hw = MLA_HEADS * MLA_HEAD_PAD
    return pl.pallas_call(
        _mla_prep_kernel,
        grid=(t // tm,),
        in_specs=[
            pl.BlockSpec((tm, _MLA_W), row),
            pl.BlockSpec((1, tm), lambda i: (0, i)),
            pl.BlockSpec((MLA_Q_RANK, hw), const),
            pl.BlockSpec((MLA_KV_RANK, MLA_HEADS * MLA_NOPE), const),
            pl.BlockSpec((MLA_HEADS * FLASH_V_ROWS, MLA_KV_RANK), const),
            pl.BlockSpec((MLA_HEADS * FLASH_V_ROWS, 1), const),
            pl.BlockSpec((1, MLA_Q_RANK), const),
            pl.BlockSpec((1, MLA_KV_RANK), const),
            pl.BlockSpec((1, MLA_HEAD_PAD), const),
            pl.BlockSpec((1, MLA_HEAD_PAD), const),
            pl.BlockSpec((MLA_ROPE // 2, 1), const),
        ],
        out_specs=[
            pl.BlockSpec((tm, hw), row),
            pl.BlockSpec((tm, hw), row),
            pl.BlockSpec((1, MLA_HEADS * FLASH_V_ROWS, tm), lambda i: (i // nblk, 0, i % nblk)),
        ],
        out_shape=[
            jax.ShapeDtypeStruct((t, hw), BF16),
            jax.ShapeDtypeStruct((t, hw), BF16),
            jax.ShapeDtypeStruct((batch, MLA_HEADS * FLASH_V_ROWS, seq), BF16),
        ],
        compiler_params=_params("parallel"),
        name="mla_prep",
    )(mla, pos, wuq, wuk, wvt, vone, qnw, kvnw, qw, kw, freq)


def _flash_kernel(q_ref, k_ref, vt_ref, o_ref, s_ref, cm_ref, acc_ref, m_ref, *, t):
    seq = q_ref.shape[0]
    nq = seq // t
    krow = lax.broadcasted_iota(I32, (t, t), 0)
    qcol = lax.broadcasted_iota(I32, (t, t), 1)
    items = [(qi, j) for qi in range(nq) for j in range(qi + 1)]

    def scores(idx):
        qi, j = items[idx]
        s = _dot_nt(k_ref[j * t:(j + 1) * t, :], q_ref[qi * t:(qi + 1) * t, :])
        s_ref[idx % 2] = s
        cm_ref[idx % 2] = jnp.max(s, axis=0, keepdims=True)

    def update(idx):
        qi, j = items[idx]
        if j == 0:
            m_ref[...] = jnp.full_like(m_ref, -jnp.inf)
            acc_ref[...] = jnp.zeros_like(acc_ref)
        s = s_ref[idx % 2]
        if j == qi:
            s = jnp.where(krow <= qcol, s, -jnp.inf)
            cm = jnp.max(s, axis=0, keepdims=True)
        else:
            cm = cm_ref[idx % 2]
        m_prev = m_ref[...]
        m_new = jnp.maximum(m_prev, cm)
        alpha = jnp.exp2(m_prev - m_new)
        p = jnp.exp2(s - m_new).astype(BF16)
        m_ref[...] = m_new
        acc_ref[...] = alpha * acc_ref[...] + _dot(vt_ref[0, :, j * t:(j + 1) * t], p)
        if j == qi:
            acc = acc_ref[...]
            out_t = acc[0:MLA_V, :] * (1.0 / acc[MLA_V:MLA_V + 1, :])
            o_ref[qi * t:(qi + 1) * t, :] = out_t.T.astype(BF16)

    scores(0)
    for idx in range(len(items)):
        if idx + 1 < len(items):
            scores(idx + 1)
        update(idx)


def _flash(q, k, vt, batch, seq, t):
    rows = q.shape[0]
    return pl.pallas_call(
        functools.partial(_flash_kernel, t=t),
        grid=(batch, MLA_HEADS),
        in_specs=[
            pl.BlockSpec((seq, MLA_HEAD_PAD), lambda b, h: (b, h)),
            pl.BlockSpec((seq, MLA_HEAD_PAD), lambda b, h: (b, h)),
            pl.BlockSpec((1, FLASH_V_ROWS, seq), lambda b, h: (b, h, 0)),
        ],
        out_specs=pl.BlockSpec((seq, MLA_V), lambda b, h: (b, h)),
        out_shape=jax.ShapeDtypeStruct((rows, MLA_HEADS * MLA_V), BF16),
        scratch_shapes=[
            pltpu.VMEM((2, t, t), F32),
            pltpu.VMEM((2, 1, t), F32),
            pltpu.VMEM((FLASH_V_ROWS, t), F32),
            pltpu.VMEM((1, t), F32),
        ],
        compiler_params=_params("parallel", "parallel"),
        name="flash",
    )(q, k, vt)


def _merge_route_kernel(x_ref, oa_ref, ob_ref, gate_ref, wa_ref, wb_ref, wo_ref, nw_ref,
                        wr_hi_ref, wr_lo_ref, br_ref,
                        x1_ref, xn_ref, eid_ref, cw_ref, cnt_ref):
    @pl.when(pl.program_id(0) == 0)
    def _():
        cnt_ref[...] = jnp.zeros_like(cnt_ref)

    tm = x_ref.shape[0]
    ya = _dot(oa_ref[...], wa_ref[...])
    yb = _dot(ob_ref[...], wb_ref[...])
    ga = gate_ref[:, 0:D_MODEL].astype(F32)
    gb = gate_ref[:, D_MODEL:2 * D_MODEL].astype(F32)
    merged = jax.nn.sigmoid(ga) * ya + jax.nn.sigmoid(gb) * yb
    x1 = x_ref[...] + _dot(merged.astype(BF16), wo_ref[...])
    x1_ref[...] = x1
    ms = jnp.mean(x1 * x1, axis=-1, keepdims=True)
    xn = (x1 * lax.rsqrt(ms + EPS)) * nw_ref[...]
    xn_ref[...] = xn
    xn_hi, xn_lo = _split_bf16(xn)
    logits = (_dot(xn_hi, wr_hi_ref[...]) + _dot(xn_lo, wr_hi_ref[...])
              + _dot(xn_hi, wr_lo_ref[...])) + br_ref[...]
    lane = lax.broadcasted_iota(I32, (tm, LANES), 1)
    work = jnp.where(lane < N_EXPERTS, logits, -jnp.inf)
    vals, idxs = [], []
    for _ in range(TOP_K):
        m = jnp.max(work, axis=-1, keepdims=True)
        idx = jnp.min(jnp.where(work == m, lane, LANES), axis=-1, keepdims=True)
        vals.append(m)
        idxs.append(idx)
        work = jnp.where(lane == idx, -jnp.inf, work)
    exps = [jnp.exp(v - vals[0]) for v in vals]
    denom = exps[0] + exps[1] + exps[2] + exps[3]
    eid = jnp.zeros((tm, LANES), I32)
    cw = jnp.zeros((tm, LANES), F32)
    sel = jnp.zeros((tm, LANES), F32)
    for kk in range(TOP_K):
        eid = jnp.where(lane == kk, idxs[kk], eid)
        cw = jnp.where(lane == kk, exps[kk] / denom, cw)
        sel = sel + (lane == idxs[kk]).astype(F32)
    eid_ref[...] = eid
    cw_ref[...] = cw
    cnt_ref[0:1, :] = cnt_ref[0:1, :] + jnp.sum(sel, axis=0, keepdims=True)


def _merge_route(x2, oa, ob, gates, wa, wb, wo, nw, wr_hi, wr_lo, br, tm):
    t = x2.shape[0]
    const = lambda i: (0, 0)
    row = lambda i: (i, 0)
    return pl.pallas_call(
        _merge_route_kernel,
        grid=(t // tm,),
        in_specs=[
            pl.BlockSpec((tm, D_MODEL), row),
            pl.BlockSpec((tm, GLA_V_WIDTH), row),
            pl.BlockSpec((tm, MLA_HEADS * MLA_V), row),
            pl.BlockSpec((tm, _GATE_W), row),
            pl.BlockSpec((GLA_V_WIDTH, D_MODEL), const),
            pl.BlockSpec((MLA_HEADS * MLA_V, D_MODEL), const),
            pl.BlockSpec((D_MODEL, D_MODEL), const),
            pl.BlockSpec((1, D_MODEL), const),
            pl.BlockSpec((D_MODEL, LANES), const),
            pl.BlockSpec((D_MODEL, LANES), const),
            pl.BlockSpec((1, LANES), const),
        ],
        out_specs=[
            pl.BlockSpec((tm, D_MODEL), row),
            pl.BlockSpec((tm, D_MODEL), row),
            pl.BlockSpec((tm, LANES), row),
            pl.BlockSpec((tm, LANES), row),
            pl.BlockSpec((8, LANES), const),
        ],
        out_shape=[
            jax.ShapeDtypeStruct((t, D_MODEL), F32),
            jax.ShapeDtypeStruct((t, D_MODEL), F32),
            jax.ShapeDtypeStruct((t, LANES), I32),
            jax.ShapeDtypeStruct((t, LANES), F32),
            jax.ShapeDtypeStruct((8, LANES), F32),
        ],
        compiler_params=_params("arbitrary"),
        name="merge_route",
    )(x2, oa, ob, gates, wa, wb, wo, nw, wr_hi, wr_lo, br)


def _positions_kernel(eid_ref, cnt_ref, pos_ref, carry_ref, *, tile):
    @pl.when(pl.program_id(0) == 0)
    def _():
        carry_ref[...] = jnp.zeros_like(carry_ref)

    tb = eid_ref.shape[0]
    lane1 = lax.broadcasted_iota(I32, (1, LANES), 1)
    cnt = cnt_ref[0:1, :]
    padded = jnp.floor((cnt + (tile - 1)) / tile) * tile
    incl = padded
    shift = 1
    while shift < N_EXPERTS:
        incl = incl + jnp.where(lane1 >= shift, pltpu.roll(incl, shift, axis=1), 0.0)
        shift *= 2
    offs = incl - padded

    lane = lax.broadcasted_iota(I32, (tb, LANES), 1)
    eid = eid_ref[...]
    onehots = [lane == jnp.broadcast_to(eid[:, kk:kk + 1], (tb, LANES)) for kk in range(TOP_K)]
    sel = jnp.zeros((tb, LANES), F32)
    for oh in onehots:
        sel = sel + oh.astype(F32)
    row = lax.broadcasted_iota(I32, (tb, tb), 0)
    col = lax.broadcasted_iota(I32, (tb, tb), 1)
    strict = (row > col).astype(BF16)
    rank = _dot(strict, sel.astype(BF16)) + carry_ref[...] + offs
    pos = jnp.zeros((tb, LANES), I32)
    for kk in range(TOP_K):
        pk = jnp.sum(jnp.where(onehots[kk], rank, 0.0), axis=-1, keepdims=True)
        pos = jnp.where(lane == kk, pk.astype(I32), pos)
    pos_ref[...] = pos
    carry_ref[...] = carry_ref[...] + jnp.sum(sel, axis=0, keepdims=True)


def _positions(eid, cnt, tb, tile):
    t = eid.shape[0]
    return pl.pallas_call(
        functools.partial(_positions_kernel, tile=tile),
        grid=(t // tb,),
        in_specs=[
            pl.BlockSpec((tb, LANES), lambda i: (i, 0)),
            pl.BlockSpec((8, LANES), lambda i: (0, 0)),
        ],
        out_specs=pl.BlockSpec((tb, LANES), lambda i: (i, 0)),
        out_shape=jax.ShapeDtypeStruct((t, LANES), I32),
        scratch_shapes=[pltpu.VMEM((1, LANES), F32)],
        compiler_params=_params("arbitrary"),
        name="positions",
    )(eid, cnt)


ROW_DMA_UNROLL = 8


def _dispatch_kernel(tz_ref, pos_ref, xn_ref, xs_ref, zero_ref, sem, zsem, *, tile):
    i = pl.program_id(0)
    tb = xn_ref.shape[0]

    def zero_copy(e):
        start = pl.multiple_of(tz_ref[e], tile)
        return pltpu.make_async_copy(zero_ref, xs_ref.at[pl.ds(start, tile), 0], zsem)

    @pl.when(i == 0)
    def _():
        zero_ref[...] = jnp.zeros_like(zero_ref)
        for e in range(N_EXPERTS):
            @pl.when(tz_ref[e] >= 0)
            def _():
                zero_copy(e).start()
        for e in range(N_EXPERTS):
            @pl.when(tz_ref[e] >= 0)
            def _():
                zero_copy(e).wait()

    def issue(t, carry):
        for kk in range(TOP_K):
            p = pos_ref[0, 0, t * TOP_K + kk]
            pltpu.make_async_copy(xn_ref.at[pl.ds(t, 1)], xs_ref.at[p], sem).start(priority=kk % 2)
        return carry

    lax.fori_loop(0, tb, issue, 0, unroll=ROW_DMA_UNROLL)
    for _ in range(TOP_K):
        pltpu.make_async_copy(xn_ref, xs_ref.at[pl.ds(0, tb), 0], sem).wait()


def _dispatch(tile_zero, pos3, xn, n_rows, tb, tile):
    t = xn.shape[0]
    return pl.pallas_call(
        functools.partial(_dispatch_kernel, tile=tile),
        grid_spec=pltpu.PrefetchScalarGridSpec(
            num_scalar_prefetch=1,
            grid=(t // tb,),
            in_specs=[
                pl.BlockSpec((1, 1, tb * TOP_K), lambda i, tz: (i, 0, 0), memory_space=pltpu.SMEM),
                pl.BlockSpec((tb, D_MODEL), lambda i, tz: (i, 0)),
            ],
            out_specs=pl.BlockSpec(memory_space=pl.ANY),
            scratch_shapes=[pltpu.VMEM((tile, D_MODEL), F32), pltpu.SemaphoreType.DMA,
                            pltpu.SemaphoreType.DMA],
        ),
        out_shape=jax.ShapeDtypeStruct((n_rows, 1, D_MODEL), F32),
        compiler_params=_params("arbitrary"),
        name="dispatch",
    )(tile_zero, pos3, xn)


def _experts_kernel(te_ref, nv_ref, xs_ref, wg_ref, bg_ref, wu_ref, bu_ref, wd_ref, bd_ref, ys_ref,
                    wgb_ref, wub_ref, wdb_ref, xbuf, ybuf, isem, osem, *, tile):
    i = pl.program_id(0)
    nv = nv_ref[0]
    valid = i < nv
    slot = i % 2

    def in_copies(step, s):
        r0 = pl.multiple_of(step * tile, tile)
        return [pltpu.make_async_copy(xs_ref.at[pl.ds(r0, tile), 0, pl.ds(c * LANES, LANES)],
                                      xbuf.at[s, :, pl.ds(c * LANES, LANES)], isem.at[s])
                for c in range(LANE_CHUNKS)]

    def out_copies(step, s):
        r0 = pl.multiple_of(step * tile, tile)
        return [pltpu.make_async_copy(ybuf.at[s, :, pl.ds(c * LANES, LANES)],
                                      ys_ref.at[pl.ds(r0, tile), 0, pl.ds(c * LANES, LANES)],
                                      osem.at[s])
                for c in range(LANE_CHUNKS)]

    @pl.when(i == 0)
    def _():
        for cp in in_copies(0, 0):
            cp.start()

    @pl.when(i + 1 < nv)
    def _():
        for cp in in_copies(i + 1, 1 - slot):
            cp.start()

    new_expert = jnp.logical_or(i == 0, te_ref[i] != te_ref[jnp.maximum(i - 1, 0)])

    @pl.when(jnp.logical_and(valid, new_expert))
    def _():
        wgb_ref[...] = wg_ref[0].astype(BF16)
        wub_ref[...] = wu_ref[0].astype(BF16)
        wdb_ref[...] = wd_ref[0].astype(BF16)

    @pl.when(valid)
    def _():
        for cp in in_copies(i, slot):
            cp.wait()

        @pl.when(i >= 2)
        def _():
            for cp in out_copies(i - 2, slot):
                cp.wait()

        xb = xbuf[slot].astype(BF16)
        g = _dot(xb, wgb_ref[...]) + bg_ref[0]
        u = _dot(xb, wub_ref[...]) + bu_ref[0]
        g = jnp.minimum(g, SWIGLU_LIMIT)
        u = jnp.clip(u, -SWIGLU_LIMIT, SWIGLU_LIMIT)
        hidden = (u + 1.0) * g * jax.nn.sigmoid(SWIGLU_ALPHA * g)
        ybuf[slot] = _dot(hidden.astype(BF16), wdb_ref[...]) + bd_ref[0]
        for cp in out_copies(i, slot):
            cp.start()

    @pl.when(i == nv - 1)
    def _():
        for cp in out_copies(i, slot):
            cp.wait()

        @pl.when(i >= 1)
        def _():
            for cp in out_copies(i - 1, 1 - slot):
                cp.wait()


def _experts(tile_expert, n_valid, xs, wg, bg, wu, bu, wd, bd, tile):
    rows = xs.shape[0]
    nt = rows // tile
    wmap = lambda i, te, nv: (te[i], 0, 0)
    return pl.pallas_call(
        functools.partial(_experts_kernel, tile=tile),
        grid_spec=pltpu.PrefetchScalarGridSpec(
            num_scalar_prefetch=2,
            grid=(nt,),
            in_specs=[
                pl.BlockSpec(memory_space=pl.ANY),
                pl.BlockSpec((1, D_MODEL, D_FF), wmap),
                pl.BlockSpec((1, 1, D_FF), wmap),
                pl.BlockSpec((1, D_MODEL, D_FF), wmap),
                pl.BlockSpec((1, 1, D_FF), wmap),
                pl.BlockSpec((1, D_FF, D_MODEL), wmap),
                pl.BlockSpec((1, 1, D_MODEL), wmap),
            ],
            out_specs=pl.BlockSpec(memory_space=pl.ANY),
            scratch_shapes=[pltpu.VMEM((D_MODEL, D_FF), BF16), pltpu.VMEM((D_MODEL, D_FF), BF16),
                            pltpu.VMEM((D_FF, D_MODEL), BF16),
                            pltpu.VMEM((2, tile, D_MODEL), F32), pltpu.VMEM((2, tile, D_MODEL), F32),
                            pltpu.SemaphoreType.DMA((2,)), pltpu.SemaphoreType.DMA((2,))],
        ),
        out_shape=jax.ShapeDtypeStruct((rows, 1, D_MODEL), F32),
        compiler_params=_params("arbitrary"),
        name="experts",
    )(tile_expert, n_valid, xs, wg, bg, wu, bu, wd, bd)


def _combine_kernel(pos_ref, posn_ref, x1_ref, cw_ref, ys_ref, o_ref, buf_ref, sem0, sem1):
    i = pl.program_id(0)
    n = pl.num_programs(0)
    tb = x1_ref.shape[0] // 2
    def issue(p_ref, sub, slot, sem):
        def body(t, carry):
            for kk in range(TOP_K):
                p = p_ref[0, 0, (sub * tb + t) * TOP_K + kk]
                pltpu.make_async_copy(ys_ref.at[p], buf_ref.at[slot, kk, pl.ds(t, 1)],
                                      sem).start(priority=kk % 2)
            return carry
        lax.fori_loop(0, tb, body, 0, unroll=ROW_DMA_UNROLL)

    def drain(slot, sem):
        for kk in range(TOP_K):
            pltpu.make_async_copy(ys_ref.at[pl.ds(0, tb), 0], buf_ref.at[slot, kk], sem).wait()

    def reduce(sub, slot):
        rows = slice(sub * tb, (sub + 1) * tb)
        cw = cw_ref[rows, :]
        acc = x1_ref[rows, :]
        for kk in range(TOP_K):
            acc = acc + cw[:, kk:kk + 1] * buf_ref[slot, kk]
        o_ref[rows, :] = acc

    @pl.when(i == 0)
    def _():
        issue(pos_ref, 0, 0, sem0)

    issue(pos_ref, 1, 1, sem1)
    drain(0, sem0)
    reduce(0, 0)

    @pl.when(i + 1 < n)
    def _():
        issue(posn_ref, 0, 0, sem0)

    drain(1, sem1)
    reduce(1, 1)


def _combine(pos3, x1, cw, ys, tb):
    t = x1.shape[0]
    n = t // (2 * tb)
    return pl.pallas_call(
        _combine_kernel,
        grid=(n,),
        in_specs=[
            pl.BlockSpec((1, 1, 2 * tb * TOP_K), lambda i: (i, 0, 0), memory_space=pltpu.SMEM),
            pl.BlockSpec((1, 1, 2 * tb * TOP_K), lambda i: (jnp.minimum(i + 1, n - 1), 0, 0),
                         memory_space=pltpu.SMEM),
            pl.BlockSpec((2 * tb, D_MODEL), lambda i: (i, 0)),
            pl.BlockSpec((2 * tb, LANES), lambda i: (i, 0)),
            pl.BlockSpec(memory_space=pl.ANY),
        ],
        out_specs=pl.BlockSpec((2 * tb, D_MODEL), lambda i: (i, 0)),
        out_shape=jax.ShapeDtypeStruct((t, D_MODEL), F32),
        scratch_shapes=[pltpu.VMEM((2, TOP_K, tb, D_MODEL), F32), pltpu.SemaphoreType.DMA,
                        pltpu.SemaphoreType.DMA],
        compiler_params=_params("arbitrary"),
        name="combine",
    )(pos3, pos3, x1, cw, ys)


def _pad_cols(a, width):
    return jnp.pad(a, ((0, 0), (0, width - a.shape[1])))


def _pack_in_proj(w_in):
    o = np.cumsum((GLA_QK_WIDTH, GLA_QK_WIDTH, GLA_V_WIDTH, GLA_V_WIDTH, GLA_GATE_RANK,
                   MLA_Q_RANK, MLA_KV_RANK, MLA_ROPE, D_MODEL, D_MODEL)).tolist()
    gla = w_in[:, 0:o[3]]
    lr = _pad_cols(w_in[:, o[3]:o[4]], _LR_W)
    mla = _pad_cols(w_in[:, o[4]:o[7]], _MLA_W)
    gates = w_in[:, o[7]:o[9]]
    return jnp.concatenate([gla, lr, mla, gates], axis=1).astype(BF16)


def _pad_heads(w, head_w):
    r = w.shape[0]
    w3 = w.reshape(r, MLA_HEADS, head_w)
    w3 = jnp.pad(w3, ((0, 0), (0, 0), (0, MLA_HEAD_PAD - head_w)))
    return w3.reshape(r, MLA_HEADS * MLA_HEAD_PAD)


def _tile_rows(n, pref):
    return pref if n % pref == 0 else n


def kernel(x, positions, attn_norm_w, w_in, w_gla_gk, b_gla_gk, gla_out_norm_w, w_gla_out,
           mla_q_norm_w, w_mla_uq, mla_kv_norm_w, w_mla_ukv, mla_qk_q_norm_w, mla_qk_k_norm_w,
           w_mla_out, w_out, moe_norm_w, w_router, b_router, w_exp_gate, b_exp_gate,
           w_exp_up, b_exp_up, w_exp_down, b_exp_down):
    batch, seq, _ = x.shape
    depth = w_in.shape[0]
    t = batch * seq
    x2 = x.reshape(t, D_MODEL)
    pos = positions.reshape(1, t).astype(F32)
    half = MLA_ROPE // 2
    freq = (ROPE_BASE ** (-jnp.arange(half, dtype=F32) / half))[:, None]

    tm = _tile_rows(t, 512)
    gla_ts = _tile_rows(seq, 256)
    flash_t = _tile_rows(seq, 512)
    moe_tile = 512
    n_rows = t * TOP_K + N_EXPERTS * moe_tile
    n_rows = (n_rows // moe_tile) * moe_tile
    disp_tb = _tile_rows(t, 512)
    comb_tb = _tile_rows(t, 256)
    pos_tb = _tile_rows(t, 256)

    for l in range(depth):
        w_pack = _pack_in_proj(w_in[l])
        wgk = jnp.pad(w_gla_gk[l], ((0, LANES - GLA_GATE_RANK), (0, 0)))
        wgk_hi, wgk_lo = _split_bf16(wgk)
        gla, la, mla, gates = _inproj(x2, attn_norm_w[l][None, :], w_pack, wgk_hi, wgk_lo,
                                      b_gla_gk[l][None, :], tm)

        o_a = _gla(gla, la, gla_out_norm_w[l][None, :], batch, seq, gla_ts)

        wuq = _pad_heads(w_mla_uq[l], MLA_QK).astype(BF16)
        wukv3 = w_mla_ukv[l].reshape(MLA_KV_RANK, MLA_HEADS, MLA_NOPE + MLA_V)
        wuk = wukv3[:, :, :MLA_NOPE].reshape(MLA_KV_RANK, MLA_HEADS * MLA_NOPE).astype(BF16)
        wvt = jnp.pad(wukv3[:, :, MLA_NOPE:].transpose(1, 2, 0),
                      ((0, 0), (0, FLASH_V_ROWS - MLA_V), (0, 0)))
        wvt = wvt.reshape(MLA_HEADS * FLASH_V_ROWS, MLA_KV_RANK).astype(BF16)
        vone = (jnp.arange(MLA_HEADS * FLASH_V_ROWS) % FLASH_V_ROWS == MLA_V).astype(F32)[:, None]
        qw = _pad_cols(mla_qk_q_norm_w[l][None, :], MLA_HEAD_PAD)
        kw = _pad_cols(mla_qk_k_norm_w[l][None, :], MLA_HEAD_PAD)
        q, k, vt = _mla_prep(mla, pos, wuq, wuk, wvt, vone, mla_q_norm_w[l][None, :],
                             mla_kv_norm_w[l][None, :], qw, kw, freq, batch, seq,
                             _tile_rows(seq, 512))
        o_b = _flash(q, k, vt, batch, seq, flash_t)

        wr = _pad_cols(w_router[l], LANES)
        wr_hi, wr_lo = _split_bf16(wr)
        br = _pad_cols(b_router[l][None, :], LANES)
        x1, xn, eid, cw, cnt = _merge_route(
            x2, o_a, o_b, gates, w_gla_out[l].astype(BF16), w_mla_out[l].astype(BF16),
            w_out[l].astype(BF16), moe_norm_w[l][None, :], wr_hi, wr_lo, br, tm)

        posn = _positions(eid, cnt, pos_tb, moe_tile)
        pos4 = posn[:, :TOP_K]
        counts = cnt[0, :N_EXPERTS].astype(I32)
        padded = ((counts + moe_tile - 1) // moe_tile) * moe_tile
        ends = jnp.cumsum(padded)
        n_tiles = n_rows // moe_tile
        n_valid = (ends[-1] // moe_tile).astype(I32)
        starts = jnp.arange(n_tiles, dtype=I32) * moe_tile
        starts = jnp.minimum(starts, ends[-1] - moe_tile)
        tile_expert = jnp.minimum(jnp.sum((starts[:, None] >= ends[None, :]).astype(I32), axis=1),
                                  N_EXPERTS - 1)
        tile_zero = jnp.where(padded > 0, ends - moe_tile, -1).astype(I32)

        xs = _dispatch(tile_zero, pos4.reshape(t // disp_tb, 1, disp_tb * TOP_K), xn, n_rows,
                       disp_tb, moe_tile)
        ys = _experts(tile_expert, n_valid.reshape(1), xs,
                      w_exp_gate[l], b_exp_gate[l][:, None, :],
                      w_exp_up[l], b_exp_up[l][:, None, :],
                      w_exp_down[l], b_exp_down[l][:, None, :], moe_tile)
        x2 = _combine(pos4.reshape(t // (2 * comb_tb), 1, 2 * comb_tb * TOP_K), x1, cw, ys, comb_tb)
    return x2.reshape(batch, seq, D_MODEL)
```

```python
import functools

import jax
import jax.numpy as jnp
import numpy as np
from jax import lax
from jax.experimental import pallas as pl
from jax.experimental.pallas import tpu as pltpu

F32 = jnp.float32
BF16 = jnp.bfloat16
I32 = jnp.int32
F8 = jnp.float8_e4m3fn
F8_TARGET = 224.0
HIDDEN_SCALE = 4.0

D_MODEL = 1024
EPS = 1e-6
GLA_HEADS = 4
GLA_DK = 128
GLA_DV = 256
GLA_GATE_RANK = 16
GLA_GATE_NORMALIZER = 16.0
GLA_QK_WIDTH = GLA_HEADS * GLA_DK
GLA_V_WIDTH = GLA_HEADS * GLA_DV
MLA_HEADS = 8
MLA_Q_RANK = 384
MLA_KV_RANK = 256
MLA_NOPE = 128
MLA_ROPE = 64
MLA_V = 128
MLA_QK = MLA_NOPE + MLA_ROPE
ROPE_BASE = 10000.0
N_EXPERTS = 32
TOP_K = 4
D_FF = 1024
SWIGLU_LIMIT = 7.0
SWIGLU_ALPHA = 1.702

LANES = 128
MLA_HEAD_PAD = 256
GLA_CHUNK = 128
FLASH_V_ROWS = 144
LOG2_E = 1.4426950408889634
VMEM_LIMIT = 56 * 1024 * 1024

_GLA_W = 2 * GLA_QK_WIDTH + 2 * GLA_V_WIDTH
_LR_W = LANES
_MLA_W = 768
_GATE_W = 2 * D_MODEL
_PACK_W = _GLA_W + _LR_W + _MLA_W + _GATE_W


def _dot(a, b):
    return jnp.dot(a, b, preferred_element_type=F32)


def _dot_nt(a, b):
    return lax.dot_general(a, b, (((1,), (1,)), ((), ())), preferred_element_type=F32)


def _dot_tn(a, b):
    return lax.dot_general(a, b, (((0,), (0,)), ((), ())), preferred_element_type=F32)


def _abs_max(a):
    return jnp.max(jnp.max(jnp.abs(a), axis=-1, keepdims=True), axis=0, keepdims=True)


def _pow2_scale(amax):
    return jnp.exp2(jnp.floor(jnp.log2(F8_TARGET / jnp.maximum(amax, 1e-30))))


def _split_bf16(a):
    hi = a.astype(BF16)
    lo = (a - hi.astype(F32)).astype(BF16)
    return hi, lo


def _params(*sem):
    return pltpu.CompilerParams(dimension_semantics=sem, vmem_limit_bytes=VMEM_LIMIT)


LANE_CHUNKS = D_MODEL // LANES


def _inproj_kernel(x_ref, nw_ref, w_ref, wgk_hi_ref, wgk_lo_ref, bgk_ref,
                   gla_ref, la_ref, mla_ref, gate_ref):
    x = x_ref[...]
    ms = jnp.mean(x * x, axis=-1, keepdims=True)
    h = (x * lax.rsqrt(ms + EPS)) * nw_ref[...]
    hb = h.astype(BF16)
    o0 = 0
    q = _dot(hb, w_ref[:, 0:GLA_QK_WIDTH]) * (GLA_DK ** -0.5)
    gla_ref[:, 0:GLA_QK_WIDTH] = q.astype(BF16)
    gla_ref[:, GLA_QK_WIDTH:_GLA_W] = _dot(hb, w_ref[:, GLA_QK_WIDTH:_GLA_W]).astype(BF16)
    o0 = _GLA_W
    lr = _dot(hb, w_ref[:, o0:o0 + _LR_W])
    lr_hi, lr_lo = _split_bf16(lr)
    a_logit = (_dot(lr_hi, wgk_hi_ref[...]) + _dot(lr_lo, wgk_hi_ref[...])
               + _dot(lr_hi, wgk_lo_ref[...])) + bgk_ref[...]
    log_sig = jnp.minimum(a_logit, 0.0) - jnp.log1p(jnp.exp(-jnp.abs(a_logit)))
    la_ref[...] = log_sig / GLA_GATE_NORMALIZER
    o0 += _LR_W
    mla_ref[...] = _dot(hb, w_ref[:, o0:o0 + _MLA_W]).astype(BF16)
    o0 += _MLA_W
    gate_ref[...] = _dot(hb, w_ref[:, o0:o0 + _GATE_W]).astype(BF16)


def _inproj(x2, nw, w_pack, wgk_hi, wgk_lo, bgk, tm):
    t = x2.shape[0]
    const = lambda i: (0, 0)
    row = lambda i: (i, 0)
    return pl.pallas_call(
        _inproj_kernel,
        grid=(t // tm,),
        in_specs=[
            pl.BlockSpec((tm, D_MODEL), row),
            pl.BlockSpec((1, D_MODEL), const),
            pl.BlockSpec((D_MODEL, _PACK_W), const, pipeline_mode=pl.Buffered(1)),
            pl.BlockSpec((LANES, GLA_QK_WIDTH), const),
            pl.BlockSpec((LANES, GLA_QK_WIDTH), const),
            pl.BlockSpec((1, GLA_QK_WIDTH), const),
        ],
        out_specs=[
            pl.BlockSpec((tm, _GLA_W), row),
            pl.BlockSpec((tm, GLA_QK_WIDTH), row),
            pl.BlockSpec((tm, _MLA_W), row),
            pl.BlockSpec((tm, _GATE_W), row),
        ],
        out_shape=[
            jax.ShapeDtypeStruct((t, _GLA_W), BF16),
            jax.ShapeDtypeStruct((t, GLA_QK_WIDTH), F32),
            jax.ShapeDtypeStruct((t, _MLA_W), BF16),
            jax.ShapeDtypeStruct((t, _GATE_W), BF16),
        ],
        compiler_params=_params("parallel"),
        name="inproj",
    )(x2, nw, w_pack, wgk_hi, wgk_lo, bgk)


def _gla_kernel(q_ref, k_ref, v_ref, g_ref, la_ref, nw_ref, o_ref, state_ref, *, n_sub):
    @pl.when(pl.program_id(1) == 0)
    def _():
        state_ref[...] = jnp.zeros_like(state_ref)

    c = GLA_CHUNK
    row = lax.broadcasted_iota(I32, (c, c), 0)
    col = lax.broadcasted_iota(I32, (c, c), 1)
    causal = row >= col
    tri = causal.astype(BF16)
    mid = c // 2 - 1
    for s in range(n_sub):
        r0 = s * c
        la = la_ref[r0:r0 + c, :]
        la_hi, la_lo = _split_bf16(la)
        cum_all = _dot(tri, la_hi) + _dot(tri, la_lo)
        for h in range(GLA_HEADS):
            ks = slice(h * GLA_DK, (h + 1) * GLA_DK)
            vs = slice(h * GLA_DV, (h + 1) * GLA_DV)
            cum = cum_all[:, ks]
            q = q_ref[r0:r0 + c, ks].astype(F32)
            k = k_ref[r0:r0 + c, ks].astype(F32)
            v = v_ref[r0:r0 + c, vs]
            ref_row = cum[mid:mid + 1, :]
            last = cum[c - 1:c, :]
            qg = (q * jnp.exp(cum - ref_row)).astype(BF16)
            kg = (k * jnp.exp(ref_row - cum)).astype(BF16)
            scores = jnp.where(causal, _dot_nt(qg, kg), 0.0).astype(BF16)
            o = _dot(scores, v)
            st = state_ref[h]
            qe = (q * jnp.exp(cum)).astype(BF16)
            o = o + _dot_nt(qe, st.astype(BF16))
            ko = (k * jnp.exp(last - cum)).astype(BF16)
            state_ref[h] = st * jnp.exp(last) + _dot_tn(v, ko)
            ms = jnp.mean(o * o, axis=-1, keepdims=True)
            on = (o * lax.rsqrt(ms + EPS)) * nw_ref[...]
            g = g_ref[r0:r0 + c, vs].astype(F32)
            o_ref[r0:r0 + c, vs] = (on * (g * jax.nn.sigmoid(g))).astype(BF16)


def _gla(gla, la, nw, batch, seq, ts):
    t = gla.shape[0]
    nblk = seq // ts
    kern = functools.partial(_gla_kernel, n_sub=ts // GLA_CHUNK)
    rowmap = lambda col: (lambda b, i: (b * nblk + i, col))
    return pl.pallas_call(
        kern,
        grid=(batch, nblk),
        in_specs=[
            pl.BlockSpec((ts, GLA_QK_WIDTH), rowmap(0)),
            pl.BlockSpec((ts, GLA_QK_WIDTH), rowmap(1)),
            pl.BlockSpec((ts, GLA_V_WIDTH), rowmap(1)),
            pl.BlockSpec((ts, GLA_V_WIDTH), rowmap(2)),
            pl.BlockSpec((ts, GLA_QK_WIDTH), rowmap(0)),
            pl.BlockSpec((1, GLA_DV), lambda b, i: (0, 0)),
        ],
        out_specs=pl.BlockSpec((ts, GLA_V_WIDTH), rowmap(0)),
        out_shape=jax.ShapeDtypeStruct((t, GLA_V_WIDTH), BF16),
        scratch_shapes=[pltpu.VMEM((GLA_HEADS, GLA_DV, GLA_DK), F32)],
        compiler_params=_params("parallel", "arbitrary"),
        name="gla",
    )(gla, gla, gla, gla, la, nw)


def _rope(x, cos, sin, lane):
    half = MLA_ROPE // 2
    rot = jnp.where(lane < half, -pltpu.roll(x, LANES - half, axis=1), pltpu.roll(x, half, axis=1))
    return x * cos + rot * sin


def _mla_prep_kernel(mla_ref, pos_ref, wuq_ref, wuk_ref, wvt_ref, vone_ref, qnw_ref, kvnw_ref,
                     qw_ref, kw_ref, freq_ref, q_ref, k_ref, vt_ref):
    tm = mla_ref.shape[0]
    cq = mla_ref[:, 0:MLA_Q_RANK].astype(F32)
    ckv = mla_ref[:, MLA_Q_RANK:MLA_Q_RANK + MLA_KV_RANK].astype(F32)
    kr = mla_ref[:, MLA_Q_RANK + MLA_KV_RANK:_MLA_W].astype(F32)

    def rms(a, w):
        ms = jnp.mean(a * a, axis=-1, keepdims=True)
        return (a * lax.rsqrt(ms + EPS)) * w

    q_all = _dot(rms(cq, qnw_ref[...]).astype(BF16), wuq_ref[...])
    ckvn = rms(ckv, kvnw_ref[...]).astype(BF16)
    k_all = _dot(ckvn, wuk_ref[...])
    vt_ref[0] = (_dot_nt(wvt_ref[...], ckvn) + vone_ref[:, 0:1]).astype(BF16)

    half = MLA_ROPE // 2
    ang_t = freq_ref[...] * pos_ref[...]
    cos_t = jnp.cos(ang_t)
    sin_t = jnp.sin(ang_t)
    cos = jnp.concatenate([cos_t, cos_t, jnp.ones((LANES - 2 * half, tm), F32)], axis=0).T
    sin = jnp.concatenate([sin_t, sin_t, jnp.zeros((LANES - 2 * half, tm), F32)], axis=0).T
    lane = lax.broadcasted_iota(I32, (tm, LANES), 1)
    qw_nope = qw_ref[:, 0:MLA_NOPE]
    qw_rope = qw_ref[:, MLA_NOPE:MLA_HEAD_PAD]
    kw_nope = kw_ref[:, 0:MLA_NOPE]
    kw_rope = kw_ref[:, MLA_NOPE:MLA_HEAD_PAD]
    kr_sq = kr * kr
    kr_rot = _rope(kr * kw_rope, cos, sin, lane)
    scale = (MLA_QK ** -0.5) * LOG2_E
    for h in range(MLA_HEADS):
        base = h * MLA_HEAD_PAD
        qn = q_all[:, base:base + MLA_NOPE]
        qr = q_all[:, base + MLA_NOPE:base + MLA_HEAD_PAD]
        ss = jnp.sum(qn * qn + qr * qr, axis=-1, keepdims=True)
        r = lax.rsqrt(ss / MLA_QK + EPS)
        q_ref[:, base:base + MLA_NOPE] = ((qn * r) * qw_nope * scale).astype(BF16)
        q_ref[:, base + MLA_NOPE:base + MLA_HEAD_PAD] = (
            _rope((qr * r) * qw_rope, cos, sin, lane) * scale).astype(BF16)
        kn = k_all[:, h * MLA_NOPE:(h + 1) * MLA_NOPE]
        ssk = jnp.sum(kn * kn + kr_sq, axis=-1, keepdims=True)
        rk = lax.rsqrt(ssk / MLA_QK + EPS)
        k_ref[:, base:base + MLA_NOPE] = ((kn * rk) * kw_nope).astype(BF16)
        k_ref[:, base + MLA_NOPE:base + MLA_HEAD_PAD] = (kr_rot * rk).astype(BF16)


def _mla_prep(mla, pos, wuq, wuk, wvt, vone, qnw, kvnw, qw, kw, freq, batch, seq, tm):
    t = mla.shape[0]
    nblk = seq // tm
    const = lambda i: (0, 0)
    row = lambda i: (i, 0)
    hw = MLA_HEADS * MLA_HEAD_PAD
    return pl.pallas_call(
        _mla_prep_kernel,
        grid=(t // tm,),
        in_specs=[
            pl.BlockSpec((tm, _MLA_W), row),
            pl.BlockSpec((1, tm), lambda i: (0, i)),
            pl.BlockSpec((MLA_Q_RANK, hw), const),
            pl.BlockSpec((MLA_KV_RANK, MLA_HEADS * MLA_NOPE), const),
            pl.BlockSpec((MLA_HEADS * FLASH_V_ROWS, MLA_KV_RANK), const),
            pl.BlockSpec((MLA_HEADS * FLASH_V_ROWS, 1), const),
            pl.BlockSpec((1, MLA_Q_RANK), const),
            pl.BlockSpec((1, MLA_KV_RANK), const),
            pl.BlockSpec((1, MLA_HEAD_PAD), const),
            pl.BlockSpec((1, MLA_HEAD_PAD), const),
            pl.BlockSpec((MLA_ROPE // 2, 1), const),
        ],
        out_specs=[
            pl.BlockSpec((tm, hw), row),
            pl.BlockSpec((tm, hw), row),
            pl.BlockSpec((1, MLA_HEADS * FLASH_V_ROWS, tm), lambda i: (i // nblk, 0, i % nblk)),
        ],
        out_shape=[
            jax.ShapeDtypeStruct((t, hw), BF16),
            jax.ShapeDtypeStruct((t, hw), BF16),
            jax.ShapeDtypeStruct((batch, MLA_HEADS * FLASH_V_ROWS, seq), BF16),
        ],
        compiler_params=_params("parallel"),
        name="mla_prep",
    )(mla, pos, wuq, wuk, wvt, vone, qnw, kvnw, qw, kw, freq)


def _flash_kernel(q_ref, k_ref, vt_ref, o_ref, s_ref, cm_ref, acc_ref, m_ref, *, t):
    seq = q_ref.shape[0]
    nq = seq // t
    krow = lax.broadcasted_iota(I32, (t, t), 0)
    qcol = lax.broadcasted_iota(I32, (t, t), 1)
    items = [(qi, j) for qi in range(nq) for j in range(qi + 1)]

    def scores(idx):
        qi, j = items[idx]
        s = _dot_nt(k_ref[j * t:(j + 1) * t, :], q_ref[qi * t:(qi + 1) * t, :])
        s_ref[idx % 2] = s
        cm_ref[idx % 2] = jnp.max(s, axis=0, keepdims=True)

    def update(idx):
        qi, j = items[idx]
        if j == 0:
            m_ref[...] = jnp.full_like(m_ref, -jnp.inf)
            acc_ref[...] = jnp.zeros_like(acc_ref)
        s = s_ref[idx % 2]
        if j == qi:
            s = jnp.where(krow <= qcol, s, -jnp.inf)
            cm = jnp.max(s, axis=0, keepdims=True)
        else:
            cm = cm_ref[idx % 2]
        m_prev = m_ref[...]
        m_new = jnp.maximum(m_prev, cm)
        alpha = jnp.exp2(m_prev - m_new)
        p = jnp.exp2(s - m_new).astype(BF16)
        m_ref[...] = m_new
        acc_ref[...] = alpha * acc_ref[...] + _dot(vt_ref[0, :, j * t:(j + 1) * t], p)
        if j == qi:
            acc = acc_ref[...]
            out_t = acc[0:MLA_V, :] * (1.0 / acc[MLA_V:MLA_V + 1, :])
            o_ref[qi * t:(qi + 1) * t, :] = out_t.T.astype(BF16)

    scores(0)
    for idx in range(len(items)):
        if idx + 1 < len(items):
            scores(idx + 1)
        update(idx)


def _flash(q, k, vt, batch, seq, t):
    rows = q.shape[0]
    return pl.pallas_call(
        functools.partial(_flash_kernel, t=t),
        grid=(batch, MLA_HEADS),
        in_specs=[
            pl.BlockSpec((seq, MLA_HEAD_PAD), lambda b, h: (b, h)),
            pl.BlockSpec((seq, MLA_HEAD_PAD), lambda b, h: (b, h)),
            pl.BlockSpec((1, FLASH_V_ROWS, seq), lambda b, h: (b, h, 0)),
        ],
        out_specs=pl.BlockSpec((seq, MLA_V), lambda b, h: (b, h)),
        out_shape=jax.ShapeDtypeStruct((rows, MLA_HEADS * MLA_V), BF16),
        scratch_shapes=[
            pltpu.VMEM((2, t, t), F32),
            pltpu.VMEM((2, 1, t), F32),
            pltpu.VMEM((FLASH_V_ROWS, t), F32),
            pltpu.VMEM((1, t), F32),
        ],
        compiler_params=_params("parallel", "parallel"),
        name="flash",
    )(q, k, vt)


def _merge_route_kernel(x_ref, oa_ref, ob_ref, gate_ref, wa_ref, wb_ref, wo_ref, nw_ref,
                        wr_hi_ref, wr_lo_ref, br_ref,
                        x1_ref, xn_ref, eid_ref, cw_ref, cnt_ref):
    @pl.when(pl.program_id(0) == 0)
    def _():
        cnt_ref[...] = jnp.zeros_like(cnt_ref)

    tm = x_ref.shape[0]
    ya = _dot(oa_ref[...], wa_ref[...])
    yb = _dot(ob_ref[...], wb_ref[...])
    ga = gate_ref[:, 0:D_MODEL].astype(F32)
    gb = gate_ref[:, D_MODEL:2 * D_MODEL].astype(F32)
    merged = jax.nn.sigmoid(ga) * ya + jax.nn.sigmoid(gb) * yb
    x1 = x_ref[...] + _dot(merged.astype(BF16), wo_ref[...])
    x1_ref[...] = x1
    ms = jnp.mean(x1 * x1, axis=-1, keepdims=True)
    xn = (x1 * lax.rsqrt(ms + EPS)) * nw_ref[...]
    xn_ref[...] = xn
    xn_hi, xn_lo = _split_bf16(xn)
    logits = (_dot(xn_hi, wr_hi_ref[...]) + _dot(xn_lo, wr_hi_ref[...])
              + _dot(xn_hi, wr_lo_ref[...])) + br_ref[...]
    lane = lax.broadcasted_iota(I32, (tm, LANES), 1)
    work = jnp.where(lane < N_EXPERTS, logits, -jnp.inf)
    vals, idxs = [], []
    for _ in range(TOP_K):
        m = jnp.max(work, axis=-1, keepdims=True)
        idx = jnp.min(jnp.where(work == m, lane, LANES), axis=-1, keepdims=True)
        vals.append(m)
        idxs.append(idx)
        work = jnp.where(lane == idx, -jnp.inf, work)
    exps = [jnp.exp(v - vals[0]) for v in vals]
    denom = exps[0] + exps[1] + exps[2] + exps[3]
    eid = jnp.zeros((tm, LANES), I32)
    cw = jnp.zeros((tm, LANES), F32)
    sel = jnp.zeros((tm, LANES), F32)
    for kk in range(TOP_K):
        eid = jnp.where(lane == kk, idxs[kk], eid)
        cw = jnp.where(lane == kk, exps[kk] / denom, cw)
        sel = sel + (lane == idxs[kk]).astype(F32)
    eid_ref[...] = eid
    cw_ref[...] = cw
    cnt_ref[0:1, :] = cnt_ref[0:1, :] + jnp.sum(sel, axis=0, keepdims=True)


def _merge_route(x2, oa, ob, gates, wa, wb, wo, nw, wr_hi, wr_lo, br, tm):
    t = x2.shape[0]
    const = lambda i: (0, 0)
    row = lambda i: (i, 0)
    return pl.pallas_call(
        _merge_route_kernel,
        grid=(t // tm,),
        in_specs=[
            pl.BlockSpec((tm, D_MODEL), row),
            pl.BlockSpec((tm, GLA_V_WIDTH), row),
            pl.BlockSpec((tm, MLA_HEADS * MLA_V), row),
            pl.BlockSpec((tm, _GATE_W), row),
            pl.BlockSpec((GLA_V_WIDTH, D_MODEL), const),
            pl.BlockSpec((MLA_HEADS * MLA_V, D_MODEL), const),
            pl.BlockSpec((D_MODEL, D_MODEL), const),
            pl.BlockSpec((1, D_MODEL), const),
            pl.BlockSpec((D_MODEL, LANES), const),
            pl.BlockSpec((D_MODEL, LANES), const),
            pl.BlockSpec((1, LANES), const),
        ],
        out_specs=[
            pl.BlockSpec((tm, D_MODEL), row),
            pl.BlockSpec((tm, D_MODEL), row),
            pl.BlockSpec((tm, LANES), row),
            pl.BlockSpec((tm, LANES), row),
            pl.BlockSpec((8, LANES), const),
        ],
        out_shape=[
            jax.ShapeDtypeStruct((t, D_MODEL), F32),
            jax.ShapeDtypeStruct((t, D_MODEL), F32),
            jax.ShapeDtypeStruct((t, LANES), I32),
            jax.ShapeDtypeStruct((t, LANES), F32),
            jax.ShapeDtypeStruct((8, LANES), F32),
        ],
        compiler_params=_params("arbitrary"),
        name="merge_route",
    )(x2, oa, ob, gates, wa, wb, wo, nw, wr_hi, wr_lo, br)


def _positions_kernel(eid_ref, cnt_ref, pos_ref, carry_ref, *, tile):
    @pl.when(pl.program_id(0) == 0)
    def _():
        carry_ref[...] = jnp.zeros_like(carry_ref)

    tb = eid_ref.shape[0]
    lane1 = lax.broadcasted_iota(I32, (1, LANES), 1)
    cnt = cnt_ref[0:1, :]
    padded = jnp.floor((cnt + (tile - 1)) / tile) * tile
    incl = padded
    shift = 1
    while shift < N_EXPERTS:
        incl = incl + jnp.where(lane1 >= shift, pltpu.roll(incl, shift, axis=1), 0.0)
        shift *= 2
    offs = incl - padded

    lane = lax.broadcasted_iota(I32, (tb, LANES), 1)
    eid = eid_ref[...]
    onehots = [lane == jnp.broadcast_to(eid[:, kk:kk + 1], (tb, LANES)) for kk in range(TOP_K)]
    sel = jnp.zeros((tb, LANES), F32)
    for oh in onehots:
        sel = sel + oh.astype(F32)
    row = lax.broadcasted_iota(I32, (tb, tb), 0)
    col = lax.broadcasted_iota(I32, (tb, tb), 1)
    strict = (row > col).astype(BF16)
    rank = _dot(strict, sel.astype(BF16)) + carry_ref[...] + offs
    pos = jnp.zeros((tb, LANES), I32)
    for kk in range(TOP_K):
        pk = jnp.sum(jnp.where(onehots[kk], rank, 0.0), axis=-1, keepdims=True)
        pos = jnp.where(lane == kk, pk.astype(I32), pos)
    pos_ref[...] = pos
    carry_ref[...] = carry_ref[...] + jnp.sum(sel, axis=0, keepdims=True)


def _positions(eid, cnt, tb, tile):
    t = eid.shape[0]
    return pl.pallas_call(
        functools.partial(_positions_kernel, tile=tile),
        grid=(t // tb,),
        in_specs=[
            pl.BlockSpec((tb, LANES), lambda i: (i, 0)),
            pl.BlockSpec((8, LANES), lambda i: (0, 0)),
        ],
        out_specs=pl.BlockSpec((tb, LANES), lambda i: (i, 0)),
        out_shape=jax.ShapeDtypeStruct((t, LANES), I32),
        scratch_shapes=[pltpu.VMEM((1, LANES), F32)],
        compiler_params=_params("arbitrary"),
        name="positions",
    )(eid, cnt)


ROW_DMA_UNROLL = 8


def _dispatch_kernel(tz_ref, pos_ref, xn_ref, xs_ref, zero_ref, sem, zsem, *, tile):
    i = pl.program_id(0)
    tb = xn_ref.shape[0]

    def zero_copy(e):
        start = pl.multiple_of(tz_ref[e], tile)
        return pltpu.make_async_copy(zero_ref, xs_ref.at[pl.ds(start, tile), 0], zsem)

    @pl.when(i == 0)
    def _():
        zero_ref[...] = jnp.zeros_like(zero_ref)
        for e in range(N_EXPERTS):
            @pl.when(tz_ref[e] >= 0)
            def _():
                zero_copy(e).start()
        for e in range(N_EXPERTS):
            @pl.when(tz_ref[e] >= 0)
            def _():
                zero_copy(e).wait()

    def issue(t, carry):
        for kk in range(TOP_K):
            p = pos_ref[0, 0, t * TOP_K + kk]
            pltpu.make_async_copy(xn_ref.at[pl.ds(t, 1)], xs_ref.at[p], sem).start(priority=kk % 2)
        return carry

    lax.fori_loop(0, tb, issue, 0, unroll=ROW_DMA_UNROLL)
    for _ in range(TOP_K):
        pltpu.make_async_copy(xn_ref, xs_ref.at[pl.ds(0, tb), 0], sem).wait()


def _dispatch(tile_zero, pos3, xn, n_rows, tb, tile):
    t = xn.shape[0]
    return pl.pallas_call(
        functools.partial(_dispatch_kernel, tile=tile),
        grid_spec=pltpu.PrefetchScalarGridSpec(
            num_scalar_prefetch=1,
            grid=(t // tb,),
            in_specs=[
                pl.BlockSpec((1, 1, tb * TOP_K), lambda i, tz: (i, 0, 0), memory_space=pltpu.SMEM),
                pl.BlockSpec((tb, D_MODEL), lambda i, tz: (i, 0)),
            ],
            out_specs=pl.BlockSpec(memory_space=pl.ANY),
            scratch_shapes=[pltpu.VMEM((tile, D_MODEL), F32), pltpu.SemaphoreType.DMA,
                            pltpu.SemaphoreType.DMA],
        ),
        out_shape=jax.ShapeDtypeStruct((n_rows, 1, D_MODEL), F32),
        compiler_params=_params("arbitrary"),
        name="dispatch",
    )(tile_zero, pos3, xn)


def _experts_kernel(te_ref, nv_ref, xs_ref, wg_ref, bg_ref, wu_ref, bu_ref, wd_ref, bd_ref, ys_ref,
                    wgb_ref, wub_ref, wdb_ref, wsc_ref, xbuf, ybuf, isem, osem, *, tile):
    i = pl.program_id(0)
    nv = nv_ref[0]
    valid = i < nv
    slot = i % 2

    def in_copies(step, s):
        r0 = pl.multiple_of(step * tile, tile)
        return [pltpu.make_async_copy(xs_ref.at[pl.ds(r0, tile), 0, pl.ds(c * LANES, LANES)],
                                      xbuf.at[s, :, pl.ds(c * LANES, LANES)], isem.at[s])
                for c in range(LANE_CHUNKS)]

    def out_copies(step, s):
        r0 = pl.multiple_of(step * tile, tile)
        return [pltpu.make_async_copy(ybuf.at[s, :, pl.ds(c * LANES, LANES)],
                                      ys_ref.at[pl.ds(r0, tile), 0, pl.ds(c * LANES, LANES)],
                                      osem.at[s])
                for c in range(LANE_CHUNKS)]

    @pl.when(i == 0)
    def _():
        for cp in in_copies(0, 0):
            cp.start()

    @pl.when(i + 1 < nv)
    def _():
        for cp in in_copies(i + 1, 1 - slot):
            cp.start()

    new_expert = jnp.logical_or(i == 0, te_ref[i] != te_ref[jnp.maximum(i - 1, 0)])

    @pl.when(jnp.logical_and(valid, new_expert))
    def _():
        for slot_w, (w_ref, w8_ref) in enumerate(((wg_ref, wgb_ref), (wu_ref, wub_ref),
                                                  (wd_ref, wdb_ref))):
            w = w_ref[0]
            sc = _pow2_scale(_abs_max(w))
            w8_ref[...] = (w * sc).astype(F8)
            wsc_ref[slot_w:slot_w + 1, :] = jnp.broadcast_to(1.0 / sc, (1, LANES))

    @pl.when(valid)
    def _():
        for cp in in_copies(i, slot):
            cp.wait()

        @pl.when(i >= 2)
        def _():
            for cp in out_copies(i - 2, slot):
                cp.wait()

        x = xbuf[slot]
        sx = _pow2_scale(_abs_max(x))
        x8 = (x * sx).astype(F8)
        inv_x = 1.0 / sx
        g = _dot(x8, wgb_ref[...]) * (inv_x * wsc_ref[0:1, 0:1]) + bg_ref[0]
        u = _dot(x8, wub_ref[...]) * (inv_x * wsc_ref[1:2, 0:1]) + bu_ref[0]
        g = jnp.minimum(g, SWIGLU_LIMIT)
        u = jnp.clip(u, -SWIGLU_LIMIT, SWIGLU_LIMIT)
        hidden = (u + 1.0) * g * jax.nn.sigmoid(SWIGLU_ALPHA * g)
        h8 = (hidden * HIDDEN_SCALE).astype(F8)
        ybuf[slot] = (_dot(h8, wdb_ref[...]) * (wsc_ref[2:3, 0:1] * (1.0 / HIDDEN_SCALE))
                      + bd_ref[0])
        for cp in out_copies(i, slot):
            cp.start()

    @pl.when(i == nv - 1)
    def _():
        for cp in out_copies(i, slot):
            cp.wait()

        @pl.when(i >= 1)
        def _():
            for cp in out_copies(i - 1, 1 - slot):
                cp.wait()


def _experts(tile_expert, n_valid, xs, wg, bg, wu, bu, wd, bd, tile):
    rows = xs.shape[0]
    nt = rows // tile
    wmap = lambda i, te, nv: (te[i], 0, 0)
    return pl.pallas_call(
        functools.partial(_experts_kernel, tile=tile),
        grid_spec=pltpu.PrefetchScalarGridSpec(
            num_scalar_prefetch=2,
            grid=(nt,),
            in_specs=[
                pl.BlockSpec(memory_space=pl.ANY),
                pl.BlockSpec((1, D_MODEL, D_FF), wmap),
                pl.BlockSpec((1, 1, D_FF), wmap),
                pl.BlockSpec((1, D_MODEL, D_FF), wmap),
                pl.BlockSpec((1, 1, D_FF), wmap),
                pl.BlockSpec((1, D_FF, D_MODEL), wmap),
                pl.BlockSpec((1, 1, D_MODEL), wmap),
            ],
            out_specs=pl.BlockSpec(memory_space=pl.ANY),
            scratch_shapes=[pltpu.VMEM((D_MODEL, D_FF), F8), pltpu.VMEM((D_MODEL, D_FF), F8),
                            pltpu.VMEM((D_FF, D_MODEL), F8), pltpu.VMEM((8, LANES), F32),
                            pltpu.VMEM((2, tile, D_MODEL), F32), pltpu.VMEM((2, tile, D_MODEL), F32),
                            pltpu.SemaphoreType.DMA((2,)), pltpu.SemaphoreType.DMA((2,))],
        ),
        out_shape=jax.ShapeDtypeStruct((rows, 1, D_MODEL), F32),
        compiler_params=_params("arbitrary"),
        name="experts",
    )(tile_expert, n_valid, xs, wg, bg, wu, bu, wd, bd)


def _combine_kernel(pos_ref, posn_ref, x1_ref, cw_ref, ys_ref, o_ref, buf_ref, sem0, sem1):
    i = pl.program_id(0)
    n = pl.num_programs(0)
    tb = x1_ref.shape[0] // 2
    def issue(p_ref, sub, slot, sem):
        def body(t, carry):
            for kk in range(TOP_K):
                p = p_ref[0, 0, (sub * tb + t) * TOP_K + kk]
                pltpu.make_async_copy(ys_ref.at[p], buf_ref.at[slot, kk, pl.ds(t, 1)],
                                      sem).start(priority=kk % 2)
            return carry
        lax.fori_loop(0, tb, body, 0, unroll=ROW_DMA_UNROLL)

    def drain(slot, sem):
        for kk in range(TOP_K):
            pltpu.make_async_copy(ys_ref.at[pl.ds(0, tb), 0], buf_ref.at[slot, kk], sem).wait()

    def reduce(sub, slot):
        rows = slice(sub * tb, (sub + 1) * tb)
        cw = cw_ref[rows, :]
        acc = x1_ref[rows, :]
        for kk in range(TOP_K):
            acc = acc + cw[:, kk:kk + 1] * buf_ref[slot, kk]
        o_ref[rows, :] = acc

    @pl.when(i == 0)
    def _():
        issue(pos_ref, 0, 0, sem0)

    issue(pos_ref, 1, 1, sem1)
    drain(0, sem0)
    reduce(0, 0)

    @pl.when(i + 1 < n)
    def _():
        issue(posn_ref, 0, 0, sem0)

    drain(1, sem1)
    reduce(1, 1)


def _combine(pos3, x1, cw, ys, tb):
    t = x1.shape[0]
    n = t // (2 * tb)
    return pl.pallas_call(
        _combine_kernel,
        grid=(n,),
        in_specs=[
            pl.BlockSpec((1, 1, 2 * tb * TOP_K), lambda i: (i, 0, 0), memory_space=pltpu.SMEM),
            pl.BlockSpec((1, 1, 2 * tb * TOP_K), lambda i: (jnp.minimum(i + 1, n - 1), 0, 0),
                         memory_space=pltpu.SMEM),
            pl.BlockSpec((2 * tb, D_MODEL), lambda i: (i, 0)),
            pl.BlockSpec((2 * tb, LANES), lambda i: (i, 0)),
            pl.BlockSpec(memory_space=pl.ANY),
        ],
        out_specs=pl.BlockSpec((2 * tb, D_MODEL), lambda i: (i, 0)),
        out_shape=jax.ShapeDtypeStruct((t, D_MODEL), F32),
        scratch_shapes=[pltpu.VMEM((2, TOP_K, tb, D_MODEL), F32), pltpu.SemaphoreType.DMA,
                        pltpu.SemaphoreType.DMA],
        compiler_params=_params("arbitrary"),
        name="combine",
    )(pos3, pos3, x1, cw, ys)


def _pad_cols(a, width):
    return jnp.pad(a, ((0, 0), (0, width - a.shape[1])))


def _pack_in_proj(w_in):
    o = np.cumsum((GLA_QK_WIDTH, GLA_QK_WIDTH, GLA_V_WIDTH, GLA_V_WIDTH, GLA_GATE_RANK,
                   MLA_Q_RANK, MLA_KV_RANK, MLA_ROPE, D_MODEL, D_MODEL)).tolist()
    gla = w_in[:, 0:o[3]]
    lr = _pad_cols(w_in[:, o[3]:o[4]], _LR_W)
    mla = _pad_cols(w_in[:, o[4]:o[7]], _MLA_W)
    gates = w_in[:, o[7]:o[9]]
    return jnp.concatenate([gla, lr, mla, gates], axis=1).astype(BF16)


def _pad_heads(w, head_w):
    r = w.shape[0]
    w3 = w.reshape(r, MLA_HEADS, head_w)
    w3 = jnp.pad(w3, ((0, 0), (0, 0), (0, MLA_HEAD_PAD - head_w)))
    return w3.reshape(r, MLA_HEADS * MLA_HEAD_PAD)


def _tile_rows(n, pref):
    return pref if n % pref == 0 else n


def kernel(x, positions, attn_norm_w, w_in, w_gla_gk, b_gla_gk, gla_out_norm_w, w_gla_out,
           mla_q_norm_w, w_mla_uq, mla_kv_norm_w, w_mla_ukv, mla_qk_q_norm_w, mla_qk_k_norm_w,
           w_mla_out, w_out, moe_norm_w, w_router, b_router, w_exp_gate, b_exp_gate,
           w_exp_up, b_exp_up, w_exp_down, b_exp_down):
    batch, seq, _ = x.shape
    depth = w_in.shape[0]
    t = batch * seq
    x2 = x.reshape(t, D_MODEL)
    pos = positions.reshape(1, t).astype(F32)
    half = MLA_ROPE // 2
    freq = (ROPE_BASE ** (-jnp.arange(half, dtype=F32) / half))[:, None]

    tm = _tile_rows(t, 512)
    gla_ts = _tile_rows(seq, 256)
    flash_t = _tile_rows(seq, 512)
    moe_tile = 512
    n_rows = t * TOP_K + N_EXPERTS * moe_tile
    n_rows = (n_rows // moe_tile) * moe_tile
    disp_tb = _tile_rows(t, 512)
    comb_tb = _tile_rows(t, 256)
    pos_tb = _tile_rows(t, 512)

    for l in range(depth):
        w_pack = _pack_in_proj(w_in[l])
        wgk = jnp.pad(w_gla_gk[l], ((0, LANES - GLA_GATE_RANK), (0, 0)))
        wgk_hi, wgk_lo = _split_bf16(wgk)
        gla, la, mla, gates = _inproj(x2, attn_norm_w[l][None, :], w_pack, wgk_hi, wgk_lo,
                                      b_gla_gk[l][None, :], tm)

        o_a = _gla(gla, la, gla_out_norm_w[l][None, :], batch, seq, gla_ts)

        wuq = _pad_heads(w_mla_uq[l], MLA_QK).astype(BF16)
        wukv3 = w_mla_ukv[l].reshape(MLA_KV_RANK, MLA_HEADS, MLA_NOPE + MLA_V)
        wuk = wukv3[:, :, :MLA_NOPE].reshape(MLA_KV_RANK, MLA_HEADS * MLA_NOPE).astype(BF16)
        wvt = jnp.pad(wukv3[:, :, MLA_NOPE:].transpose(1, 2, 0),
                      ((0, 0), (0, FLASH_V_ROWS - MLA_V), (0, 0)))
        wvt = wvt.reshape(MLA_HEADS * FLASH_V_ROWS, MLA_KV_RANK).astype(BF16)
        vone = (jnp.arange(MLA_HEADS * FLASH_V_ROWS) % FLASH_V_ROWS == MLA_V).astype(F32)[:, None]
        qw = _pad_cols(mla_qk_q_norm_w[l][None, :], MLA_HEAD_PAD)
        kw = _pad_cols(mla_qk_k_norm_w[l][None, :], MLA_HEAD_PAD)
        q, k, vt = _mla_prep(mla, pos, wuq, wuk, wvt, vone, mla_q_norm_w[l][None, :],
                             mla_kv_norm_w[l][None, :], qw, kw, freq, batch, seq,
                             _tile_rows(seq, 512))
        o_b = _flash(q, k, vt, batch, seq, flash_t)

        wr = _pad_cols(w_router[l], LANES)
        wr_hi, wr_lo = _split_bf16(wr)
        br = _pad_cols(b_router[l][None, :], LANES)
        x1, xn, eid, cw, cnt = _merge_route(
            x2, o_a, o_b, gates, w_gla_out[l].astype(BF16), w_mla_out[l].astype(BF16),
            w_out[l].astype(BF16), moe_norm_w[l][None, :], wr_hi, wr_lo, br, tm)

        posn = _positions(eid, cnt, pos_tb, moe_tile)
        pos4 = posn[:, :TOP_K]
        counts = cnt[0, :N_EXPERTS].astype(I32)
        padded = ((counts + moe_tile - 1) // moe_tile) * moe_tile
        ends = jnp.cumsum(padded)
        n_tiles = n_rows // moe_tile
        n_valid = (ends[-1] // moe_tile).astype(I32)
        starts = jnp.arange(n_tiles, dtype=I32) * moe_tile
        starts = jnp.minimum(starts, ends[-1] - moe_tile)
        tile_expert = jnp.minimum(jnp.sum((starts[:, None] >= ends[None, :]).astype(I32), axis=1),
                                  N_EXPERTS - 1)
        tile_zero = jnp.where(padded > 0, ends - moe_tile, -1).astype(I32)

        xs = _dispatch(tile_zero, pos4.reshape(t // disp_tb, 1, disp_tb * TOP_K), xn, n_rows,
                       disp_tb, moe_tile)
        ys = _experts(tile_expert, n_valid.reshape(1), xs,
                      w_exp_gate[l], b_exp_gate[l][:, None, :],
                      w_exp_up[l], b_exp_up[l][:, None, :],
                      w_exp_down[l], b_exp_down[l][:, None, :], moe_tile)
        x2 = _combine(pos4.reshape(t // (2 * comb_tb), 1, 2 * comb_tb * TOP_K), x1, cw, ys, comb_tb)
    return x2.reshape(batch, seq, D_MODEL)
```

```python
import functools

import jax
import jax.numpy as jnp
import numpy as np
from jax import lax
from jax.experimental import pallas as pl
from jax.experimental.pallas import tpu as pltpu

F32 = jnp.float32
BF16 = jnp.bfloat16
I32 = jnp.int32
F8 = jnp.float8_e4m3fn
F8_TARGET = 224.0
HIDDEN_SCALE = 4.0

D_MODEL = 1024
EPS = 1e-6
GLA_HEADS = 4
GLA_DK = 128
GLA_DV = 256
GLA_GATE_RANK = 16
GLA_GATE_NORMALIZER = 16.0
GLA_QK_WIDTH = GLA_HEADS * GLA_DK
GLA_V_WIDTH = GLA_HEADS * GLA_DV
MLA_HEADS = 8
MLA_Q_RANK = 384
MLA_KV_RANK = 256
MLA_NOPE = 128
MLA_ROPE = 64
MLA_V = 128
MLA_QK = MLA_NOPE + MLA_ROPE
ROPE_BASE = 10000.0
N_EXPERTS = 32
TOP_K = 4
D_FF = 1024
SWIGLU_LIMIT = 7.0
SWIGLU_ALPHA = 1.702

LANES = 128
MLA_HEAD_PAD = 256
GLA_CHUNK = 128
FLASH_V_ROWS = 144
LOG2_E = 1.4426950408889634
VMEM_LIMIT = 56 * 1024 * 1024

_GLA_W = 2 * GLA_QK_WIDTH + 2 * GLA_V_WIDTH
_LR_W = LANES
_MLA_W = 768
_GATE_W = 2 * D_MODEL
_PACK_W = _GLA_W + _LR_W + _MLA_W + _GATE_W


def _dot(a, b):
    return jnp.dot(a, b, preferred_element_type=F32)


def _dot_nt(a, b):
    return lax.dot_general(a, b, (((1,), (1,)), ((), ())), preferred_element_type=F32)


def _dot_tn(a, b):
    return lax.dot_general(a, b, (((0,), (0,)), ((), ())), preferred_element_type=F32)


def _abs_max(a):
    return jnp.max(jnp.max(jnp.abs(a), axis=-1, keepdims=True), axis=0, keepdims=True)


def _pow2_scale(amax):
    return jnp.exp2(jnp.floor(jnp.log2(F8_TARGET / jnp.maximum(amax, 1e-30))))


def _split_bf16(a):
    hi = a.astype(BF16)
    lo = (a - hi.astype(F32)).astype(BF16)
    return hi, lo


def _params(*sem):
    return pltpu.CompilerParams(dimension_semantics=sem, vmem_limit_bytes=VMEM_LIMIT)


LANE_CHUNKS = D_MODEL // LANES


def _inproj_kernel(x_ref, nw_ref, w_ref, wgk_hi_ref, wgk_lo_ref, bgk_ref, pos_ref,
                   wuq_ref, wuk_ref, wvt_ref, vone_ref, qnw_ref, kvnw_ref, qw_ref, kw_ref, freq_ref,
                   gla_ref, la_ref, gate_ref, q_ref, k_ref, vt_ref):
    x = x_ref[...]
    ms = jnp.mean(x * x, axis=-1, keepdims=True)
    h = (x * lax.rsqrt(ms + EPS)) * nw_ref[...]
    hb = h.astype(BF16)
    mla = _dot(hb, w_ref[:, _GLA_W + _LR_W:_GLA_W + _LR_W + _MLA_W])
    _mla_prep(mla, pos_ref, wuq_ref, wuk_ref, wvt_ref, vone_ref, qnw_ref, kvnw_ref, qw_ref, kw_ref,
              freq_ref, q_ref, k_ref, vt_ref)
    o0 = 0
    q = _dot(hb, w_ref[:, 0:GLA_QK_WIDTH]) * (GLA_DK ** -0.5)
    gla_ref[:, 0:GLA_QK_WIDTH] = q.astype(BF16)
    gla_ref[:, GLA_QK_WIDTH:_GLA_W] = _dot(hb, w_ref[:, GLA_QK_WIDTH:_GLA_W]).astype(BF16)
    o0 = _GLA_W
    lr = _dot(hb, w_ref[:, o0:o0 + _LR_W])
    lr_hi, lr_lo = _split_bf16(lr)
    a_logit = (_dot(lr_hi, wgk_hi_ref[...]) + _dot(lr_lo, wgk_hi_ref[...])
               + _dot(lr_hi, wgk_lo_ref[...])) + bgk_ref[...]
    log_sig = jnp.minimum(a_logit, 0.0) - jnp.log1p(jnp.exp(-jnp.abs(a_logit)))
    la_ref[...] = log_sig / GLA_GATE_NORMALIZER
    o0 += _LR_W + _MLA_W
    gate_ref[...] = _dot(hb, w_ref[:, o0:o0 + _GATE_W]).astype(BF16)


def _inproj(x2, nw, w_pack, wgk_hi, wgk_lo, bgk, pos, wuq, wuk, wvt, vone, qnw, kvnw, qw, kw, freq,
            batch, seq, tm):
    t = x2.shape[0]
    nblk = seq // tm
    const = lambda i: (0, 0)
    row = lambda i: (i, 0)
    hw = MLA_HEADS * MLA_HEAD_PAD
    return pl.pallas_call(
        _inproj_kernel,
        grid=(t // tm,),
        in_specs=[
            pl.BlockSpec((tm, D_MODEL), row),
            pl.BlockSpec((1, D_MODEL), const),
            pl.BlockSpec((D_MODEL, _PACK_W), const, pipeline_mode=pl.Buffered(1)),
            pl.BlockSpec((LANES, GLA_QK_WIDTH), const),
            pl.BlockSpec((LANES, GLA_QK_WIDTH), const),
            pl.BlockSpec((1, GLA_QK_WIDTH), const),
            pl.BlockSpec((1, tm), lambda i: (0, i)),
            pl.BlockSpec((MLA_Q_RANK, hw), const),
            pl.BlockSpec((MLA_KV_RANK, MLA_HEADS * MLA_NOPE), const),
            pl.BlockSpec((MLA_HEADS * FLASH_V_ROWS, MLA_KV_RANK), const),
            pl.BlockSpec((MLA_HEADS * FLASH_V_ROWS, 1), const),
            pl.BlockSpec((1, MLA_Q_RANK), const),
            pl.BlockSpec((1, MLA_KV_RANK), const),
            pl.BlockSpec((1, MLA_HEAD_PAD), const),
            pl.BlockSpec((1, MLA_HEAD_PAD), const),
            pl.BlockSpec((MLA_ROPE // 2, 1), const),
        ],
        out_specs=[
            pl.BlockSpec((tm, _GLA_W), row),
            pl.BlockSpec((tm, GLA_QK_WIDTH), row),
            pl.BlockSpec((tm, _GATE_W), row),
            pl.BlockSpec((tm, hw), row),
            pl.BlockSpec((tm, hw), row),
            pl.BlockSpec((1, MLA_HEADS * FLASH_V_ROWS, tm), lambda i: (i // nblk, 0, i % nblk)),
        ],
        out_shape=[
            jax.ShapeDtypeStruct((t, _GLA_W), BF16),
            jax.ShapeDtypeStruct((t, GLA_QK_WIDTH), F32),
            jax.ShapeDtypeStruct((t, _GATE_W), BF16),
            jax.ShapeDtypeStruct((t, hw), BF16),
            jax.ShapeDtypeStruct((t, hw), BF16),
            jax.ShapeDtypeStruct((batch, MLA_HEADS * FLASH_V_ROWS, seq), BF16),
        ],
        compiler_params=_params("parallel"),
        name="inproj",
    )(x2, nw, w_pack, wgk_hi, wgk_lo, bgk, pos, wuq, wuk, wvt, vone, qnw, kvnw, qw, kw, freq)


def _gla_kernel(q_ref, k_ref, v_ref, g_ref, la_ref, nw_ref, o_ref, state_ref, *, n_sub):
    @pl.when(pl.program_id(1) == 0)
    def _():
        state_ref[...] = jnp.zeros_like(state_ref)

    c = GLA_CHUNK
    row = lax.broadcasted_iota(I32, (c, c), 0)
    col = lax.broadcasted_iota(I32, (c, c), 1)
    causal = row >= col
    tri = causal.astype(BF16)
    mid = c // 2 - 1
    for s in range(n_sub):
        r0 = s * c
        la = la_ref[r0:r0 + c, :]
        la_hi, la_lo = _split_bf16(la)
        cum_all = _dot(tri, la_hi) + _dot(tri, la_lo)
        for h in range(GLA_HEADS):
            ks = slice(h * GLA_DK, (h + 1) * GLA_DK)
            vs = slice(h * GLA_DV, (h + 1) * GLA_DV)
            cum = cum_all[:, ks]
            q = q_ref[r0:r0 + c, ks].astype(F32)
            k = k_ref[r0:r0 + c, ks].astype(F32)
            v = v_ref[r0:r0 + c, vs]
            ref_row = cum[mid:mid + 1, :]
            last = cum[c - 1:c, :]
            qg = (q * jnp.exp(cum - ref_row)).astype(BF16)
            kg = (k * jnp.exp(ref_row - cum)).astype(BF16)
            scores = jnp.where(causal, _dot_nt(qg, kg), 0.0).astype(BF16)
            o = _dot(scores, v)
            st = state_ref[h]
            qe = (q * jnp.exp(cum)).astype(BF16)
            o = o + _dot_nt(qe, st.astype(BF16))
            ko = (k * jnp.exp(last - cum)).astype(BF16)
            state_ref[h] = st * jnp.exp(last) + _dot_tn(v, ko)
            ms = jnp.mean(o * o, axis=-1, keepdims=True)
            on = (o * lax.rsqrt(ms + EPS)) * nw_ref[...]
            g = g_ref[r0:r0 + c, vs].astype(F32)
            o_ref[r0:r0 + c, vs] = (on * (g * jax.nn.sigmoid(g))).astype(BF16)


def _gla(gla, la, nw, batch, seq, ts):
    t = gla.shape[0]
    nblk = seq // ts
    kern = functools.partial(_gla_kernel, n_sub=ts // GLA_CHUNK)
    rowmap = lambda col: (lambda b, i: (b * nblk + i, col))
    return pl.pallas_call(
        kern,
        grid=(batch, nblk),
        in_specs=[
            pl.BlockSpec((ts, GLA_QK_WIDTH), rowmap(0)),
            pl.BlockSpec((ts, GLA_QK_WIDTH), rowmap(1)),
            pl.BlockSpec((ts, GLA_V_WIDTH), rowmap(1)),
            pl.BlockSpec((ts, GLA_V_WIDTH), rowmap(2)),
            pl.BlockSpec((ts, GLA_QK_WIDTH), rowmap(0)),
            pl.BlockSpec((1, GLA_DV), lambda b, i: (0, 0)),
        ],
        out_specs=pl.BlockSpec((ts, GLA_V_WIDTH), rowmap(0)),
        out_shape=jax.ShapeDtypeStruct((t, GLA_V_WIDTH), BF16),
        scratch_shapes=[pltpu.VMEM((GLA_HEADS, GLA_DV, GLA_DK), F32)],
        compiler_params=_params("parallel", "arbitrary"),
        name="gla",
    )(gla, gla, gla, gla, la, nw)


def _rope(x, cos, sin, lane):
    half = MLA_ROPE // 2
    rot = jnp.where(lane < half, -pltpu.roll(x, LANES - half, axis=1), pltpu.roll(x, half, axis=1))
    return x * cos + rot * sin


def _mla_prep(mla, pos_ref, wuq_ref, wuk_ref, wvt_ref, vone_ref, qnw_ref, kvnw_ref,
              qw_ref, kw_ref, freq_ref, q_ref, k_ref, vt_ref):
    tm = mla.shape[0]
    cq = mla[:, 0:MLA_Q_RANK]
    ckv = mla[:, MLA_Q_RANK:MLA_Q_RANK + MLA_KV_RANK]
    kr = mla[:, MLA_Q_RANK + MLA_KV_RANK:_MLA_W]

    def rms(a, w):
        ms = jnp.mean(a * a, axis=-1, keepdims=True)
        return (a * lax.rsqrt(ms + EPS)) * w

    q_all = _dot(rms(cq, qnw_ref[...]).astype(BF16), wuq_ref[...])
    ckvn = rms(ckv, kvnw_ref[...]).astype(BF16)
    k_all = _dot(ckvn, wuk_ref[...])
    vt_ref[0] = (_dot_nt(wvt_ref[...], ckvn) + vone_ref[:, 0:1]).astype(BF16)

    half = MLA_ROPE // 2
    ang_t = freq_ref[...] * pos_ref[...]
    cos_t = jnp.cos(ang_t)
    sin_t = jnp.sin(ang_t)
    cos = jnp.concatenate([cos_t, cos_t, jnp.ones((LANES - 2 * half, tm), F32)], axis=0).T
    sin = jnp.concatenate([sin_t, sin_t, jnp.zeros((LANES - 2 * half, tm), F32)], axis=0).T
    lane = lax.broadcasted_iota(I32, (tm, LANES), 1)
    qw_nope = qw_ref[:, 0:MLA_NOPE]
    qw_rope = qw_ref[:, MLA_NOPE:MLA_HEAD_PAD]
    kw_nope = kw_ref[:, 0:MLA_NOPE]
    kw_rope = kw_ref[:, MLA_NOPE:MLA_HEAD_PAD]
    kr_sq = kr * kr
    kr_rot = _rope(kr * kw_rope, cos, sin, lane)
    scale = (MLA_QK ** -0.5) * LOG2_E
    for h in range(MLA_HEADS):
        base = h * MLA_HEAD_PAD
        qn = q_all[:, base:base + MLA_NOPE]
        qr = q_all[:, base + MLA_NOPE:base + MLA_HEAD_PAD]
        ss = jnp.sum(qn * qn + qr * qr, axis=-1, keepdims=True)
        r = lax.rsqrt(ss / MLA_QK + EPS)
        q_ref[:, base:base + MLA_NOPE] = ((qn * r) * qw_nope * scale).astype(BF16)
        q_ref[:, base + MLA_NOPE:base + MLA_HEAD_PAD] = (
            _rope((qr * r) * qw_rope, cos, sin, lane) * scale).astype(BF16)
        kn = k_all[:, h * MLA_NOPE:(h + 1) * MLA_NOPE]
        ssk = jnp.sum(kn * kn + kr_sq, axis=-1, keepdims=True)
        rk = lax.rsqrt(ssk / MLA_QK + EPS)
        k_ref[:, base:base + MLA_NOPE] = ((kn * rk) * kw_nope).astype(BF16)
        k_ref[:, base + MLA_NOPE:base + MLA_HEAD_PAD] = (kr_rot * rk).astype(BF16)


def _flash_kernel(q_ref, k_ref, vt_ref, o_ref, s_ref, cm_ref, acc_ref, m_ref, *, t):
    seq = q_ref.shape[0]
    nq = seq // t
    krow = lax.broadcasted_iota(I32, (t, t), 0)
    qcol = lax.broadcasted_iota(I32, (t, t), 1)
    items = [(qi, j) for qi in range(nq) for j in range(qi + 1)]

    def scores(idx):
        qi, j = items[idx]
        s = _dot_nt(k_ref[j * t:(j + 1) * t, :], q_ref[qi * t:(qi + 1) * t, :])
        s_ref[idx % 2] = s
        cm_ref[idx % 2] = jnp.max(s, axis=0, keepdims=True)

    def update(idx):
        qi, j = items[idx]
        if j == 0:
            m_ref[...] = jnp.full_like(m_ref, -jnp.inf)
            acc_ref[...] = jnp.zeros_like(acc_ref)
        s = s_ref[idx % 2]
        if j == qi:
            s = jnp.where(krow <= qcol, s, -jnp.inf)
            cm = jnp.max(s, axis=0, keepdims=True)
        else:
            cm = cm_ref[idx % 2]
        m_prev = m_ref[...]
        m_new = jnp.maximum(m_prev, cm)
        alpha = jnp.exp2(m_prev - m_new)
        p = jnp.exp2(s - m_new).astype(BF16)
        m_ref[...] = m_new
        acc_ref[...] = alpha * acc_ref[...] + _dot(vt_ref[0, :, j * t:(j + 1) * t], p)
        if j == qi:
            acc = acc_ref[...]
            out_t = acc[0:MLA_V, :] * (1.0 / acc[MLA_V:MLA_V + 1, :])
            o_ref[qi * t:(qi + 1) * t, :] = out_t.T.astype(BF16)

    scores(0)
    for idx in range(len(items)):
        if idx + 1 < len(items):
            scores(idx + 1)
        update(idx)


def _flash(q, k, vt, batch, seq, t):
    rows = q.shape[0]
    return pl.pallas_call(
        functools.partial(_flash_kernel, t=t),
        grid=(batch, MLA_HEADS),
        in_specs=[
            pl.BlockSpec((seq, MLA_HEAD_PAD), lambda b, h: (b, h)),
            pl.BlockSpec((seq, MLA_HEAD_PAD), lambda b, h: (b, h)),
            pl.BlockSpec((1, FLASH_V_ROWS, seq), lambda b, h: (b, h, 0)),
        ],
        out_specs=pl.BlockSpec((seq, MLA_V), lambda b, h: (b, h)),
        out_shape=jax.ShapeDtypeStruct((rows, MLA_HEADS * MLA_V), BF16),
        scratch_shapes=[
            pltpu.VMEM((2, t, t), F32),
            pltpu.VMEM((2, 1, t), F32),
            pltpu.VMEM((FLASH_V_ROWS, t), F32),
            pltpu.VMEM((1, t), F32),
        ],
        compiler_params=_params("parallel", "parallel"),
        name="flash",
    )(q, k, vt)


def _merge_route_kernel(x_ref, oa_ref, ob_ref, gate_ref, wa_ref, wb_ref, wo_ref, nw_ref,
                        wr_hi_ref, wr_lo_ref, br_ref,
                        x1_ref, xn_ref, eid_ref, cw_ref, cnt_ref):
    @pl.when(pl.program_id(0) == 0)
    def _():
        cnt_ref[...] = jnp.zeros_like(cnt_ref)

    tm = x_ref.shape[0]
    ya = _dot(oa_ref[...], wa_ref[...])
    yb = _dot(ob_ref[...], wb_ref[...])
    ga = gate_ref[:, 0:D_MODEL].astype(F32)
    gb = gate_ref[:, D_MODEL:2 * D_MODEL].astype(F32)
    merged = jax.nn.sigmoid(ga) * ya + jax.nn.sigmoid(gb) * yb
    x1 = x_ref[...] + _dot(merged.astype(BF16), wo_ref[...])
    x1_ref[...] = x1
    ms = jnp.mean(x1 * x1, axis=-1, keepdims=True)
    xn = (x1 * lax.rsqrt(ms + EPS)) * nw_ref[...]
    xn_ref[...] = xn
    xn_hi, xn_lo = _split_bf16(xn)
    logits = (_dot(xn_hi, wr_hi_ref[...]) + _dot(xn_lo, wr_hi_ref[...])
              + _dot(xn_hi, wr_lo_ref[...])) + br_ref[...]
    lane = lax.broadcasted_iota(I32, (tm, LANES), 1)
    work = jnp.where(lane < N_EXPERTS, logits, -jnp.inf)
    vals, idxs = [], []
    for _ in range(TOP_K):
        m = jnp.max(work, axis=-1, keepdims=True)
        idx = jnp.min(jnp.where(work == m, lane, LANES), axis=-1, keepdims=True)
        vals.append(m)
        idxs.append(idx)
        work = jnp.where(lane == idx, -jnp.inf, work)
    exps = [jnp.exp(v - vals[0]) for v in vals]
    denom = exps[0] + exps[1] + exps[2] + exps[3]
    eid = jnp.zeros((tm, LANES), I32)
    cw = jnp.zeros((tm, LANES), F32)
    sel = jnp.zeros((tm, LANES), F32)
    for kk in range(TOP_K):
        eid = jnp.where(lane == kk, idxs[kk], eid)
        cw = jnp.where(lane == kk, exps[kk] / denom, cw)
        sel = sel + (lane == idxs[kk]).astype(F32)
    eid_ref[...] = eid
    cw_ref[...] = cw
    cnt_ref[0:1, :] = cnt_ref[0:1, :] + jnp.sum(sel, axis=0, keepdims=True)


def _merge_route(x2, oa, ob, gates, wa, wb, wo, nw, wr_hi, wr_lo, br, tm):
    t = x2.shape[0]
    const = lambda i: (0, 0)
    row = lambda i: (i, 0)
    return pl.pallas_call(
        _merge_route_kernel,
        grid=(t // tm,),
        in_specs=[
            pl.BlockSpec((tm, D_MODEL), row),
            pl.BlockSpec((tm, GLA_V_WIDTH), row),
            pl.BlockSpec((tm, MLA_HEADS * MLA_V), row),
            pl.BlockSpec((tm, _GATE_W), row),
            pl.BlockSpec((GLA_V_WIDTH, D_MODEL), const),
            pl.BlockSpec((MLA_HEADS * MLA_V, D_MODEL), const),
            pl.BlockSpec((D_MODEL, D_MODEL), const),
            pl.BlockSpec((1, D_MODEL), const),
            pl.BlockSpec((D_MODEL, LANES), const),
            pl.BlockSpec((D_MODEL, LANES), const),
            pl.BlockSpec((1, LANES), const),
        ],
        out_specs=[
            pl.BlockSpec((tm, D_MODEL), row),
            pl.BlockSpec((tm, D_MODEL), row),
            pl.BlockSpec((tm, LANES), row),
            pl.BlockSpec((tm, LANES), row),
            pl.BlockSpec((8, LANES), const),
        ],
        out_shape=[
            jax.ShapeDtypeStruct((t, D_MODEL), F32),
            jax.ShapeDtypeStruct((t, D_MODEL), F32),
            jax.ShapeDtypeStruct((t, LANES), I32),
            jax.ShapeDtypeStruct((t, LANES), F32),
            jax.ShapeDtypeStruct((8, LANES), F32),
        ],
        compiler_params=_params("arbitrary"),
        name="merge_route",
    )(x2, oa, ob, gates, wa, wb, wo, nw, wr_hi, wr_lo, br)


def _positions_kernel(eid_ref, cnt_ref, pos_ref, carry_ref, *, tile):
    @pl.when(pl.program_id(0) == 0)
    def _():
        carry_ref[...] = jnp.zeros_like(carry_ref)

    tb = eid_ref.shape[0]
    lane1 = lax.broadcasted_iota(I32, (1, LANES), 1)
    cnt = cnt_ref[0:1, :]
    padded = jnp.floor((cnt + (tile - 1)) / tile) * tile
    incl = padded
    shift = 1
    while shift < N_EXPERTS:
        incl = incl + jnp.where(lane1 >= shift, pltpu.roll(incl, shift, axis=1), 0.0)
        shift *= 2
    offs = incl - padded

    lane = lax.broadcasted_iota(I32, (tb, LANES), 1)
    eid = eid_ref[...]
    onehots = [lane == jnp.broadcast_to(eid[:, kk:kk + 1], (tb, LANES)) for kk in range(TOP_K)]
    sel = jnp.zeros((tb, LANES), F32)
    for oh in onehots:
        sel = sel + oh.astype(F32)
    row = lax.broadcasted_iota(I32, (tb, tb), 0)
    col = lax.broadcasted_iota(I32, (tb, tb), 1)
    strict = (row > col).astype(BF16)
    rank = _dot(strict, sel.astype(BF16)) + carry_ref[...] + offs
    pos = jnp.zeros((tb, LANES), I32)
    for kk in range(TOP_K):
        pk = jnp.sum(jnp.where(onehots[kk], rank, 0.0), axis=-1, keepdims=True)
        pos = jnp.where(lane == kk, pk.astype(I32), pos)
    pos_ref[...] = pos
    carry_ref[...] = carry_ref[...] + jnp.sum(sel, axis=0, keepdims=True)


def _positions(eid, cnt, tb, tile):
    t = eid.shape[0]
    return pl.pallas_call(
        functools.partial(_positions_kernel, tile=tile),
        grid=(t // tb,),
        in_specs=[
            pl.BlockSpec((tb, LANES), lambda i: (i, 0)),
            pl.BlockSpec((8, LANES), lambda i: (0, 0)),
        ],
        out_specs=pl.BlockSpec((tb, LANES), lambda i: (i, 0)),
        out_shape=jax.ShapeDtypeStruct((t, LANES), I32),
        scratch_shapes=[pltpu.VMEM((1, LANES), F32)],
        compiler_params=_params("arbitrary"),
        name="positions",
    )(eid, cnt)


ROW_DMA_UNROLL = 8


def _dispatch_kernel(tz_ref, pos_ref, xn_ref, xs_ref, zero_ref, sem, zsem, *, tile):
    i = pl.program_id(0)
    tb = xn_ref.shape[0]

    def zero_copy(e):
        start = pl.multiple_of(tz_ref[e], tile)
        return pltpu.make_async_copy(zero_ref, xs_ref.at[pl.ds(start, tile), 0], zsem)

    @pl.when(i == 0)
    def _():
        zero_ref[...] = jnp.zeros_like(zero_ref)
        for e in range(N_EXPERTS):
            @pl.when(tz_ref[e] >= 0)
            def _():
                zero_copy(e).start()
        for e in range(N_EXPERTS):
            @pl.when(tz_ref[e] >= 0)
            def _():
                zero_copy(e).wait()

    def issue(t, carry):
        for kk in range(TOP_K):
            p = pos_ref[0, 0, t * TOP_K + kk]
            pltpu.make_async_copy(xn_ref.at[pl.ds(t, 1)], xs_ref.at[p], sem).start(priority=kk % 2)
        return carry

    lax.fori_loop(0, tb, issue, 0, unroll=ROW_DMA_UNROLL)
    for _ in range(TOP_K):
        pltpu.make_async_copy(xn_ref, xs_ref.at[pl.ds(0, tb), 0], sem).wait()


def _dispatch(tile_zero, pos3, xn, n_rows, tb, tile):
    t = xn.shape[0]
    return pl.pallas_call(
        functools.partial(_dispatch_kernel, tile=tile),
        grid_spec=pltpu.PrefetchScalarGridSpec(
            num_scalar_prefetch=1,
            grid=(t // tb,),
            in_specs=[
                pl.BlockSpec((1, 1, tb * TOP_K), lambda i, tz: (i, 0, 0), memory_space=pltpu.SMEM),
                pl.BlockSpec((tb, D_MODEL), lambda i, tz: (i, 0)),
            ],
            out_specs=pl.BlockSpec(memory_space=pl.ANY),
            scratch_shapes=[pltpu.VMEM((tile, D_MODEL), F32), pltpu.SemaphoreType.DMA,
                            pltpu.SemaphoreType.DMA],
        ),
        out_shape=jax.ShapeDtypeStruct((n_rows, 1, D_MODEL), F32),
        compiler_params=_params("arbitrary"),
        name="dispatch",
    )(tile_zero, pos3, xn)


def _experts_kernel(te_ref, nv_ref, xs_ref, wg_ref, bg_ref, wu_ref, bu_ref, wd_ref, bd_ref, ys_ref,
                    wgb_ref, wub_ref, wdb_ref, wsc_ref, xbuf, ybuf, isem, osem, *, tile):
    i = pl.program_id(0)
    nv = nv_ref[0]
    valid = i < nv
    slot = i % 2

    def in_copies(step, s):
        r0 = pl.multiple_of(step * tile, tile)
        return [pltpu.make_async_copy(xs_ref.at[pl.ds(r0, tile), 0, pl.ds(c * LANES, LANES)],
                                      xbuf.at[s, :, pl.ds(c * LANES, LANES)], isem.at[s])
                for c in range(LANE_CHUNKS)]

    def out_copies(step, s):
        r0 = pl.multiple_of(step * tile, tile)
        return [pltpu.make_async_copy(ybuf.at[s, :, pl.ds(c * LANES, LANES)],
                                      ys_ref.at[pl.ds(r0, tile), 0, pl.ds(c * LANES, LANES)],
                                      osem.at[s])
                for c in range(LANE_CHUNKS)]

    @pl.when(i == 0)
    def _():
        for cp in in_copies(0, 0):
            cp.start()

    @pl.when(i + 1 < nv)
    def _():
        for cp in in_copies(i + 1, 1 - slot):
            cp.start()

    new_expert = jnp.logical_or(i == 0, te_ref[i] != te_ref[jnp.maximum(i - 1, 0)])

    @pl.when(jnp.logical_and(valid, new_expert))
    def _():
        for slot_w, (w_ref, w8_ref) in enumerate(((wg_ref, wgb_ref), (wu_ref, wub_ref),
                                                  (wd_ref, wdb_ref))):
            w = w_ref[0]
            sc = _pow2_scale(_abs_max(w))
            w8_ref[...] = (w * sc).astype(F8)
            wsc_ref[slot_w:slot_w + 1, :] = jnp.broadcast_to(1.0 / sc, (1, LANES))

    @pl.when(valid)
    def _():
        for cp in in_copies(i, slot):
            cp.wait()

        @pl.when(i >= 2)
        def _():
            for cp in out_copies(i - 2, slot):
                cp.wait()

        x = xbuf[slot]
        sx = _pow2_scale(_abs_max(x))
        x8 = (x * sx).astype(F8)
        inv_x = 1.0 / sx
        g = _dot(x8, wgb_ref[...]) * (inv_x * wsc_ref[0:1, 0:1]) + bg_ref[0]
        u = _dot(x8, wub_ref[...]) * (inv_x * wsc_ref[1:2, 0:1]) + bu_ref[0]
        g = jnp.minimum(g, SWIGLU_LIMIT)
        u = jnp.clip(u, -SWIGLU_LIMIT, SWIGLU_LIMIT)
        hidden = (u + 1.0) * g * jax.nn.sigmoid(SWIGLU_ALPHA * g)
        h8 = (hidden * HIDDEN_SCALE).astype(F8)
        ybuf[slot] = (_dot(h8, wdb_ref[...]) * (wsc_ref[2:3, 0:1] * (1.0 / HIDDEN_SCALE))
                      + bd_ref[0])
        for cp in out_copies(i, slot):
            cp.start()

    @pl.when(i == nv - 1)
    def _():
        for cp in out_copies(i, slot):
            cp.wait()

        @pl.when(i >= 1)
        def _():
            for cp in out_copies(i - 1, 1 - slot):
                cp.wait()


def _experts(tile_expert, n_valid, xs, wg, bg, wu, bu, wd, bd, tile):
    rows = xs.shape[0]
    nt = rows // tile
    wmap = lambda i, te, nv: (te[i], 0, 0)
    return pl.pallas_call(
        functools.partial(_experts_kernel, tile=tile),
        grid_spec=pltpu.PrefetchScalarGridSpec(
            num_scalar_prefetch=2,
            grid=(nt,),
            in_specs=[
                pl.BlockSpec(memory_space=pl.ANY),
                pl.BlockSpec((1, D_MODEL, D_FF), wmap),
                pl.BlockSpec((1, 1, D_FF), wmap),
                pl.BlockSpec((1, D_MODEL, D_FF), wmap),
                pl.BlockSpec((1, 1, D_FF), wmap),
                pl.BlockSpec((1, D_FF, D_MODEL), wmap),
                pl.BlockSpec((1, 1, D_MODEL), wmap),
            ],
            out_specs=pl.BlockSpec(memory_space=pl.ANY),
            scratch_shapes=[pltpu.VMEM((D_MODEL, D_FF), F8), pltpu.VMEM((D_MODEL, D_FF), F8),
                            pltpu.VMEM((D_FF, D_MODEL), F8), pltpu.VMEM((8, LANES), F32),
                            pltpu.VMEM((2, tile, D_MODEL), F32), pltpu.VMEM((2, tile, D_MODEL), F32),
                            pltpu.SemaphoreType.DMA((2,)), pltpu.SemaphoreType.DMA((2,))],
        ),
        out_shape=jax.ShapeDtypeStruct((rows, 1, D_MODEL), F32),
        compiler_params=_params("arbitrary"),
        name="experts",
    )(tile_expert, n_valid, xs, wg, bg, wu, bu, wd, bd)


def _combine_kernel(pos_ref, posn_ref, x1_ref, cw_ref, ys_ref, o_ref, buf_ref, sem0, sem1):
    i = pl.program_id(0)
    n = pl.num_programs(0)
    tb = x1_ref.shape[0] // 2
    def issue(p_ref, sub, slot, sem):
        def body(t, carry):
            for kk in range(TOP_K):
                p = p_ref[0, 0, (sub * tb + t) * TOP_K + kk]
                pltpu.make_async_copy(ys_ref.at[p], buf_ref.at[slot, kk, pl.ds(t, 1)],
                                      sem).start(priority=kk % 2)
            return carry
        lax.fori_loop(0, tb, body, 0, unroll=ROW_DMA_UNROLL)

    def drain(slot, sem):
        for kk in range(TOP_K):
            pltpu.make_async_copy(ys_ref.at[pl.ds(0, tb), 0], buf_ref.at[slot, kk], sem).wait()

    def reduce(sub, slot):
        rows = slice(sub * tb, (sub + 1) * tb)
        cw = cw_ref[rows, :]
        acc = x1_ref[rows, :]
        for kk in range(TOP_K):
            acc = acc + cw[:, kk:kk + 1] * buf_ref[slot, kk]
        o_ref[rows, :] = acc

    @pl.when(i == 0)
    def _():
        issue(pos_ref, 0, 0, sem0)

    issue(pos_ref, 1, 1, sem1)
    drain(0, sem0)
    reduce(0, 0)

    @pl.when(i + 1 < n)
    def _():
        issue(posn_ref, 0, 0, sem0)

    drain(1, sem1)
    reduce(1, 1)


def _combine(pos3, x1, cw, ys, tb):
    t = x1.shape[0]
    n = t // (2 * tb)
    return pl.pallas_call(
        _combine_kernel,
        grid=(n,),
        in_specs=[
            pl.BlockSpec((1, 1, 2 * tb * TOP_K), lambda i: (i, 0, 0), memory_space=pltpu.SMEM),
            pl.BlockSpec((1, 1, 2 * tb * TOP_K), lambda i: (jnp.minimum(i + 1, n - 1), 0, 0),
                         memory_space=pltpu.SMEM),
            pl.BlockSpec((2 * tb, D_MODEL), lambda i: (i, 0)),
            pl.BlockSpec((2 * tb, LANES), lambda i: (i, 0)),
            pl.BlockSpec(memory_space=pl.ANY),
        ],
        out_specs=pl.BlockSpec((2 * tb, D_MODEL), lambda i: (i, 0)),
        out_shape=jax.ShapeDtypeStruct((t, D_MODEL), F32),
        scratch_shapes=[pltpu.VMEM((2, TOP_K, tb, D_MODEL), F32), pltpu.SemaphoreType.DMA,
                        pltpu.SemaphoreType.DMA],
        compiler_params=_params("arbitrary"),
        name="combine",
    )(pos3, pos3, x1, cw, ys)


def _pad_cols(a, width):
    return jnp.pad(a, ((0, 0), (0, width - a.shape[1])))


def _pack_in_proj(w_in):
    o = np.cumsum((GLA_QK_WIDTH, GLA_QK_WIDTH, GLA_V_WIDTH, GLA_V_WIDTH, GLA_GATE_RANK,
                   MLA_Q_RANK, MLA_KV_RANK, MLA_ROPE, D_MODEL, D_MODEL)).tolist()
    gla = w_in[:, 0:o[3]]
    lr = _pad_cols(w_in[:, o[3]:o[4]], _LR_W)
    mla = _pad_cols(w_in[:, o[4]:o[7]], _MLA_W)
    gates = w_in[:, o[7]:o[9]]
    return jnp.concatenate([gla, lr, mla, gates], axis=1).astype(BF16)


def _pad_heads(w, head_w):
    r = w.shape[0]
    w3 = w.reshape(r, MLA_HEADS, head_w)
    w3 = jnp.pad(w3, ((0, 0), (0, 0), (0, MLA_HEAD_PAD - head_w)))
    return w3.reshape(r, MLA_HEADS * MLA_HEAD_PAD)


def _tile_rows(n, pref):
    return pref if n % pref == 0 else n


def kernel(x, positions, attn_norm_w, w_in, w_gla_gk, b_gla_gk, gla_out_norm_w, w_gla_out,
           mla_q_norm_w, w_mla_uq, mla_kv_norm_w, w_mla_ukv, mla_qk_q_norm_w, mla_qk_k_norm_w,
           w_mla_out, w_out, moe_norm_w, w_router, b_router, w_exp_gate, b_exp_gate,
           w_exp_up, b_exp_up, w_exp_down, b_exp_down):
    batch, seq, _ = x.shape
    depth = w_in.shape[0]
    t = batch * seq
    x2 = x.reshape(t, D_MODEL)
    pos = positions.reshape(1, t).astype(F32)
    half = MLA_ROPE // 2
    freq = (ROPE_BASE ** (-jnp.arange(half, dtype=F32) / half))[:, None]

    tm = _tile_rows(t, 512)
    gla_ts = _tile_rows(seq, 256)
    flash_t = _tile_rows(seq, 512)
    moe_tile = 512
    n_rows = t * TOP_K + N_EXPERTS * moe_tile
    n_rows = (n_rows // moe_tile) * moe_tile
    disp_tb = _tile_rows(t, 512)
    comb_tb = _tile_rows(t, 256)
    pos_tb = _tile_rows(t, 512)

    for l in range(depth):
        w_pack = _pack_in_proj(w_in[l])
        wgk = jnp.pad(w_gla_gk[l], ((0, LANES - GLA_GATE_RANK), (0, 0)))
        wgk_hi, wgk_lo = _split_bf16(wgk)
        wuq = _pad_heads(w_mla_uq[l], MLA_QK).astype(BF16)
        wukv3 = w_mla_ukv[l].reshape(MLA_KV_RANK, MLA_HEADS, MLA_NOPE + MLA_V)
        wuk = wukv3[:, :, :MLA_NOPE].reshape(MLA_KV_RANK, MLA_HEADS * MLA_NOPE).astype(BF16)
        wvt = jnp.pad(wukv3[:, :, MLA_NOPE:].transpose(1, 2, 0),
                      ((0, 0), (0, FLASH_V_ROWS - MLA_V), (0, 0)))
        wvt = wvt.reshape(MLA_HEADS * FLASH_V_ROWS, MLA_KV_RANK).astype(BF16)
        vone = (jnp.arange(MLA_HEADS * FLASH_V_ROWS) % FLASH_V_ROWS == MLA_V).astype(F32)[:, None]
        qw = _pad_cols(mla_qk_q_norm_w[l][None, :], MLA_HEAD_PAD)
        kw = _pad_cols(mla_qk_k_norm_w[l][None, :], MLA_HEAD_PAD)
        gla, la, gates, q, k, vt = _inproj(
            x2, attn_norm_w[l][None, :], w_pack, wgk_hi, wgk_lo, b_gla_gk[l][None, :], pos,
            wuq, wuk, wvt, vone, mla_q_norm_w[l][None, :], mla_kv_norm_w[l][None, :], qw, kw, freq,
            batch, seq, _tile_rows(seq, 512))

        o_a = _gla(gla, la, gla_out_norm_w[l][None, :], batch, seq, gla_ts)
        o_b = _flash(q, k, vt, batch, seq, flash_t)

        wr = _pad_cols(w_router[l], LANES)
        wr_hi, wr_lo = _split_bf16(wr)
        br = _pad_cols(b_router[l][None, :], LANES)
        x1, xn, eid, cw, cnt = _merge_route(
            x2, o_a, o_b, gates, w_gla_out[l].astype(BF16), w_mla_out[l].astype(BF16),
            w_out[l].astype(BF16), moe_norm_w[l][None, :], wr_hi, wr_lo, br, tm)

        posn = _positions(eid, cnt, pos_tb, moe_tile)
        pos4 = posn[:, :TOP_K]
        counts = cnt[0, :N_EXPERTS].astype(I32)
        padded = ((counts + moe_tile - 1) // moe_tile) * moe_tile
        ends = jnp.cumsum(padded)
        n_tiles = n_rows // moe_tile
        n_valid = (ends[-1] // moe_tile).astype(I32)
        starts = jnp.arange(n_tiles, dtype=I32) * moe_tile
        starts = jnp.minimum(starts, ends[-1] - moe_tile)
        tile_expert = jnp.minimum(jnp.sum((starts[:, None] >= ends[None, :]).astype(I32), axis=1),
                                  N_EXPERTS - 1)
        tile_zero = jnp.where(padded > 0, ends - moe_tile, -1).astype(I32)

        xs = _dispatch(tile_zero, pos4.reshape(t // disp_tb, 1, disp_tb * TOP_K), xn, n_rows,
                       disp_tb, moe_tile)
        ys = _experts(tile_expert, n_valid.reshape(1), xs,
                      w_exp_gate[l], b_exp_gate[l][:, None, :],
                      w_exp_up[l], b_exp_up[l][:, None, :],
                      w_exp_down[l], b_exp_down[l][:, None, :], moe_tile)
        x2 = _combine(pos4.reshape(t // (2 * comb_tb), 1, 2 * comb_tb * TOP_K), x1, cw, ys, comb_tb)
    return x2.reshape(batch, seq, D_MODEL)
```

```python
import functools

import jax
import jax.numpy as jnp
import numpy as np
from jax import lax
from jax.experimental import pallas as pl
from jax.experimental.pallas import tpu as pltpu

F32 = jnp.float32
BF16 = jnp.bfloat16
I32 = jnp.int32
F8 = jnp.float8_e4m3fn
F8_TARGET = 224.0
HIDDEN_SCALE = 4.0

D_MODEL = 1024
EPS = 1e-6
GLA_HEADS = 4
GLA_DK = 128
GLA_DV = 256
GLA_GATE_RANK = 16
GLA_GATE_NORMALIZER = 16.0
GLA_QK_WIDTH = GLA_HEADS * GLA_DK
GLA_V_WIDTH = GLA_HEADS * GLA_DV
MLA_HEADS = 8
MLA_Q_RANK = 384
MLA_KV_RANK = 256
MLA_NOPE = 128
MLA_ROPE = 64
MLA_V = 128
MLA_QK = MLA_NOPE + MLA_ROPE
ROPE_BASE = 10000.0
N_EXPERTS = 32
TOP_K = 4
D_FF = 1024
SWIGLU_LIMIT = 7.0
SWIGLU_ALPHA = 1.702

LANES = 128
MLA_HEAD_PAD = 256
GLA_CHUNK = 128
GLA_BLOCK = 256
TOKEN_TILE = 512
FLASH_V_ROWS = 144
LOG2_E = 1.4426950408889634
VMEM_LIMIT = 56 * 1024 * 1024

_GLA_W = 2 * GLA_QK_WIDTH + 2 * GLA_V_WIDTH
_LR_W = LANES
_MLA_W = 768
_GATE_W = 2 * D_MODEL
_PACK_W = _GLA_W + _LR_W + _MLA_W + _GATE_W


def _dot(a, b):
    return jnp.dot(a, b, preferred_element_type=F32)


def _dot_nt(a, b):
    return lax.dot_general(a, b, (((1,), (1,)), ((), ())), preferred_element_type=F32)


def _dot_tn(a, b):
    return lax.dot_general(a, b, (((0,), (0,)), ((), ())), preferred_element_type=F32)


def _abs_max(a):
    return jnp.max(jnp.max(jnp.abs(a), axis=-1, keepdims=True), axis=0, keepdims=True)


def _pow2_scale(amax):
    return jnp.exp2(jnp.floor(jnp.log2(F8_TARGET / jnp.maximum(amax, 1e-30))))


def _split_bf16(a):
    hi = a.astype(BF16)
    lo = (a - hi.astype(F32)).astype(BF16)
    return hi, lo


def _params(*sem):
    return pltpu.CompilerParams(dimension_semantics=sem, vmem_limit_bytes=VMEM_LIMIT)


LANE_CHUNKS = D_MODEL // LANES


def _inproj_kernel(x_ref, nw_ref, w_ref, wgk_hi_ref, wgk_lo_ref, bgk_ref, pos_ref,
                   wuq_ref, wuk_ref, wvt_ref, vone_ref, qnw_ref, kvnw_ref, qw_ref, kw_ref, freq_ref,
                   gla_ref, la_ref, gate_ref, q_ref, k_ref, vt_ref):
    x = x_ref[...]
    ms = jnp.mean(x * x, axis=-1, keepdims=True)
    h = (x * lax.rsqrt(ms + EPS)) * nw_ref[...]
    hb = h.astype(BF16)
    mla = _dot(hb, w_ref[:, _GLA_W + _LR_W:_GLA_W + _LR_W + _MLA_W])
    _mla_prep(mla, pos_ref, wuq_ref, wuk_ref, wvt_ref, vone_ref, qnw_ref, kvnw_ref, qw_ref, kw_ref,
              freq_ref, q_ref, k_ref, vt_ref)
    o0 = 0
    q = _dot(hb, w_ref[:, 0:GLA_QK_WIDTH]) * (GLA_DK ** -0.5)
    gla_ref[:, 0:GLA_QK_WIDTH] = q.astype(BF16)
    gla_ref[:, GLA_QK_WIDTH:_GLA_W] = _dot(hb, w_ref[:, GLA_QK_WIDTH:_GLA_W]).astype(BF16)
    o0 = _GLA_W
    lr = _dot(hb, w_ref[:, o0:o0 + _LR_W])
    lr_hi, lr_lo = _split_bf16(lr)
    a_logit = (_dot(lr_hi, wgk_hi_ref[...]) + _dot(lr_lo, wgk_hi_ref[...])
               + _dot(lr_hi, wgk_lo_ref[...])) + bgk_ref[...]
    log_sig = jnp.minimum(a_logit, 0.0) - jnp.log1p(jnp.exp(-jnp.abs(a_logit)))
    la_ref[...] = log_sig / GLA_GATE_NORMALIZER
    o0 += _LR_W + _MLA_W
    gate_ref[...] = _dot(hb, w_ref[:, o0:o0 + _GATE_W]).astype(BF16)


def _inproj(x2, nw, w_pack, wgk_hi, wgk_lo, bgk, pos, wuq, wuk, wvt, vone, qnw, kvnw, qw, kw, freq,
            batch, seq, tm):
    t = x2.shape[0]
    nblk = seq // tm
    const = lambda i: (0, 0)
    row = lambda i: (i, 0)
    hw = MLA_HEADS * MLA_HEAD_PAD
    return pl.pallas_call(
        _inproj_kernel,
        grid=(t // tm,),
        in_specs=[
            pl.BlockSpec((tm, D_MODEL), row),
            pl.BlockSpec((1, D_MODEL), const),
            pl.BlockSpec((D_MODEL, _PACK_W), const, pipeline_mode=pl.Buffered(1)),
            pl.BlockSpec((LANES, GLA_QK_WIDTH), const),
            pl.BlockSpec((LANES, GLA_QK_WIDTH), const),
            pl.BlockSpec((1, GLA_QK_WIDTH), const),
            pl.BlockSpec((1, tm), lambda i: (0, i)),
            pl.BlockSpec((MLA_Q_RANK, hw), const),
            pl.BlockSpec((MLA_KV_RANK, MLA_HEADS * MLA_NOPE), const),
            pl.BlockSpec((MLA_HEADS * FLASH_V_ROWS, MLA_KV_RANK), const),
            pl.BlockSpec((MLA_HEADS * FLASH_V_ROWS, 1), const),
            pl.BlockSpec((1, MLA_Q_RANK), const),
            pl.BlockSpec((1, MLA_KV_RANK), const),
            pl.BlockSpec((1, MLA_HEAD_PAD), const),
            pl.BlockSpec((1, MLA_HEAD_PAD), const),
            pl.BlockSpec((MLA_ROPE // 2, 1), const),
        ],
        out_specs=[
            pl.BlockSpec((tm, _GLA_W), row),
            pl.BlockSpec((tm, GLA_QK_WIDTH), row),
            pl.BlockSpec((tm, _GATE_W), row),
            pl.BlockSpec((tm, hw), row),
            pl.BlockSpec((tm, hw), row),
            pl.BlockSpec((1, MLA_HEADS * FLASH_V_ROWS, tm), lambda i: (i // nblk, 0, i % nblk)),
        ],
        out_shape=[
            jax.ShapeDtypeStruct((t, _GLA_W), BF16),
            jax.ShapeDtypeStruct((t, GLA_QK_WIDTH), F32),
            jax.ShapeDtypeStruct((t, _GATE_W), BF16),
            jax.ShapeDtypeStruct((t, hw), BF16),
            jax.ShapeDtypeStruct((t, hw), BF16),
            jax.ShapeDtypeStruct((batch, MLA_HEADS * FLASH_V_ROWS, seq), BF16),
        ],
        compiler_params=_params("parallel"),
        name="inproj",
    )(x2, nw, w_pack, wgk_hi, wgk_lo, bgk, pos, wuq, wuk, wvt, vone, qnw, kvnw, qw, kw, freq)


def _gla_kernel(q_ref, k_ref, v_ref, g_ref, la_ref, nw_ref, o_ref, state_ref, *, n_sub):
    @pl.when(pl.program_id(1) == 0)
    def _():
        state_ref[...] = jnp.zeros_like(state_ref)

    c = GLA_CHUNK
    row = lax.broadcasted_iota(I32, (c, c), 0)
    col = lax.broadcasted_iota(I32, (c, c), 1)
    causal = row >= col
    tri = causal.astype(BF16)
    mid = c // 2 - 1
    for s in range(n_sub):
        r0 = s * c
        la = la_ref[r0:r0 + c, :]
        la_hi, la_lo = _split_bf16(la)
        cum_all = _dot(tri, la_hi) + _dot(tri, la_lo)
        for h in range(GLA_HEADS):
            ks = slice(h * GLA_DK, (h + 1) * GLA_DK)
            vs = slice(h * GLA_DV, (h + 1) * GLA_DV)
            cum = cum_all[:, ks]
            q = q_ref[r0:r0 + c, ks].astype(F32)
            k = k_ref[r0:r0 + c, ks].astype(F32)
            v = v_ref[r0:r0 + c, vs]
            ref_row = cum[mid:mid + 1, :]
            last = cum[c - 1:c, :]
            qg = (q * jnp.exp(cum - ref_row)).astype(BF16)
            kg = (k * jnp.exp(ref_row - cum)).astype(BF16)
            scores = jnp.where(causal, _dot_nt(qg, kg), 0.0).astype(BF16)
            o = _dot(scores, v)
            st = state_ref[h]
            qe = (q * jnp.exp(cum)).astype(BF16)
            o = o + _dot_nt(qe, st.astype(BF16))
            ko = (k * jnp.exp(last - cum)).astype(BF16)
            state_ref[h] = st * jnp.exp(last) + _dot_tn(v, ko)
            ms = jnp.mean(o * o, axis=-1, keepdims=True)
            on = (o * lax.rsqrt(ms + EPS)) * nw_ref[...]
            g = g_ref[r0:r0 + c, vs].astype(F32)
            o_ref[r0:r0 + c, vs] = (on * (g * jax.nn.sigmoid(g))).astype(BF16)


def _gla(gla, la, nw, batch, seq, ts):
    t = gla.shape[0]
    nblk = seq // ts
    kern = functools.partial(_gla_kernel, n_sub=ts // GLA_CHUNK)
    rowmap = lambda col: (lambda b, i: (b * nblk + i, col))
    return pl.pallas_call(
        kern,
        grid=(batch, nblk),
        in_specs=[
            pl.BlockSpec((ts, GLA_QK_WIDTH), rowmap(0)),
            pl.BlockSpec((ts, GLA_QK_WIDTH), rowmap(1)),
            pl.BlockSpec((ts, GLA_V_WIDTH), rowmap(1)),
            pl.BlockSpec((ts, GLA_V_WIDTH), rowmap(2)),
            pl.BlockSpec((ts, GLA_QK_WIDTH), rowmap(0)),
            pl.BlockSpec((1, GLA_DV), lambda b, i: (0, 0)),
        ],
        out_specs=pl.BlockSpec((ts, GLA_V_WIDTH), rowmap(0)),
        out_shape=jax.ShapeDtypeStruct((t, GLA_V_WIDTH), BF16),
        scratch_shapes=[pltpu.VMEM((GLA_HEADS, GLA_DV, GLA_DK), F32)],
        compiler_params=_params("parallel", "arbitrary"),
        name="gla",
    )(gla, gla, gla, gla, la, nw)


def _rope(x, cos, sin, lane):
    half = MLA_ROPE // 2
    rot = jnp.where(lane < half, -pltpu.roll(x, LANES - half, axis=1), pltpu.roll(x, half, axis=1))
    return x * cos + rot * sin


def _mla_prep(mla, pos_ref, wuq_ref, wuk_ref, wvt_ref, vone_ref, qnw_ref, kvnw_ref,
              qw_ref, kw_ref, freq_ref, q_ref, k_ref, vt_ref):
    tm = mla.shape[0]
    cq = mla[:, 0:MLA_Q_RANK]
    ckv = mla[:, MLA_Q_RANK:MLA_Q_RANK + MLA_KV_RANK]
    kr = mla[:, MLA_Q_RANK + MLA_KV_RANK:_MLA_W]

    def rms(a, w):
        ms = jnp.mean(a * a, axis=-1, keepdims=True)
        return (a * lax.rsqrt(ms + EPS)) * w

    q_all = _dot(rms(cq, qnw_ref[...]).astype(BF16), wuq_ref[...])
    ckvn = rms(ckv, kvnw_ref[...]).astype(BF16)
    k_all = _dot(ckvn, wuk_ref[...])
    vt_ref[0] = (_dot_nt(wvt_ref[...], ckvn) + vone_ref[:, 0:1]).astype(BF16)

    half = MLA_ROPE // 2
    ang_t = freq_ref[...] * pos_ref[...]
    cos_t = jnp.cos(ang_t)
    sin_t = jnp.sin(ang_t)
    cos = jnp.concatenate([cos_t, cos_t, jnp.ones((LANES - 2 * half, tm), F32)], axis=0).T
    sin = jnp.concatenate([sin_t, sin_t, jnp.zeros((LANES - 2 * half, tm), F32)], axis=0).T
    lane = lax.broadcasted_iota(I32, (tm, LANES), 1)
    qw_nope = qw_ref[:, 0:MLA_NOPE]
    qw_rope = qw_ref[:, MLA_NOPE:MLA_HEAD_PAD]
    kw_nope = kw_ref[:, 0:MLA_NOPE]
    kw_rope = kw_ref[:, MLA_NOPE:MLA_HEAD_PAD]
    kr_sq = kr * kr
    kr_rot = _rope(kr * kw_rope, cos, sin, lane)
    scale = (MLA_QK ** -0.5) * LOG2_E
    for h in range(MLA_HEADS):
        base = h * MLA_HEAD_PAD
        qn = q_all[:, base:base + MLA_NOPE]
        qr = q_all[:, base + MLA_NOPE:base + MLA_HEAD_PAD]
        ss = jnp.sum(qn * qn + qr * qr, axis=-1, keepdims=True)
        r = lax.rsqrt(ss / MLA_QK + EPS)
        q_ref[:, base:base + MLA_NOPE] = ((qn * r) * qw_nope * scale).astype(BF16)
        q_ref[:, base + MLA_NOPE:base + MLA_HEAD_PAD] = (
            _rope((qr * r) * qw_rope, cos, sin, lane) * scale).astype(BF16)
        kn = k_all[:, h * MLA_NOPE:(h + 1) * MLA_NOPE]
        ssk = jnp.sum(kn * kn + kr_sq, axis=-1, keepdims=True)
        rk = lax.rsqrt(ssk / MLA_QK + EPS)
        k_ref[:, base:base + MLA_NOPE] = ((kn * rk) * kw_nope).astype(BF16)
        k_ref[:, base + MLA_NOPE:base + MLA_HEAD_PAD] = (kr_rot * rk).astype(BF16)


def _flash_kernel(q_ref, k_ref, vt_ref, o_ref, s_ref, cm_ref, acc_ref, m_ref, *, t):
    seq = q_ref.shape[0]
    nq = seq // t
    krow = lax.broadcasted_iota(I32, (t, t), 0)
    qcol = lax.broadcasted_iota(I32, (t, t), 1)
    items = [(qi, j) for qi in range(nq) for j in range(qi + 1)]

    def scores(idx):
        qi, j = items[idx]
        s = _dot_nt(k_ref[j * t:(j + 1) * t, :], q_ref[qi * t:(qi + 1) * t, :])
        s_ref[idx % 2] = s
        cm_ref[idx % 2] = jnp.max(s, axis=0, keepdims=True)

    def update(idx):
        qi, j = items[idx]
        if j == 0:
            m_ref[...] = jnp.full_like(m_ref, -jnp.inf)
            acc_ref[...] = jnp.zeros_like(acc_ref)
        s = s_ref[idx % 2]
        if j == qi:
            s = jnp.where(krow <= qcol, s, -jnp.inf)
            cm = jnp.max(s, axis=0, keepdims=True)
        else:
            cm = cm_ref[idx % 2]
        m_prev = m_ref[...]
        m_new = jnp.maximum(m_prev, cm)
        alpha = jnp.exp2(m_prev - m_new)
        p = jnp.exp2(s - m_new).astype(BF16)
        m_ref[...] = m_new
        acc_ref[...] = alpha * acc_ref[...] + _dot(vt_ref[0, :, j * t:(j + 1) * t], p)
        if j == qi:
            acc = acc_ref[...]
            out_t = acc[0:MLA_V, :] * (1.0 / acc[MLA_V:MLA_V + 1, :])
            o_ref[qi * t:(qi + 1) * t, :] = out_t.T.astype(BF16)

    scores(0)
    for idx in range(len(items)):
        if idx + 1 < len(items):
            scores(idx + 1)
        update(idx)


def _flash(q, k, vt, batch, seq, t):
    rows = q.shape[0]
    return pl.pallas_call(
        functools.partial(_flash_kernel, t=t),
        grid=(batch, MLA_HEADS),
        in_specs=[
            pl.BlockSpec((seq, MLA_HEAD_PAD), lambda b, h: (b, h)),
            pl.BlockSpec((seq, MLA_HEAD_PAD), lambda b, h: (b, h)),
            pl.BlockSpec((1, FLASH_V_ROWS, seq), lambda b, h: (b, h, 0)),
        ],
        out_specs=pl.BlockSpec((seq, MLA_V), lambda b, h: (b, h)),
        out_shape=jax.ShapeDtypeStruct((rows, MLA_HEADS * MLA_V), BF16),
        scratch_shapes=[
            pltpu.VMEM((2, t, t), F32),
            pltpu.VMEM((2, 1, t), F32),
            pltpu.VMEM((FLASH_V_ROWS, t), F32),
            pltpu.VMEM((1, t), F32),
        ],
        compiler_params=_params("parallel", "parallel"),
        name="flash",
    )(q, k, vt)


def _merge_route_kernel(x_ref, oa_ref, ob_ref, gate_ref, wa_ref, wb_ref, wo_ref, nw_ref,
                        wr_hi_ref, wr_lo_ref, br_ref,
                        x1_ref, xn_ref, eid_ref, cw_ref, cnt_ref):
    @pl.when(pl.program_id(0) == 0)
    def _():
        cnt_ref[...] = jnp.zeros_like(cnt_ref)

    tm = x_ref.shape[0]
    ya = _dot(oa_ref[...], wa_ref[...])
    yb = _dot(ob_ref[...], wb_ref[...])
    ga = gate_ref[:, 0:D_MODEL].astype(F32)
    gb = gate_ref[:, D_MODEL:2 * D_MODEL].astype(F32)
    merged = jax.nn.sigmoid(ga) * ya + jax.nn.sigmoid(gb) * yb
    x1 = x_ref[...] + _dot(merged.astype(BF16), wo_ref[...])
    x1_ref[...] = x1
    ms = jnp.mean(x1 * x1, axis=-1, keepdims=True)
    xn = (x1 * lax.rsqrt(ms + EPS)) * nw_ref[...]
    xn_ref[...] = xn
    xn_hi, xn_lo = _split_bf16(xn)
    logits = (_dot(xn_hi, wr_hi_ref[...]) + _dot(xn_lo, wr_hi_ref[...])
              + _dot(xn_hi, wr_lo_ref[...])) + br_ref[...]
    lane = lax.broadcasted_iota(I32, (tm, LANES), 1)
    work = jnp.where(lane < N_EXPERTS, logits, -jnp.inf)
    vals, idxs = [], []
    for _ in range(TOP_K):
        m = jnp.max(work, axis=-1, keepdims=True)
        idx = jnp.min(jnp.where(work == m, lane, LANES), axis=-1, keepdims=True)
        vals.append(m)
        idxs.append(idx)
        work = jnp.where(lane == idx, -jnp.inf, work)
    exps = [jnp.exp(v - vals[0]) for v in vals]
    denom = exps[0] + exps[1] + exps[2] + exps[3]
    eid = jnp.zeros((tm, LANES), I32)
    cw = jnp.zeros((tm, LANES), F32)
    sel = jnp.zeros((tm, LANES), F32)
    for kk in range(TOP_K):
        eid = jnp.where(lane == kk, idxs[kk], eid)
        cw = jnp.where(lane == kk, exps[kk] / denom, cw)
        sel = sel + (lane == idxs[kk]).astype(F32)
    eid_ref[...] = eid
    cw_ref[...] = cw
    cnt_ref[0:1, :] = cnt_ref[0:1, :] + jnp.sum(sel, axis=0, keepdims=True)


def _merge_route(x2, oa, ob, gates, wa, wb, wo, nw, wr_hi, wr_lo, br, tm):
    t = x2.shape[0]
    const = lambda i: (0, 0)
    row = lambda i: (i, 0)
    return pl.pallas_call(
        _merge_route_kernel,
        grid=(t // tm,),
        in_specs=[
            pl.BlockSpec((tm, D_MODEL), row),
            pl.BlockSpec((tm, GLA_V_WIDTH), row),
            pl.BlockSpec((tm, MLA_HEADS * MLA_V), row),
            pl.BlockSpec((tm, _GATE_W), row),
            pl.BlockSpec((GLA_V_WIDTH, D_MODEL), const),
            pl.BlockSpec((MLA_HEADS * MLA_V, D_MODEL), const),
            pl.BlockSpec((D_MODEL, D_MODEL), const),
            pl.BlockSpec((1, D_MODEL), const),
            pl.BlockSpec((D_MODEL, LANES), const),
            pl.BlockSpec((D_MODEL, LANES), const),
            pl.BlockSpec((1, LANES), const),
        ],
        out_specs=[
            pl.BlockSpec((tm, D_MODEL), row),
            pl.BlockSpec((tm, D_MODEL), row),
            pl.BlockSpec((tm, LANES), row),
            pl.BlockSpec((tm, LANES), row),
            pl.BlockSpec((8, LANES), const),
        ],
        out_shape=[
            jax.ShapeDtypeStruct((t, D_MODEL), F32),
            jax.ShapeDtypeStruct((t, D_MODEL), F32),
            jax.ShapeDtypeStruct((t, LANES), I32),
            jax.ShapeDtypeStruct((t, LANES), F32),
            jax.ShapeDtypeStruct((8, LANES), F32),
        ],
        compiler_params=_params("arbitrary"),
        name="merge_route",
    )(x2, oa, ob, gates, wa, wb, wo, nw, wr_hi, wr_lo, br)


def _positions_kernel(eid_ref, cnt_ref, pos_ref, carry_ref, *, tile):
    @pl.when(pl.program_id(0) == 0)
    def _():
        carry_ref[...] = jnp.zeros_like(carry_ref)

    tb = eid_ref.shape[0]
    lane1 = lax.broadcasted_iota(I32, (1, LANES), 1)
    cnt = cnt_ref[0:1, :]
    padded = jnp.floor((cnt + (tile - 1)) / tile) * tile
    incl = padded
    shift = 1
    while shift < N_EXPERTS:
        incl = incl + jnp.where(lane1 >= shift, pltpu.roll(incl, shift, axis=1), 0.0)
        shift *= 2
    offs = incl - padded

    lane = lax.broadcasted_iota(I32, (tb, LANES), 1)
    eid = eid_ref[...]
    onehots = [lane == jnp.broadcast_to(eid[:, kk:kk + 1], (tb, LANES)) for kk in range(TOP_K)]
    sel = jnp.zeros((tb, LANES), F32)
    for oh in onehots:
        sel = sel + oh.astype(F32)
    row = lax.broadcasted_iota(I32, (tb, tb), 0)
    col = lax.broadcasted_iota(I32, (tb, tb), 1)
    strict = (row > col).astype(BF16)
    rank = _dot(strict, sel.astype(BF16)) + carry_ref[...] + offs
    pos = jnp.zeros((tb, LANES), I32)
    for kk in range(TOP_K):
        pk = jnp.sum(jnp.where(onehots[kk], rank, 0.0), axis=-1, keepdims=True)
        pos = jnp.where(lane == kk, pk.astype(I32), pos)
    pos_ref[0] = pos.T[0:8, :]
    carry_ref[...] = carry_ref[...] + jnp.sum(sel, axis=0, keepdims=True)


def _positions(eid, cnt, tb, tile):
    t = eid.shape[0]
    return pl.pallas_call(
        functools.partial(_positions_kernel, tile=tile),
        grid=(t // tb,),
        in_specs=[
            pl.BlockSpec((tb, LANES), lambda i: (i, 0)),
            pl.BlockSpec((8, LANES), lambda i: (0, 0)),
        ],
        out_specs=pl.BlockSpec((1, 8, tb), lambda i: (i, 0, 0)),
        out_shape=jax.ShapeDtypeStruct((t // tb, 8, tb), I32),
        scratch_shapes=[pltpu.VMEM((1, LANES), F32)],
        compiler_params=_params("arbitrary"),
        name="positions",
    )(eid, cnt)


ROW_DMA_UNROLL = 8


def _dispatch_kernel(tz_ref, pos_ref, xn_ref, xs_ref, zero_ref, sem, zsem, *, tile):
    i = pl.program_id(0)
    tb = xn_ref.shape[0]

    def zero_copy(e):
        start = pl.multiple_of(tz_ref[e], tile)
        return pltpu.make_async_copy(zero_ref, xs_ref.at[pl.ds(start, tile), 0], zsem)

    @pl.when(i == 0)
    def _():
        zero_ref[...] = jnp.zeros_like(zero_ref)
        for e in range(N_EXPERTS):
            @pl.when(tz_ref[e] >= 0)
            def _():
                zero_copy(e).start()
        for e in range(N_EXPERTS):
            @pl.when(tz_ref[e] >= 0)
            def _():
                zero_copy(e).wait()

    def issue(t, carry):
        for kk in range(TOP_K):
            p = pos_ref[0, kk, t]
            pltpu.make_async_copy(xn_ref.at[pl.ds(t, 1)], xs_ref.at[p], sem).start(priority=kk % 2)
        return carry

    lax.fori_loop(0, tb, issue, 0, unroll=ROW_DMA_UNROLL)
    for _ in range(TOP_K):
        pltpu.make_async_copy(xn_ref, xs_ref.at[pl.ds(0, tb), 0], sem).wait()


def _dispatch(tile_zero, pos3, xn, n_rows, tb, tile):
    t = xn.shape[0]
    return pl.pallas_call(
        functools.partial(_dispatch_kernel, tile=tile),
        grid_spec=pltpu.PrefetchScalarGridSpec(
            num_scalar_prefetch=1,
            grid=(t // tb,),
            in_specs=[
                pl.BlockSpec((1, 8, tb), lambda i, tz: (i, 0, 0), memory_space=pltpu.SMEM),
                pl.BlockSpec((tb, D_MODEL), lambda i, tz: (i, 0)),
            ],
            out_specs=pl.BlockSpec(memory_space=pl.ANY),
            scratch_shapes=[pltpu.VMEM((tile, D_MODEL), F32), pltpu.SemaphoreType.DMA,
                            pltpu.SemaphoreType.DMA],
        ),
        out_shape=jax.ShapeDtypeStruct((n_rows, 1, D_MODEL), F32),
        compiler_params=_params("arbitrary"),
        name="dispatch",
    )(tile_zero, pos3, xn)


def _experts_kernel(te_ref, nv_ref, xs_ref, wg_ref, bg_ref, wu_ref, bu_ref, wd_ref, bd_ref, ys_ref,
                    wgb_ref, wub_ref, wdb_ref, wsc_ref, xbuf, ybuf, isem, osem, *, tile):
    i = pl.program_id(0)
    nv = nv_ref[0]
    valid = i < nv
    slot = i % 2

    def in_copies(step, s):
        r0 = pl.multiple_of(step * tile, tile)
        return [pltpu.make_async_copy(xs_ref.at[pl.ds(r0, tile), 0, pl.ds(c * LANES, LANES)],
                                      xbuf.at[s, :, pl.ds(c * LANES, LANES)], isem.at[s])
                for c in range(LANE_CHUNKS)]

    def out_copies(step, s):
        r0 = pl.multiple_of(step * tile, tile)
        return [pltpu.make_async_copy(ybuf.at[s, :, pl.ds(c * LANES, LANES)],
                                      ys_ref.at[pl.ds(r0, tile), 0, pl.ds(c * LANES, LANES)],
                                      osem.at[s])
                for c in range(LANE_CHUNKS)]

    @pl.when(i == 0)
    def _():
        for cp in in_copies(0, 0):
            cp.start()

    @pl.when(i + 1 < nv)
    def _():
        for cp in in_copies(i + 1, 1 - slot):
            cp.start()

    new_expert = jnp.logical_or(i == 0, te_ref[i] != te_ref[jnp.maximum(i - 1, 0)])

    @pl.when(jnp.logical_and(valid, new_expert))
    def _():
        for slot_w, (w_ref, w8_ref) in enumerate(((wg_ref, wgb_ref), (wu_ref, wub_ref),
                                                  (wd_ref, wdb_ref))):
            w = w_ref[0]
            sc = _pow2_scale(_abs_max(w))
            w8_ref[...] = (w * sc).astype(F8)
            wsc_ref[slot_w:slot_w + 1, :] = jnp.broadcast_to(1.0 / sc, (1, LANES))

    @pl.when(valid)
    def _():
        for cp in in_copies(i, slot):
            cp.wait()

        @pl.when(i >= 2)
        def _():
            for cp in out_copies(i - 2, slot):
                cp.wait()

        x = xbuf[slot]
        sx = _pow2_scale(_abs_max(x))
        x8 = (x * sx).astype(F8)
        inv_x = 1.0 / sx
        g = _dot(x8, wgb_ref[...]) * (inv_x * wsc_ref[0:1, 0:1]) + bg_ref[0]
        u = _dot(x8, wub_ref[...]) * (inv_x * wsc_ref[1:2, 0:1]) + bu_ref[0]
        g = jnp.minimum(g, SWIGLU_LIMIT)
        u = jnp.clip(u, -SWIGLU_LIMIT, SWIGLU_LIMIT)
        hidden = (u + 1.0) * g * jax.nn.sigmoid(SWIGLU_ALPHA * g)
        h8 = (hidden * HIDDEN_SCALE).astype(F8)
        ybuf[slot] = (_dot(h8, wdb_ref[...]) * (wsc_ref[2:3, 0:1] * (1.0 / HIDDEN_SCALE))
                      + bd_ref[0])
        for cp in out_copies(i, slot):
            cp.start()

    @pl.when(i == nv - 1)
    def _():
        for cp in out_copies(i, slot):
            cp.wait()

        @pl.when(i >= 1)
        def _():
            for cp in out_copies(i - 1, 1 - slot):
                cp.wait()


def _experts(tile_expert, n_valid, xs, wg, bg, wu, bu, wd, bd, tile):
    rows = xs.shape[0]
    nt = rows // tile
    wmap = lambda i, te, nv: (te[i], 0, 0)
    return pl.pallas_call(
        functools.partial(_experts_kernel, tile=tile),
        grid_spec=pltpu.PrefetchScalarGridSpec(
            num_scalar_prefetch=2,
            grid=(nt,),
            in_specs=[
                pl.BlockSpec(memory_space=pl.ANY),
                pl.BlockSpec((1, D_MODEL, D_FF), wmap),
                pl.BlockSpec((1, 1, D_FF), wmap),
                pl.BlockSpec((1, D_MODEL, D_FF), wmap),
                pl.BlockSpec((1, 1, D_FF), wmap),
                pl.BlockSpec((1, D_FF, D_MODEL), wmap),
                pl.BlockSpec((1, 1, D_MODEL), wmap),
            ],
            out_specs=pl.BlockSpec(memory_space=pl.ANY),
            scratch_shapes=[pltpu.VMEM((D_MODEL, D_FF), F8), pltpu.VMEM((D_MODEL, D_FF), F8),
                            pltpu.VMEM((D_FF, D_MODEL), F8), pltpu.VMEM((8, LANES), F32),
                            pltpu.VMEM((2, tile, D_MODEL), F32), pltpu.VMEM((2, tile, D_MODEL), F32),
                            pltpu.SemaphoreType.DMA((2,)), pltpu.SemaphoreType.DMA((2,))],
        ),
        out_shape=jax.ShapeDtypeStruct((rows, 1, D_MODEL), F32),
        compiler_params=_params("arbitrary"),
        name="experts",
    )(tile_expert, n_valid, xs, wg, bg, wu, bu, wd, bd)


def _combine_kernel(pos_ref, posn_ref, x1_ref, cw_ref, ys_ref, o_ref, buf_ref, sem0, sem1):
    i = pl.program_id(0)
    n = pl.num_programs(0)
    tb = x1_ref.shape[0] // 2
    def issue(p_ref, sub, slot, sem):
        def body(t, carry):
            for kk in range(TOP_K):
                p = p_ref[0, kk, sub * tb + t]
                pltpu.make_async_copy(ys_ref.at[p], buf_ref.at[slot, kk, pl.ds(t, 1)],
                                      sem).start(priority=kk % 2)
            return carry
        lax.fori_loop(0, tb, body, 0, unroll=ROW_DMA_UNROLL)

    def drain(slot, sem):
        for kk in range(TOP_K):
            pltpu.make_async_copy(ys_ref.at[pl.ds(0, tb), 0], buf_ref.at[slot, kk], sem).wait()

    def reduce(sub, slot):
        rows = slice(sub * tb, (sub + 1) * tb)
        cw = cw_ref[rows, :]
        acc = x1_ref[rows, :]
        for kk in range(TOP_K):
            acc = acc + cw[:, kk:kk + 1] * buf_ref[slot, kk]
        o_ref[rows, :] = acc

    @pl.when(i == 0)
    def _():
        issue(pos_ref, 0, 0, sem0)

    issue(pos_ref, 1, 1, sem1)
    drain(0, sem0)
    reduce(0, 0)

    @pl.when(i + 1 < n)
    def _():
        issue(posn_ref, 0, 0, sem0)

    drain(1, sem1)
    reduce(1, 1)


def _combine(pos3, x1, cw, ys, tb):
    t = x1.shape[0]
    n = t // (2 * tb)
    return pl.pallas_call(
        _combine_kernel,
        grid=(n,),
        in_specs=[
            pl.BlockSpec((1, 8, 2 * tb), lambda i: (i, 0, 0), memory_space=pltpu.SMEM),
            pl.BlockSpec((1, 8, 2 * tb), lambda i: (jnp.minimum(i + 1, n - 1), 0, 0),
                         memory_space=pltpu.SMEM),
            pl.BlockSpec((2 * tb, D_MODEL), lambda i: (i, 0)),
            pl.BlockSpec((2 * tb, LANES), lambda i: (i, 0)),
            pl.BlockSpec(memory_space=pl.ANY),
        ],
        out_specs=pl.BlockSpec((2 * tb, D_MODEL), lambda i: (i, 0)),
        out_shape=jax.ShapeDtypeStruct((t, D_MODEL), F32),
        scratch_shapes=[pltpu.VMEM((2, TOP_K, tb, D_MODEL), F32), pltpu.SemaphoreType.DMA,
                        pltpu.SemaphoreType.DMA],
        compiler_params=_params("arbitrary"),
        name="combine",
    )(pos3, pos3, x1, cw, ys)


def _pad_cols(a, width):
    return jnp.pad(a, ((0, 0), (0, width - a.shape[1])))


def _pack_in_proj(w_in):
    o = np.cumsum((GLA_QK_WIDTH, GLA_QK_WIDTH, GLA_V_WIDTH, GLA_V_WIDTH, GLA_GATE_RANK,
                   MLA_Q_RANK, MLA_KV_RANK, MLA_ROPE, D_MODEL, D_MODEL)).tolist()
    gla = w_in[:, 0:o[3]]
    lr = _pad_cols(w_in[:, o[3]:o[4]], _LR_W)
    mla = _pad_cols(w_in[:, o[4]:o[7]], _MLA_W)
    gates = w_in[:, o[7]:o[9]]
    return jnp.concatenate([gla, lr, mla, gates], axis=1).astype(BF16)


def _pad_heads(w, head_w):
    r = w.shape[0]
    w3 = w.reshape(r, MLA_HEADS, head_w)
    w3 = jnp.pad(w3, ((0, 0), (0, 0), (0, MLA_HEAD_PAD - head_w)))
    return w3.reshape(r, MLA_HEADS * MLA_HEAD_PAD)


def _tile_rows(n, pref):
    return pref if n % pref == 0 else n


def kernel(x, positions, attn_norm_w, w_in, w_gla_gk, b_gla_gk, gla_out_norm_w, w_gla_out,
           mla_q_norm_w, w_mla_uq, mla_kv_norm_w, w_mla_ukv, mla_qk_q_norm_w, mla_qk_k_norm_w,
           w_mla_out, w_out, moe_norm_w, w_router, b_router, w_exp_gate, b_exp_gate,
           w_exp_up, b_exp_up, w_exp_down, b_exp_down):
    batch, seq, _ = x.shape
    depth = w_in.shape[0]
    t = batch * seq
    x2 = x.reshape(t, D_MODEL)
    pos = positions.reshape(1, t).astype(F32)
    half = MLA_ROPE // 2
    freq = (ROPE_BASE ** (-jnp.arange(half, dtype=F32) / half))[:, None]

    tm = _tile_rows(t, TOKEN_TILE)
    seq_tile = _tile_rows(seq, TOKEN_TILE)
    gla_ts = _tile_rows(seq, GLA_BLOCK)
    moe_tile = TOKEN_TILE
    n_rows = ((t * TOP_K) // moe_tile + N_EXPERTS) * moe_tile

    for l in range(depth):
        w_pack = _pack_in_proj(w_in[l])
        wgk = jnp.pad(w_gla_gk[l], ((0, LANES - GLA_GATE_RANK), (0, 0)))
        wgk_hi, wgk_lo = _split_bf16(wgk)
        wuq = _pad_heads(w_mla_uq[l], MLA_QK).astype(BF16)
        wukv3 = w_mla_ukv[l].reshape(MLA_KV_RANK, MLA_HEADS, MLA_NOPE + MLA_V)
        wuk = wukv3[:, :, :MLA_NOPE].reshape(MLA_KV_RANK, MLA_HEADS * MLA_NOPE).astype(BF16)
        wvt = jnp.pad(wukv3[:, :, MLA_NOPE:].transpose(1, 2, 0),
                      ((0, 0), (0, FLASH_V_ROWS - MLA_V), (0, 0)))
        wvt = wvt.reshape(MLA_HEADS * FLASH_V_ROWS, MLA_KV_RANK).astype(BF16)
        vone = (jnp.arange(MLA_HEADS * FLASH_V_ROWS) % FLASH_V_ROWS == MLA_V).astype(F32)[:, None]
        qw = _pad_cols(mla_qk_q_norm_w[l][None, :], MLA_HEAD_PAD)
        kw = _pad_cols(mla_qk_k_norm_w[l][None, :], MLA_HEAD_PAD)
        gla, la, gates, q, k, vt = _inproj(
            x2, attn_norm_w[l][None, :], w_pack, wgk_hi, wgk_lo, b_gla_gk[l][None, :], pos,
            wuq, wuk, wvt, vone, mla_q_norm_w[l][None, :], mla_kv_norm_w[l][None, :], qw, kw, freq,
            batch, seq, seq_tile)

        o_a = _gla(gla, la, gla_out_norm_w[l][None, :], batch, seq, gla_ts)
        o_b = _flash(q, k, vt, batch, seq, seq_tile)

        wr = _pad_cols(w_router[l], LANES)
        wr_hi, wr_lo = _split_bf16(wr)
        br = _pad_cols(b_router[l][None, :], LANES)
        x1, xn, eid, cw, cnt = _merge_route(
            x2, o_a, o_b, gates, w_gla_out[l].astype(BF16), w_mla_out[l].astype(BF16),
            w_out[l].astype(BF16), moe_norm_w[l][None, :], wr_hi, wr_lo, br, tm)

        pos3 = _positions(eid, cnt, tm, moe_tile)
        counts = cnt[0, :N_EXPERTS].astype(I32)
        padded = ((counts + moe_tile - 1) // moe_tile) * moe_tile
        ends = jnp.cumsum(padded)
        n_tiles = n_rows // moe_tile
        n_valid = (ends[-1] // moe_tile).astype(I32)
        starts = jnp.arange(n_tiles, dtype=I32) * moe_tile
        starts = jnp.minimum(starts, ends[-1] - moe_tile)
        tile_expert = jnp.minimum(jnp.sum((starts[:, None] >= ends[None, :]).astype(I32), axis=1),
                                  N_EXPERTS - 1)
        tile_zero = jnp.where(padded > 0, ends - moe_tile, -1).astype(I32)

        xs = _dispatch(tile_zero, pos3, xn, n_rows, tm, moe_tile)
        ys = _experts(tile_expert, n_valid.reshape(1), xs,
                      w_exp_gate[l], b_exp_gate[l][:, None, :],
                      w_exp_up[l], b_exp_up[l][:, None, :],
                      w_exp_down[l], b_exp_down[l][:, None, :], moe_tile)
        x2 = _combine(pos3, x1, cw, ys, tm // 2)
    return x2.reshape(batch, seq, D_MODEL)
```

```python
import functools

import jax
import jax.numpy as jnp
import numpy as np
from jax import lax
from jax.experimental import pallas as pl
from jax.experimental.pallas import tpu as pltpu

F32 = jnp.float32
BF16 = jnp.bfloat16
I32 = jnp.int32
F8 = jnp.float8_e4m3fn
F8_TARGET = 224.0
HIDDEN_SCALE = 4.0

D_MODEL = 1024
EPS = 1e-6
GLA_HEADS = 4
GLA_DK = 128
GLA_DV = 256
GLA_GATE_RANK = 16
GLA_GATE_NORMALIZER = 16.0
GLA_QK_WIDTH = GLA_HEADS * GLA_DK
GLA_V_WIDTH = GLA_HEADS * GLA_DV
MLA_HEADS = 8
MLA_Q_RANK = 384
MLA_KV_RANK = 256
MLA_NOPE = 128
MLA_ROPE = 64
MLA_V = 128
MLA_QK = MLA_NOPE + MLA_ROPE
ROPE_BASE = 10000.0
N_EXPERTS = 32
TOP_K = 4
D_FF = 1024
SWIGLU_LIMIT = 7.0
SWIGLU_ALPHA = 1.702

LANES = 128
MLA_HEAD_PAD = 256
GLA_CHUNK = 128
GLA_BLOCK = 256
TOKEN_TILE = 512
FLASH_V_ROWS = 144
LOG2_E = 1.4426950408889634
VMEM_LIMIT = 56 * 1024 * 1024

_GLA_W = 2 * GLA_QK_WIDTH + 2 * GLA_V_WIDTH
_LR_W = LANES
_MLA_W = 768
_GATE_W = 2 * D_MODEL
_PACK_W = _GLA_W + _LR_W + _MLA_W + _GATE_W


def _dot(a, b):
    return jnp.dot(a, b, preferred_element_type=F32)


def _dot_nt(a, b):
    return lax.dot_general(a, b, (((1,), (1,)), ((), ())), preferred_element_type=F32)


def _dot_tn(a, b):
    return lax.dot_general(a, b, (((0,), (0,)), ((), ())), preferred_element_type=F32)


def _abs_max(a):
    return jnp.max(jnp.max(jnp.abs(a), axis=-1, keepdims=True), axis=0, keepdims=True)


def _pow2_scale(amax):
    return jnp.exp2(jnp.floor(jnp.log2(F8_TARGET / jnp.maximum(amax, 1e-30))))


def _split_bf16(a):
    hi = a.astype(BF16)
    lo = (a - hi.astype(F32)).astype(BF16)
    return hi, lo


def _params(*sem):
    return pltpu.CompilerParams(dimension_semantics=sem, vmem_limit_bytes=VMEM_LIMIT)


LANE_CHUNKS = D_MODEL // LANES


def _inproj_kernel(x_ref, nw_ref, w_ref, wgk_hi_ref, wgk_lo_ref, bgk_ref, pos_ref,
                   wuq_ref, wuk_ref, wvt_ref, vone_ref, qnw_ref, kvnw_ref, qw_ref, kw_ref, freq_ref,
                   gla_ref, la_ref, gate_ref, q_ref, k_ref, vt_ref):
    x = x_ref[...]
    ms = jnp.mean(x * x, axis=-1, keepdims=True)
    h = (x * lax.rsqrt(ms + EPS)) * nw_ref[...]
    hb = h.astype(BF16)
    mla = _dot(hb, w_ref[:, _GLA_W + _LR_W:_GLA_W + _LR_W + _MLA_W])
    _mla_prep(mla, pos_ref, wuq_ref, wuk_ref, wvt_ref, vone_ref, qnw_ref, kvnw_ref, qw_ref, kw_ref,
              freq_ref, q_ref, k_ref, vt_ref)
    o0 = 0
    q = _dot(hb, w_ref[:, 0:GLA_QK_WIDTH]) * (GLA_DK ** -0.5)
    gla_ref[:, 0:GLA_QK_WIDTH] = q.astype(BF16)
    gla_ref[:, GLA_QK_WIDTH:_GLA_W] = _dot(hb, w_ref[:, GLA_QK_WIDTH:_GLA_W]).astype(BF16)
    o0 = _GLA_W
    lr = _dot(hb, w_ref[:, o0:o0 + _LR_W])
    lr_hi, lr_lo = _split_bf16(lr)
    a_logit = (_dot(lr_hi, wgk_hi_ref[...]) + _dot(lr_lo, wgk_hi_ref[...])
               + _dot(lr_hi, wgk_lo_ref[...])) + bgk_ref[...]
    log_sig = jnp.minimum(a_logit, 0.0) - jnp.log1p(jnp.exp(-jnp.abs(a_logit)))
    la_ref[...] = log_sig / GLA_GATE_NORMALIZER
    o0 += _LR_W + _MLA_W
    gate_ref[...] = _dot(hb, w_ref[:, o0:o0 + _GATE_W]).astype(BF16)


def _inproj(x2, nw, w_pack, wgk_hi, wgk_lo, bgk, pos, wuq, wuk, wvt, vone, qnw, kvnw, qw, kw, freq,
            batch, seq, tm):
    t = x2.shape[0]
    nblk = seq // tm
    const = lambda i: (0, 0)
    row = lambda i: (i, 0)
    hw = MLA_HEADS * MLA_HEAD_PAD
    return pl.pallas_call(
        _inproj_kernel,
        grid=(t // tm,),
        in_specs=[
            pl.BlockSpec((tm, D_MODEL), row),
            pl.BlockSpec((1, D_MODEL), const),
            pl.BlockSpec((D_MODEL, _PACK_W), const, pipeline_mode=pl.Buffered(1)),
            pl.BlockSpec((LANES, GLA_QK_WIDTH), const),
            pl.BlockSpec((LANES, GLA_QK_WIDTH), const),
            pl.BlockSpec((1, GLA_QK_WIDTH), const),
            pl.BlockSpec((1, tm), lambda i: (0, i)),
            pl.BlockSpec((MLA_Q_RANK, hw), const),
            pl.BlockSpec((MLA_KV_RANK, MLA_HEADS * MLA_NOPE), const),
            pl.BlockSpec((MLA_HEADS * FLASH_V_ROWS, MLA_KV_RANK), const),
            pl.BlockSpec((MLA_HEADS * FLASH_V_ROWS, 1), const),
            pl.BlockSpec((1, MLA_Q_RANK), const),
            pl.BlockSpec((1, MLA_KV_RANK), const),
            pl.BlockSpec((1, MLA_HEAD_PAD), const),
            pl.BlockSpec((1, MLA_HEAD_PAD), const),
            pl.BlockSpec((MLA_ROPE // 2, 1), const),
        ],
        out_specs=[
            pl.BlockSpec((tm, _GLA_W), row),
            pl.BlockSpec((tm, GLA_QK_WIDTH), row),
            pl.BlockSpec((tm, _GATE_W), row),
            pl.BlockSpec((tm, hw), row),
            pl.BlockSpec((tm, hw), row),
            pl.BlockSpec((1, MLA_HEADS * FLASH_V_ROWS, tm), lambda i: (i // nblk, 0, i % nblk)),
        ],
        out_shape=[
            jax.ShapeDtypeStruct((t, _GLA_W), BF16),
            jax.ShapeDtypeStruct((t, GLA_QK_WIDTH), F32),
            jax.ShapeDtypeStruct((t, _GATE_W), BF16),
            jax.ShapeDtypeStruct((t, hw), BF16),
            jax.ShapeDtypeStruct((t, hw), BF16),
            jax.ShapeDtypeStruct((batch, MLA_HEADS * FLASH_V_ROWS, seq), BF16),
        ],
        compiler_params=_params("parallel"),
        name="inproj",
    )(x2, nw, w_pack, wgk_hi, wgk_lo, bgk, pos, wuq, wuk, wvt, vone, qnw, kvnw, qw, kw, freq)


def _gla_kernel(q_ref, k_ref, v_ref, g_ref, la_ref, nw_ref, o_ref, state_ref, *, n_sub):
    @pl.when(pl.program_id(1) == 0)
    def _():
        state_ref[...] = jnp.zeros_like(state_ref)

    c = GLA_CHUNK
    row = lax.broadcasted_iota(I32, (c, c), 0)
    col = lax.broadcasted_iota(I32, (c, c), 1)
    causal = row >= col
    tri = causal.astype(BF16)
    mid = c // 2 - 1
    for s in range(n_sub):
        r0 = s * c
        la = la_ref[r0:r0 + c, :]
        la_hi, la_lo = _split_bf16(la)
        cum_all = _dot(tri, la_hi) + _dot(tri, la_lo)
        for h in range(GLA_HEADS):
            ks = slice(h * GLA_DK, (h + 1) * GLA_DK)
            vs = slice(h * GLA_DV, (h + 1) * GLA_DV)
            cum = cum_all[:, ks]
            q = q_ref[r0:r0 + c, ks].astype(F32)
            k = k_ref[r0:r0 + c, ks].astype(F32)
            v = v_ref[r0:r0 + c, vs]
            ref_row = cum[mid:mid + 1, :]
            last = cum[c - 1:c, :]
            qg = (q * jnp.exp(cum - ref_row)).astype(BF16)
            kg = (k * jnp.exp(ref_row - cum)).astype(BF16)
            scores = jnp.where(causal, _dot_nt(qg, kg), 0.0).astype(BF16)
            o = _dot(scores, v)
            st = state_ref[h]
            qe = (q * jnp.exp(cum)).astype(BF16)
            o = o + _dot_nt(qe, st.astype(BF16))
            ko = (k * jnp.exp(last - cum)).astype(BF16)
            state_ref[h] = st * jnp.exp(last) + _dot_tn(v, ko)
            ms = jnp.mean(o * o, axis=-1, keepdims=True)
            on = (o * lax.rsqrt(ms + EPS)) * nw_ref[...]
            g = g_ref[r0:r0 + c, vs].astype(F32)
            o_ref[r0:r0 + c, vs] = (on * (g * jax.nn.sigmoid(g))).astype(BF16)


def _gla(gla, la, nw, batch, seq, ts):
    t = gla.shape[0]
    nblk = seq // ts
    kern = functools.partial(_gla_kernel, n_sub=ts // GLA_CHUNK)
    rowmap = lambda col: (lambda b, i: (b * nblk + i, col))
    return pl.pallas_call(
        kern,
        grid=(batch, nblk),
        in_specs=[
            pl.BlockSpec((ts, GLA_QK_WIDTH), rowmap(0)),
            pl.BlockSpec((ts, GLA_QK_WIDTH), rowmap(1)),
            pl.BlockSpec((ts, GLA_V_WIDTH), rowmap(1)),
            pl.BlockSpec((ts, GLA_V_WIDTH), rowmap(2)),
            pl.BlockSpec((ts, GLA_QK_WIDTH), rowmap(0)),
            pl.BlockSpec((1, GLA_DV), lambda b, i: (0, 0)),
        ],
        out_specs=pl.BlockSpec((ts, GLA_V_WIDTH), rowmap(0)),
        out_shape=jax.ShapeDtypeStruct((t, GLA_V_WIDTH), BF16),
        scratch_shapes=[pltpu.VMEM((GLA_HEADS, GLA_DV, GLA_DK), F32)],
        compiler_params=_params("parallel", "arbitrary"),
        name="gla",
    )(gla, gla, gla, gla, la, nw)


def _rope(x, cos, sin, lane):
    half = MLA_ROPE // 2
    rot = jnp.where(lane < half, -pltpu.roll(x, LANES - half, axis=1), pltpu.roll(x, half, axis=1))
    return x * cos + rot * sin


def _mla_prep(mla, pos_ref, wuq_ref, wuk_ref, wvt_ref, vone_ref, qnw_ref, kvnw_ref,
              qw_ref, kw_ref, freq_ref, q_ref, k_ref, vt_ref):
    tm = mla.shape[0]
    cq = mla[:, 0:MLA_Q_RANK]
    ckv = mla[:, MLA_Q_RANK:MLA_Q_RANK + MLA_KV_RANK]
    kr = mla[:, MLA_Q_RANK + MLA_KV_RANK:_MLA_W]

    def rms(a, w):
        ms = jnp.mean(a * a, axis=-1, keepdims=True)
        return (a * lax.rsqrt(ms + EPS)) * w

    q_all = _dot(rms(cq, qnw_ref[...]).astype(BF16), wuq_ref[...])
    ckvn = rms(ckv, kvnw_ref[...]).astype(BF16)
    k_all = _dot(ckvn, wuk_ref[...])
    vt_ref[0] = (_dot_nt(wvt_ref[...], ckvn) + vone_ref[:, 0:1]).astype(BF16)

    half = MLA_ROPE // 2
    ang_t = freq_ref[...] * pos_ref[...]
    cos_t = jnp.cos(ang_t)
    sin_t = jnp.sin(ang_t)
    cos = jnp.concatenate([cos_t, cos_t, jnp.ones((LANES - 2 * half, tm), F32)], axis=0).T
    sin = jnp.concatenate([sin_t, sin_t, jnp.zeros((LANES - 2 * half, tm), F32)], axis=0).T
    lane = lax.broadcasted_iota(I32, (tm, LANES), 1)
    qw_nope = qw_ref[:, 0:MLA_NOPE]
    qw_rope = qw_ref[:, MLA_NOPE:MLA_HEAD_PAD]
    kw_nope = kw_ref[:, 0:MLA_NOPE]
    kw_rope = kw_ref[:, MLA_NOPE:MLA_HEAD_PAD]
    kr_sq = kr * kr
    kr_rot = _rope(kr * kw_rope, cos, sin, lane)
    scale = (MLA_QK ** -0.5) * LOG2_E
    for h in range(MLA_HEADS):
        base = h * MLA_HEAD_PAD
        qn = q_all[:, base:base + MLA_NOPE]
        qr = q_all[:, base + MLA_NOPE:base + MLA_HEAD_PAD]
        ss = jnp.sum(qn * qn + qr * qr, axis=-1, keepdims=True)
        r = lax.rsqrt(ss / MLA_QK + EPS)
        q_ref[:, base:base + MLA_NOPE] = ((qn * r) * qw_nope * scale).astype(BF16)
        q_ref[:, base + MLA_NOPE:base + MLA_HEAD_PAD] = (
            _rope((qr * r) * qw_rope, cos, sin, lane) * scale).astype(BF16)
        kn = k_all[:, h * MLA_NOPE:(h + 1) * MLA_NOPE]
        ssk = jnp.sum(kn * kn + kr_sq, axis=-1, keepdims=True)
        rk = lax.rsqrt(ssk / MLA_QK + EPS)
        k_ref[:, base:base + MLA_NOPE] = ((kn * rk) * kw_nope).astype(BF16)
        k_ref[:, base + MLA_NOPE:base + MLA_HEAD_PAD] = (kr_rot * rk).astype(BF16)


def _flash_kernel(q_ref, k_ref, vt_ref, o_ref, s_ref, cm_ref, acc_ref, m_ref, *, t):
    seq = q_ref.shape[0]
    nq = seq // t
    krow = lax.broadcasted_iota(I32, (t, t), 0)
    qcol = lax.broadcasted_iota(I32, (t, t), 1)
    items = [(qi, j) for qi in range(nq) for j in range(qi + 1)]

    def scores(idx):
        qi, j = items[idx]
        s = _dot_nt(k_ref[j * t:(j + 1) * t, :], q_ref[qi * t:(qi + 1) * t, :])
        s_ref[idx % 2] = s
        cm_ref[idx % 2] = jnp.max(s, axis=0, keepdims=True)

    def update(idx):
        qi, j = items[idx]
        if j == 0:
            m_ref[...] = jnp.full_like(m_ref, -jnp.inf)
            acc_ref[...] = jnp.zeros_like(acc_ref)
        s = s_ref[idx % 2]
        if j == qi:
            s = jnp.where(krow <= qcol, s, -jnp.inf)
            cm = jnp.max(s, axis=0, keepdims=True)
        else:
            cm = cm_ref[idx % 2]
        m_prev = m_ref[...]
        m_new = jnp.maximum(m_prev, cm)
        alpha = jnp.exp2(m_prev - m_new)
        p = jnp.exp2(s - m_new).astype(BF16)
        m_ref[...] = m_new
        acc_ref[...] = alpha * acc_ref[...] + _dot(vt_ref[0, :, j * t:(j + 1) * t], p)
        if j == qi:
            acc = acc_ref[...]
            out_t = acc[0:MLA_V, :] * (1.0 / acc[MLA_V:MLA_V + 1, :])
            o_ref[qi * t:(qi + 1) * t, :] = out_t.T.astype(BF16)

    scores(0)
    for idx in range(len(items)):
        if idx + 1 < len(items):
            scores(idx + 1)
        update(idx)


def _flash(q, k, vt, batch, seq, t):
    rows = q.shape[0]
    return pl.pallas_call(
        functools.partial(_flash_kernel, t=t),
        grid=(batch, MLA_HEADS),
        in_specs=[
            pl.BlockSpec((seq, MLA_HEAD_PAD), lambda b, h: (b, h)),
            pl.BlockSpec((seq, MLA_HEAD_PAD), lambda b, h: (b, h)),
            pl.BlockSpec((1, FLASH_V_ROWS, seq), lambda b, h: (b, h, 0)),
        ],
        out_specs=pl.BlockSpec((seq, MLA_V), lambda b, h: (b, h)),
        out_shape=jax.ShapeDtypeStruct((rows, MLA_HEADS * MLA_V), BF16),
        scratch_shapes=[
            pltpu.VMEM((2, t, t), F32),
            pltpu.VMEM((2, 1, t), F32),
            pltpu.VMEM((FLASH_V_ROWS, t), F32),
            pltpu.VMEM((1, t), F32),
        ],
        compiler_params=_params("parallel", "parallel"),
        name="flash",
    )(q, k, vt)


def _merge_route_kernel(x_ref, oa_ref, ob_ref, gate_ref, wa_ref, wb_ref, wo_ref, nw_ref,
                        wr_hi_ref, wr_lo_ref, br_ref,
                        x1_ref, xn_ref, eid_ref, cw_ref, cnt_ref):
    @pl.when(pl.program_id(0) == 0)
    def _():
        cnt_ref[...] = jnp.zeros_like(cnt_ref)

    tm = x_ref.shape[0]
    ya = _dot(oa_ref[...], wa_ref[...])
    yb = _dot(ob_ref[...], wb_ref[...])
    ga = gate_ref[:, 0:D_MODEL].astype(F32)
    gb = gate_ref[:, D_MODEL:2 * D_MODEL].astype(F32)
    merged = jax.nn.sigmoid(ga) * ya + jax.nn.sigmoid(gb) * yb
    x1 = x_ref[...] + _dot(merged.astype(BF16), wo_ref[...])
    x1_ref[...] = x1
    ms = jnp.mean(x1 * x1, axis=-1, keepdims=True)
    xn = (x1 * lax.rsqrt(ms + EPS)) * nw_ref[...]
    xn_ref[...] = xn
    xn_hi, xn_lo = _split_bf16(xn)
    logits = (_dot(xn_hi, wr_hi_ref[...]) + _dot(xn_lo, wr_hi_ref[...])
              + _dot(xn_hi, wr_lo_ref[...])) + br_ref[...]
    lane = lax.broadcasted_iota(I32, (tm, LANES), 1)
    work = jnp.where(lane < N_EXPERTS, logits, -jnp.inf)
    vals, idxs = [], []
    for _ in range(TOP_K):
        m = jnp.max(work, axis=-1, keepdims=True)
        idx = jnp.min(jnp.where(work == m, lane, LANES), axis=-1, keepdims=True)
        vals.append(m)
        idxs.append(idx)
        work = jnp.where(lane == idx, -jnp.inf, work)
    exps = [jnp.exp(v - vals[0]) for v in vals]
    denom = exps[0] + exps[1] + exps[2] + exps[3]
    eid = jnp.zeros((tm, LANES), I32)
    cw = jnp.zeros((tm, LANES), F32)
    sel = jnp.zeros((tm, LANES), F32)
    for kk in range(TOP_K):
        eid = jnp.where(lane == kk, idxs[kk], eid)
        cw = jnp.where(lane == kk, exps[kk] / denom, cw)
        sel = sel + (lane == idxs[kk]).astype(F32)
    eid_ref[...] = eid
    cw_ref[...] = cw
    cnt_ref[0:1, :] = cnt_ref[0:1, :] + jnp.sum(sel, axis=0, keepdims=True)


def _merge_route(x2, oa, ob, gates, wa, wb, wo, nw, wr_hi, wr_lo, br, tm):
    t = x2.shape[0]
    const = lambda i: (0, 0)
    row = lambda i: (i, 0)
    return pl.pallas_call(
        _merge_route_kernel,
        grid=(t // tm,),
        in_specs=[
            pl.BlockSpec((tm, D_MODEL), row),
            pl.BlockSpec((tm, GLA_V_WIDTH), row),
            pl.BlockSpec((tm, MLA_HEADS * MLA_V), row),
            pl.BlockSpec((tm, _GATE_W), row),
            pl.BlockSpec((GLA_V_WIDTH, D_MODEL), const),
            pl.BlockSpec((MLA_HEADS * MLA_V, D_MODEL), const),
            pl.BlockSpec((D_MODEL, D_MODEL), const),
            pl.BlockSpec((1, D_MODEL), const),
            pl.BlockSpec((D_MODEL, LANES), const),
            pl.BlockSpec((D_MODEL, LANES), const),
            pl.BlockSpec((1, LANES), const),
        ],
        out_specs=[
            pl.BlockSpec((tm, D_MODEL), row),
            pl.BlockSpec((tm, D_MODEL), row),
            pl.BlockSpec((tm, LANES), row),
            pl.BlockSpec((tm, LANES), row),
            pl.BlockSpec((8, LANES), const),
        ],
        out_shape=[
            jax.ShapeDtypeStruct((t, D_MODEL), F32),
            jax.ShapeDtypeStruct((t, D_MODEL), F32),
            jax.ShapeDtypeStruct((t, LANES), I32),
            jax.ShapeDtypeStruct((t, LANES), F32),
            jax.ShapeDtypeStruct((8, LANES), F32),
        ],
        compiler_params=_params("arbitrary"),
        name="merge_route",
    )(x2, oa, ob, gates, wa, wb, wo, nw, wr_hi, wr_lo, br)


def _positions_kernel(eid_ref, cnt_ref, pos_ref, carry_ref, *, tile):
    @pl.when(pl.program_id(0) == 0)
    def _():
        carry_ref[...] = jnp.zeros_like(carry_ref)

    tb = eid_ref.shape[0]
    lane1 = lax.broadcasted_iota(I32, (1, LANES), 1)
    cnt = cnt_ref[0:1, :]
    padded = jnp.floor((cnt + (tile - 1)) / tile) * tile
    incl = padded
    shift = 1
    while shift < N_EXPERTS:
        incl = incl + jnp.where(lane1 >= shift, pltpu.roll(incl, shift, axis=1), 0.0)
        shift *= 2
    offs = incl - padded

    lane = lax.broadcasted_iota(I32, (tb, LANES), 1)
    eid = eid_ref[...]
    onehots = [lane == jnp.broadcast_to(eid[:, kk:kk + 1], (tb, LANES)) for kk in range(TOP_K)]
    sel = jnp.zeros((tb, LANES), F32)
    for oh in onehots:
        sel = sel + oh.astype(F32)
    row = lax.broadcasted_iota(I32, (tb, tb), 0)
    col = lax.broadcasted_iota(I32, (tb, tb), 1)
    strict = (row > col).astype(BF16)
    rank = _dot(strict, sel.astype(BF16)) + carry_ref[...] + offs
    pos = jnp.zeros((tb, LANES), I32)
    for kk in range(TOP_K):
        pk = jnp.sum(jnp.where(onehots[kk], rank, 0.0), axis=-1, keepdims=True)
        pos = jnp.where(lane == kk, pk.astype(I32), pos)
    pos_ref[...] = pos
    carry_ref[...] = carry_ref[...] + jnp.sum(sel, axis=0, keepdims=True)


def _positions(eid, cnt, tb, tile):
    t = eid.shape[0]
    return pl.pallas_call(
        functools.partial(_positions_kernel, tile=tile),
        grid=(t // tb,),
        in_specs=[
            pl.BlockSpec((tb, LANES), lambda i: (i, 0)),
            pl.BlockSpec((8, LANES), lambda i: (0, 0)),
        ],
        out_specs=pl.BlockSpec((tb, LANES), lambda i: (i, 0)),
        out_shape=jax.ShapeDtypeStruct((t, LANES), I32),
        scratch_shapes=[pltpu.VMEM((1, LANES), F32)],
        compiler_params=_params("arbitrary"),
        name="positions",
    )(eid, cnt)


ROW_DMA_UNROLL = 8


def _dispatch_kernel(tz_ref, pos_ref, xn_ref, xs_ref, zero_ref, sem, zsem, *, tile):
    i = pl.program_id(0)
    tb = xn_ref.shape[0]

    def zero_copy(e):
        start = pl.multiple_of(tz_ref[e], tile)
        return pltpu.make_async_copy(zero_ref, xs_ref.at[pl.ds(start, tile), 0], zsem)

    @pl.when(i == 0)
    def _():
        zero_ref[...] = jnp.zeros_like(zero_ref)
        for e in range(N_EXPERTS):
            @pl.when(tz_ref[e] >= 0)
            def _():
                zero_copy(e).start()
        for e in range(N_EXPERTS):
            @pl.when(tz_ref[e] >= 0)
            def _():
                zero_copy(e).wait()

    def issue(t, carry):
        for kk in range(TOP_K):
            p = pos_ref[0, 0, t * TOP_K + kk]
            pltpu.make_async_copy(xn_ref.at[pl.ds(t, 1)], xs_ref.at[p], sem).start(priority=kk % 2)
        return carry

    lax.fori_loop(0, tb, issue, 0, unroll=ROW_DMA_UNROLL)
    for _ in range(TOP_K):
        pltpu.make_async_copy(xn_ref, xs_ref.at[pl.ds(0, tb), 0], sem).wait()


def _dispatch(tile_zero, pos3, xn, n_rows, tb, tile):
    t = xn.shape[0]
    return pl.pallas_call(
        functools.partial(_dispatch_kernel, tile=tile),
        grid_spec=pltpu.PrefetchScalarGridSpec(
            num_scalar_prefetch=1,
            grid=(t // tb,),
            in_specs=[
                pl.BlockSpec((1, 1, tb * TOP_K), lambda i, tz: (i, 0, 0), memory_space=pltpu.SMEM),
                pl.BlockSpec((tb, D_MODEL), lambda i, tz: (i, 0)),
            ],
            out_specs=pl.BlockSpec(memory_space=pl.ANY),
            scratch_shapes=[pltpu.VMEM((tile, D_MODEL), F32), pltpu.SemaphoreType.DMA,
                            pltpu.SemaphoreType.DMA],
        ),
        out_shape=jax.ShapeDtypeStruct((n_rows, 1, D_MODEL), F32),
        compiler_params=_params("arbitrary"),
        name="dispatch",
    )(tile_zero, pos3, xn)


def _experts_kernel(te_ref, nv_ref, xs_ref, xsc_ref, wg_ref, bg_ref, wu_ref, bu_ref, wd_ref, bd_ref,
                    ys_ref,
                    wgb_ref, wub_ref, wdb_ref, wsc_ref, xbuf, ybuf, isem, osem, *, tile):
    i = pl.program_id(0)
    nv = nv_ref[0]
    valid = i < nv
    slot = i % 2

    def in_copies(step, s):
        r0 = pl.multiple_of(step * tile, tile)
        return [pltpu.make_async_copy(xs_ref.at[pl.ds(r0, tile), 0, pl.ds(c * LANES, LANES)],
                                      xbuf.at[s, :, pl.ds(c * LANES, LANES)], isem.at[s])
                for c in range(LANE_CHUNKS)]

    def out_copies(step, s):
        r0 = pl.multiple_of(step * tile, tile)
        return [pltpu.make_async_copy(ybuf.at[s, :, pl.ds(c * LANES, LANES)],
                                      ys_ref.at[pl.ds(r0, tile), 0, pl.ds(c * LANES, LANES)],
                                      osem.at[s])
                for c in range(LANE_CHUNKS)]

    @pl.when(i == 0)
    def _():
        for cp in in_copies(0, 0):
            cp.start()

    @pl.when(i + 1 < nv)
    def _():
        for cp in in_copies(i + 1, 1 - slot):
            cp.start()

    new_expert = jnp.logical_or(i == 0, te_ref[i] != te_ref[jnp.maximum(i - 1, 0)])

    @pl.when(jnp.logical_and(valid, new_expert))
    def _():
        for slot_w, (w_ref, w8_ref) in enumerate(((wg_ref, wgb_ref), (wu_ref, wub_ref),
                                                  (wd_ref, wdb_ref))):
            w = w_ref[0]
            sc = _pow2_scale(_abs_max(w))
            w8_ref[...] = (w * sc).astype(F8)
            wsc_ref[slot_w:slot_w + 1, :] = jnp.broadcast_to(1.0 / sc, (1, LANES))

    @pl.when(valid)
    def _():
        for cp in in_copies(i, slot):
            cp.wait()

        @pl.when(i >= 2)
        def _():
            for cp in out_copies(i - 2, slot):
                cp.wait()

        x = xbuf[slot]
        sx = xsc_ref[0:1, 0:1]
        x8 = (x * sx).astype(F8)
        inv_x = 1.0 / sx
        g = _dot(x8, wgb_ref[...]) * (inv_x * wsc_ref[0:1, 0:1]) + bg_ref[0]
        u = _dot(x8, wub_ref[...]) * (inv_x * wsc_ref[1:2, 0:1]) + bu_ref[0]
        g = jnp.minimum(g, SWIGLU_LIMIT)
        u = jnp.clip(u, -SWIGLU_LIMIT, SWIGLU_LIMIT)
        hidden = (u + 1.0) * g * jax.nn.sigmoid(SWIGLU_ALPHA * g)
        h8 = (hidden * HIDDEN_SCALE).astype(F8)
        ybuf[slot] = (_dot(h8, wdb_ref[...]) * (wsc_ref[2:3, 0:1] * (1.0 / HIDDEN_SCALE))
                      + bd_ref[0])
        for cp in out_copies(i, slot):
            cp.start()

    @pl.when(i == nv - 1)
    def _():
        for cp in out_copies(i, slot):
            cp.wait()

        @pl.when(i >= 1)
        def _():
            for cp in out_copies(i - 1, 1 - slot):
                cp.wait()


def _experts(tile_expert, n_valid, xs, x_scale, wg, bg, wu, bu, wd, bd, tile):
    rows = xs.shape[0]
    nt = rows // tile
    wmap = lambda i, te, nv: (te[i], 0, 0)
    return pl.pallas_call(
        functools.partial(_experts_kernel, tile=tile),
        grid_spec=pltpu.PrefetchScalarGridSpec(
            num_scalar_prefetch=2,
            grid=(nt,),
            in_specs=[
                pl.BlockSpec(memory_space=pl.ANY),
                pl.BlockSpec((1, LANES), lambda i, te, nv: (0, 0)),
                pl.BlockSpec((1, D_MODEL, D_FF), wmap),
                pl.BlockSpec((1, 1, D_FF), wmap),
                pl.BlockSpec((1, D_MODEL, D_FF), wmap),
                pl.BlockSpec((1, 1, D_FF), wmap),
                pl.BlockSpec((1, D_FF, D_MODEL), wmap),
                pl.BlockSpec((1, 1, D_MODEL), wmap),
            ],
            out_specs=pl.BlockSpec(memory_space=pl.ANY),
            scratch_shapes=[pltpu.VMEM((D_MODEL, D_FF), F8), pltpu.VMEM((D_MODEL, D_FF), F8),
                            pltpu.VMEM((D_FF, D_MODEL), F8), pltpu.VMEM((8, LANES), F32),
                            pltpu.VMEM((2, tile, D_MODEL), F32), pltpu.VMEM((2, tile, D_MODEL), F32),
                            pltpu.SemaphoreType.DMA((2,)), pltpu.SemaphoreType.DMA((2,))],
        ),
        out_shape=jax.ShapeDtypeStruct((rows, 1, D_MODEL), F32),
        compiler_params=_params("arbitrary"),
        name="experts",
    )(tile_expert, n_valid, xs, x_scale, wg, bg, wu, bu, wd, bd)


def _combine_kernel(pos_ref, posn_ref, x1_ref, cw_ref, ys_ref, o_ref, buf_ref, sem0, sem1):
    i = pl.program_id(0)
    n = pl.num_programs(0)
    tb = x1_ref.shape[0] // 2
    def issue(p_ref, sub, slot, sem):
        def body(t, carry):
            for kk in range(TOP_K):
                p = p_ref[0, 0, (sub * tb + t) * TOP_K + kk]
                pltpu.make_async_copy(ys_ref.at[p], buf_ref.at[slot, kk, pl.ds(t, 1)],
                                      sem).start(priority=kk % 2)
            return carry
        lax.fori_loop(0, tb, body, 0, unroll=ROW_DMA_UNROLL)

    def drain(slot, sem):
        for kk in range(TOP_K):
            pltpu.make_async_copy(ys_ref.at[pl.ds(0, tb), 0], buf_ref.at[slot, kk], sem).wait()

    def reduce(sub, slot):
        rows = slice(sub * tb, (sub + 1) * tb)
        cw = cw_ref[rows, :]
        acc = x1_ref[rows, :]
        for kk in range(TOP_K):
            acc = acc + cw[:, kk:kk + 1] * buf_ref[slot, kk]
        o_ref[rows, :] = acc

    @pl.when(i == 0)
    def _():
        issue(pos_ref, 0, 0, sem0)

    issue(pos_ref, 1, 1, sem1)
    drain(0, sem0)
    reduce(0, 0)

    @pl.when(i + 1 < n)
    def _():
        issue(posn_ref, 0, 0, sem0)

    drain(1, sem1)
    reduce(1, 1)


def _combine(pos3, x1, cw, ys, tb):
    t = x1.shape[0]
    n = t // (2 * tb)
    return pl.pallas_call(
        _combine_kernel,
        grid=(n,),
        in_specs=[
            pl.BlockSpec((1, 1, 2 * tb * TOP_K), lambda i: (i, 0, 0), memory_space=pltpu.SMEM),
            pl.BlockSpec((1, 1, 2 * tb * TOP_K), lambda i: (jnp.minimum(i + 1, n - 1), 0, 0),
                         memory_space=pltpu.SMEM),
            pl.BlockSpec((2 * tb, D_MODEL), lambda i: (i, 0)),
            pl.BlockSpec((2 * tb, LANES), lambda i: (i, 0)),
            pl.BlockSpec(memory_space=pl.ANY),
        ],
        out_specs=pl.BlockSpec((2 * tb, D_MODEL), lambda i: (i, 0)),
        out_shape=jax.ShapeDtypeStruct((t, D_MODEL), F32),
        scratch_shapes=[pltpu.VMEM((2, TOP_K, tb, D_MODEL), F32), pltpu.SemaphoreType.DMA,
                        pltpu.SemaphoreType.DMA],
        compiler_params=_params("arbitrary"),
        name="combine",
    )(pos3, pos3, x1, cw, ys)


def _pad_cols(a, width):
    return jnp.pad(a, ((0, 0), (0, width - a.shape[1])))


def _pack_in_proj(w_in):
    o = np.cumsum((GLA_QK_WIDTH, GLA_QK_WIDTH, GLA_V_WIDTH, GLA_V_WIDTH, GLA_GATE_RANK,
                   MLA_Q_RANK, MLA_KV_RANK, MLA_ROPE, D_MODEL, D_MODEL)).tolist()
    gla = w_in[:, 0:o[3]]
    lr = _pad_cols(w_in[:, o[3]:o[4]], _LR_W)
    mla = _pad_cols(w_in[:, o[4]:o[7]], _MLA_W)
    gates = w_in[:, o[7]:o[9]]
    return jnp.concatenate([gla, lr, mla, gates], axis=1).astype(BF16)


def _pad_heads(w, head_w):
    r = w.shape[0]
    w3 = w.reshape(r, MLA_HEADS, head_w)
    w3 = jnp.pad(w3, ((0, 0), (0, 0), (0, MLA_HEAD_PAD - head_w)))
    return w3.reshape(r, MLA_HEADS * MLA_HEAD_PAD)


def _tile_rows(n, pref):
    return pref if n % pref == 0 else n


def kernel(x, positions, attn_norm_w, w_in, w_gla_gk, b_gla_gk, gla_out_norm_w, w_gla_out,
           mla_q_norm_w, w_mla_uq, mla_kv_norm_w, w_mla_ukv, mla_qk_q_norm_w, mla_qk_k_norm_w,
           w_mla_out, w_out, moe_norm_w, w_router, b_router, w_exp_gate, b_exp_gate,
           w_exp_up, b_exp_up, w_exp_down, b_exp_down):
    batch, seq, _ = x.shape
    depth = w_in.shape[0]
    t = batch * seq
    x2 = x.reshape(t, D_MODEL)
    pos = positions.reshape(1, t).astype(F32)
    half = MLA_ROPE // 2
    freq = (ROPE_BASE ** (-jnp.arange(half, dtype=F32) / half))[:, None]

    tm = _tile_rows(t, TOKEN_TILE)
    seq_tile = _tile_rows(seq, TOKEN_TILE)
    gla_ts = _tile_rows(seq, GLA_BLOCK)
    moe_tile = TOKEN_TILE
    n_rows = ((t * TOP_K) // moe_tile + N_EXPERTS) * moe_tile

    for l in range(depth):
        w_pack = _pack_in_proj(w_in[l])
        wgk = jnp.pad(w_gla_gk[l], ((0, LANES - GLA_GATE_RANK), (0, 0)))
        wgk_hi, wgk_lo = _split_bf16(wgk)
        wuq = _pad_heads(w_mla_uq[l], MLA_QK).astype(BF16)
        wukv3 = w_mla_ukv[l].reshape(MLA_KV_RANK, MLA_HEADS, MLA_NOPE + MLA_V)
        wuk = wukv3[:, :, :MLA_NOPE].reshape(MLA_KV_RANK, MLA_HEADS * MLA_NOPE).astype(BF16)
        wvt = jnp.pad(wukv3[:, :, MLA_NOPE:].transpose(1, 2, 0),
                      ((0, 0), (0, FLASH_V_ROWS - MLA_V), (0, 0)))
        wvt = wvt.reshape(MLA_HEADS * FLASH_V_ROWS, MLA_KV_RANK).astype(BF16)
        vone = (jnp.arange(MLA_HEADS * FLASH_V_ROWS) % FLASH_V_ROWS == MLA_V).astype(F32)[:, None]
        qw = _pad_cols(mla_qk_q_norm_w[l][None, :], MLA_HEAD_PAD)
        kw = _pad_cols(mla_qk_k_norm_w[l][None, :], MLA_HEAD_PAD)
        gla, la, gates, q, k, vt = _inproj(
            x2, attn_norm_w[l][None, :], w_pack, wgk_hi, wgk_lo, b_gla_gk[l][None, :], pos,
            wuq, wuk, wvt, vone, mla_q_norm_w[l][None, :], mla_kv_norm_w[l][None, :], qw, kw, freq,
            batch, seq, seq_tile)

        o_a = _gla(gla, la, gla_out_norm_w[l][None, :], batch, seq, gla_ts)
        o_b = _flash(q, k, vt, batch, seq, seq_tile)

        wr = _pad_cols(w_router[l], LANES)
        wr_hi, wr_lo = _split_bf16(wr)
        br = _pad_cols(b_router[l][None, :], LANES)
        x1, xn, eid, cw, cnt = _merge_route(
            x2, o_a, o_b, gates, w_gla_out[l].astype(BF16), w_mla_out[l].astype(BF16),
            w_out[l].astype(BF16), moe_norm_w[l][None, :], wr_hi, wr_lo, br, tm)

        pos3 = _positions(eid, cnt, tm, moe_tile)[:, :TOP_K].reshape(t // tm, 1, tm * TOP_K)
        counts = cnt[0, :N_EXPERTS].astype(I32)
        padded = ((counts + moe_tile - 1) // moe_tile) * moe_tile
        ends = jnp.cumsum(padded)
        n_tiles = n_rows // moe_tile
        n_valid = (ends[-1] // moe_tile).astype(I32)
        starts = jnp.arange(n_tiles, dtype=I32) * moe_tile
        starts = jnp.minimum(starts, ends[-1] - moe_tile)
        tile_expert = jnp.minimum(jnp.sum((starts[:, None] >= ends[None, :]).astype(I32), axis=1),
                                  N_EXPERTS - 1)
        tile_zero = jnp.where(padded > 0, ends - moe_tile, -1).astype(I32)

        xs = _dispatch(tile_zero, pos3, xn, n_rows, tm, moe_tile)
        x_bound = (D_MODEL ** 0.5) * jnp.max(jnp.abs(moe_norm_w[l]))
        x_scale = jnp.broadcast_to(_pow2_scale(x_bound), (1, LANES))
        ys = _experts(tile_expert, n_valid.reshape(1), xs, x_scale,
                      w_exp_gate[l], b_exp_gate[l][:, None, :],
                      w_exp_up[l], b_exp_up[l][:, None, :],
                      w_exp_down[l], b_exp_down[l][:, None, :], moe_tile)
        x2 = _combine(pos3, x1, cw, ys, tm // 2)
    return x2.reshape(batch, seq, D_MODEL)
```

```python
import functools

import jax
import jax.numpy as jnp
import numpy as np
from jax import lax
from jax.experimental import pallas as pl
from jax.experimental.pallas import tpu as pltpu

F32 = jnp.float32
BF16 = jnp.bfloat16
I32 = jnp.int32
F8 = jnp.float8_e4m3fn
F8_TARGET = 224.0
HIDDEN_SCALE = 4.0

D_MODEL = 1024
EPS = 1e-6
GLA_HEADS = 4
GLA_DK = 128
GLA_DV = 256
GLA_GATE_RANK = 16
GLA_GATE_NORMALIZER = 16.0
GLA_QK_WIDTH = GLA_HEADS * GLA_DK
GLA_V_WIDTH = GLA_HEADS * GLA_DV
MLA_HEADS = 8
MLA_Q_RANK = 384
MLA_KV_RANK = 256
MLA_NOPE = 128
MLA_ROPE = 64
MLA_V = 128
MLA_QK = MLA_NOPE + MLA_ROPE
ROPE_BASE = 10000.0
N_EXPERTS = 32
TOP_K = 4
D_FF = 1024
SWIGLU_LIMIT = 7.0
SWIGLU_ALPHA = 1.702

LANES = 128
MLA_HEAD_PAD = 256
GLA_CHUNK = 128
TOKEN_TILE = 512
FLASH_V_ROWS = 144
LOG2_E = 1.4426950408889634
VMEM_LIMIT = 56 * 1024 * 1024

_GLA_W = 2 * GLA_QK_WIDTH + 2 * GLA_V_WIDTH
_LR_W = LANES
_MLA_W = 768
_GATE_W = 2 * D_MODEL
_PACK_W = _GLA_W + _LR_W + _MLA_W + _GATE_W


def _dot(a, b):
    return jnp.dot(a, b, preferred_element_type=F32)


def _dot_nt(a, b):
    return lax.dot_general(a, b, (((1,), (1,)), ((), ())), preferred_element_type=F32)


def _dot_tn(a, b):
    return lax.dot_general(a, b, (((0,), (0,)), ((), ())), preferred_element_type=F32)


def _abs_max(a):
    return jnp.max(jnp.max(jnp.abs(a), axis=-1, keepdims=True), axis=0, keepdims=True)


def _pow2_scale(amax):
    return jnp.exp2(jnp.floor(jnp.log2(F8_TARGET / jnp.maximum(amax, 1e-30))))


def _split_bf16(a):
    hi = a.astype(BF16)
    lo = (a - hi.astype(F32)).astype(BF16)
    return hi, lo


def _params(*sem):
    return pltpu.CompilerParams(dimension_semantics=sem, vmem_limit_bytes=VMEM_LIMIT)


LANE_CHUNKS = D_MODEL // LANES


def _inproj_kernel(x_ref, nw_ref, w_ref, wgk_hi_ref, wgk_lo_ref, bgk_ref, pos_ref,
                   wuq_ref, wuk_ref, wvt_ref, vone_ref, qnw_ref, kvnw_ref, qw_ref, kw_ref, freq_ref,
                   gnw_ref, oa_ref, gate_ref, q_ref, k_ref, vt_ref, state_ref, *, steps_per_seq):
    @pl.when(pl.program_id(0) % steps_per_seq == 0)
    def _():
        state_ref[...] = jnp.zeros_like(state_ref)

    x = x_ref[...]
    ms = jnp.mean(x * x, axis=-1, keepdims=True)
    h = (x * lax.rsqrt(ms + EPS)) * nw_ref[...]
    hb = h.astype(BF16)
    mla = _dot(hb, w_ref[:, _GLA_W + _LR_W:_GLA_W + _LR_W + _MLA_W])
    _mla_prep(mla, pos_ref, wuq_ref, wuk_ref, wvt_ref, vone_ref, qnw_ref, kvnw_ref, qw_ref, kw_ref,
              freq_ref, q_ref, k_ref, vt_ref)
    o0 = 0
    gq = (_dot(hb, w_ref[:, 0:GLA_QK_WIDTH]) * (GLA_DK ** -0.5)).astype(BF16)
    gk = _dot(hb, w_ref[:, GLA_QK_WIDTH:2 * GLA_QK_WIDTH]).astype(BF16)
    gv = _dot(hb, w_ref[:, 2 * GLA_QK_WIDTH:2 * GLA_QK_WIDTH + GLA_V_WIDTH]).astype(BF16)
    gg = _dot(hb, w_ref[:, 2 * GLA_QK_WIDTH + GLA_V_WIDTH:_GLA_W])
    o0 = _GLA_W
    lr = _dot(hb, w_ref[:, o0:o0 + _LR_W])
    lr_hi, lr_lo = _split_bf16(lr)
    a_logit = (_dot(lr_hi, wgk_hi_ref[...]) + _dot(lr_lo, wgk_hi_ref[...])
               + _dot(lr_hi, wgk_lo_ref[...])) + bgk_ref[...]
    log_sig = jnp.minimum(a_logit, 0.0) - jnp.log1p(jnp.exp(-jnp.abs(a_logit)))
    _gla_chunks(gq, gk, gv, gg, log_sig / GLA_GATE_NORMALIZER, gnw_ref, oa_ref, state_ref)
    o0 += _LR_W + _MLA_W
    gate_ref[...] = _dot(hb, w_ref[:, o0:o0 + _GATE_W]).astype(BF16)


def _inproj(x2, nw, w_pack, wgk_hi, wgk_lo, bgk, pos, wuq, wuk, wvt, vone, qnw, kvnw, qw, kw, freq,
            gnw, batch, seq, tm):
    t = x2.shape[0]
    nblk = seq // tm
    const = lambda i: (0, 0)
    row = lambda i: (i, 0)
    hw = MLA_HEADS * MLA_HEAD_PAD
    return pl.pallas_call(
        functools.partial(_inproj_kernel, steps_per_seq=nblk),
        grid=(t // tm,),
        in_specs=[
            pl.BlockSpec((tm, D_MODEL), row),
            pl.BlockSpec((1, D_MODEL), const),
            pl.BlockSpec((D_MODEL, _PACK_W), const, pipeline_mode=pl.Buffered(1)),
            pl.BlockSpec((LANES, GLA_QK_WIDTH), const),
            pl.BlockSpec((LANES, GLA_QK_WIDTH), const),
            pl.BlockSpec((1, GLA_QK_WIDTH), const),
            pl.BlockSpec((1, tm), lambda i: (0, i)),
            pl.BlockSpec((MLA_Q_RANK, hw), const),
            pl.BlockSpec((MLA_KV_RANK, MLA_HEADS * MLA_NOPE), const),
            pl.BlockSpec((MLA_HEADS * FLASH_V_ROWS, MLA_KV_RANK), const),
            pl.BlockSpec((MLA_HEADS * FLASH_V_ROWS, 1), const),
            pl.BlockSpec((1, MLA_Q_RANK), const),
            pl.BlockSpec((1, MLA_KV_RANK), const),
            pl.BlockSpec((1, MLA_HEAD_PAD), const),
            pl.BlockSpec((1, MLA_HEAD_PAD), const),
            pl.BlockSpec((MLA_ROPE // 2, 1), const),
            pl.BlockSpec((1, GLA_DV), const),
        ],
        out_specs=[
            pl.BlockSpec((tm, GLA_V_WIDTH), row),
            pl.BlockSpec((tm, _GATE_W), row),
            pl.BlockSpec((tm, hw), row),
            pl.BlockSpec((tm, hw), row),
            pl.BlockSpec((1, MLA_HEADS * FLASH_V_ROWS, tm), lambda i: (i // nblk, 0, i % nblk)),
        ],
        out_shape=[
            jax.ShapeDtypeStruct((t, GLA_V_WIDTH), BF16),
            jax.ShapeDtypeStruct((t, _GATE_W), BF16),
            jax.ShapeDtypeStruct((t, hw), BF16),
            jax.ShapeDtypeStruct((t, hw), BF16),
            jax.ShapeDtypeStruct((batch, MLA_HEADS * FLASH_V_ROWS, seq), BF16),
        ],
        scratch_shapes=[pltpu.VMEM((GLA_HEADS, GLA_DV, GLA_DK), F32)],
        compiler_params=_params("arbitrary"),
        name="inproj",
    )(x2, nw, w_pack, wgk_hi, wgk_lo, bgk, pos, wuq, wuk, wvt, vone, qnw, kvnw, qw, kw, freq, gnw)


def _gla_chunks(q_all, k_all, v_all, g_all, la_all, nw_ref, o_ref, state_ref):
    c = GLA_CHUNK
    row = lax.broadcasted_iota(I32, (c, c), 0)
    col = lax.broadcasted_iota(I32, (c, c), 1)
    causal = row >= col
    tri = causal.astype(BF16)
    mid = c // 2 - 1
    for s in range(q_all.shape[0] // c):
        r0 = s * c
        la = la_all[r0:r0 + c, :]
        la_hi, la_lo = _split_bf16(la)
        cum_all = _dot(tri, la_hi) + _dot(tri, la_lo)
        for h in range(GLA_HEADS):
            ks = slice(h * GLA_DK, (h + 1) * GLA_DK)
            vs = slice(h * GLA_DV, (h + 1) * GLA_DV)
            cum = cum_all[:, ks]
            q = q_all[r0:r0 + c, ks].astype(F32)
            k = k_all[r0:r0 + c, ks].astype(F32)
            v = v_all[r0:r0 + c, vs]
            ref_row = cum[mid:mid + 1, :]
            last = cum[c - 1:c, :]
            qg = (q * jnp.exp(cum - ref_row)).astype(BF16)
            kg = (k * jnp.exp(ref_row - cum)).astype(BF16)
            scores = jnp.where(causal, _dot_nt(qg, kg), 0.0).astype(BF16)
            o = _dot(scores, v)
            st = state_ref[h]
            qe = (q * jnp.exp(cum)).astype(BF16)
            o = o + _dot_nt(qe, st.astype(BF16))
            ko = (k * jnp.exp(last - cum)).astype(BF16)
            state_ref[h] = st * jnp.exp(last) + _dot_tn(v, ko)
            ms = jnp.mean(o * o, axis=-1, keepdims=True)
            on = (o * lax.rsqrt(ms + EPS)) * nw_ref[...]
            g = g_all[r0:r0 + c, vs]
            o_ref[r0:r0 + c, vs] = (on * (g * jax.nn.sigmoid(g))).astype(BF16)


def _rope(x, cos, sin, lane):
    half = MLA_ROPE // 2
    rot = jnp.where(lane < half, -pltpu.roll(x, LANES - half, axis=1), pltpu.roll(x, half, axis=1))
    return x * cos + rot * sin


def _mla_prep(mla, pos_ref, wuq_ref, wuk_ref, wvt_ref, vone_ref, qnw_ref, kvnw_ref,
              qw_ref, kw_ref, freq_ref, q_ref, k_ref, vt_ref):
    tm = mla.shape[0]
    cq = mla[:, 0:MLA_Q_RANK]
    ckv = mla[:, MLA_Q_RANK:MLA_Q_RANK + MLA_KV_RANK]
    kr = mla[:, MLA_Q_RANK + MLA_KV_RANK:_MLA_W]

    def rms(a, w):
        ms = jnp.mean(a * a, axis=-1, keepdims=True)
        return (a * lax.rsqrt(ms + EPS)) * w

    q_all = _dot(rms(cq, qnw_ref[...]).astype(BF16), wuq_ref[...])
    ckvn = rms(ckv, kvnw_ref[...]).astype(BF16)
    k_all = _dot(ckvn, wuk_ref[...])
    vt_ref[0] = (_dot_nt(wvt_ref[...], ckvn) + vone_ref[:, 0:1]).astype(BF16)

    half = MLA_ROPE // 2
    ang_t = freq_ref[...] * pos_ref[...]
    cos_t = jnp.cos(ang_t)
    sin_t = jnp.sin(ang_t)
    cos = jnp.concatenate([cos_t, cos_t, jnp.ones((LANES - 2 * half, tm), F32)], axis=0).T
    sin = jnp.concatenate([sin_t, sin_t, jnp.zeros((LANES - 2 * half, tm), F32)], axis=0).T
    lane = lax.broadcasted_iota(I32, (tm, LANES), 1)
    qw_nope = qw_ref[:, 0:MLA_NOPE]
    qw_rope = qw_ref[:, MLA_NOPE:MLA_HEAD_PAD]
    kw_nope = kw_ref[:, 0:MLA_NOPE]
    kw_rope = kw_ref[:, MLA_NOPE:MLA_HEAD_PAD]
    kr_sq = kr * kr
    kr_rot = _rope(kr * kw_rope, cos, sin, lane)
    scale = (MLA_QK ** -0.5) * LOG2_E
    for h in range(MLA_HEADS):
        base = h * MLA_HEAD_PAD
        qn = q_all[:, base:base + MLA_NOPE]
        qr = q_all[:, base + MLA_NOPE:base + MLA_HEAD_PAD]
        ss = jnp.sum(qn * qn + qr * qr, axis=-1, keepdims=True)
        r = lax.rsqrt(ss / MLA_QK + EPS)
        q_ref[:, base:base + MLA_NOPE] = ((qn * r) * qw_nope * scale).astype(BF16)
        q_ref[:, base + MLA_NOPE:base + MLA_HEAD_PAD] = (
            _rope((qr * r) * qw_rope, cos, sin, lane) * scale).astype(BF16)
        kn = k_all[:, h * MLA_NOPE:(h + 1) * MLA_NOPE]
        ssk = jnp.sum(kn * kn + kr_sq, axis=-1, keepdims=True)
        rk = lax.rsqrt(ssk / MLA_QK + EPS)
        k_ref[:, base:base + MLA_NOPE] = ((kn * rk) * kw_nope).astype(BF16)
        k_ref[:, base + MLA_NOPE:base + MLA_HEAD_PAD] = (kr_rot * rk).astype(BF16)


def _flash_kernel(q_ref, k_ref, vt_ref, o_ref, s_ref, cm_ref, acc_ref, m_ref, *, t):
    seq = q_ref.shape[0]
    nq = seq // t
    krow = lax.broadcasted_iota(I32, (t, t), 0)
    qcol = lax.broadcasted_iota(I32, (t, t), 1)
    items = [(qi, j) for qi in range(nq) for j in range(qi + 1)]

    def scores(idx):
        qi, j = items[idx]
        s = _dot_nt(k_ref[j * t:(j + 1) * t, :], q_ref[qi * t:(qi + 1) * t, :])
        s_ref[idx % 2] = s
        cm_ref[idx % 2] = jnp.max(s, axis=0, keepdims=True)

    def update(idx):
        qi, j = items[idx]
        if j == 0:
            m_ref[...] = jnp.full_like(m_ref, -jnp.inf)
            acc_ref[...] = jnp.zeros_like(acc_ref)
        s = s_ref[idx % 2]
        if j == qi:
            s = jnp.where(krow <= qcol, s, -jnp.inf)
            cm = jnp.max(s, axis=0, keepdims=True)
        else:
            cm = cm_ref[idx % 2]
        m_prev = m_ref[...]
        m_new = jnp.maximum(m_prev, cm)
        alpha = jnp.exp2(m_prev - m_new)
        p = jnp.exp2(s - m_new).astype(BF16)
        m_ref[...] = m_new
        acc_ref[...] = alpha * acc_ref[...] + _dot(vt_ref[0, :, j * t:(j + 1) * t], p)
        if j == qi:
            acc = acc_ref[...]
            out_t = acc[0:MLA_V, :] * (1.0 / acc[MLA_V:MLA_V + 1, :])
            o_ref[qi * t:(qi + 1) * t, :] = out_t.T.astype(BF16)

    scores(0)
    for idx in range(len(items)):
        if idx + 1 < len(items):
            scores(idx + 1)
        update(idx)


def _flash(q, k, vt, batch, seq, t):
    rows = q.shape[0]
    return pl.pallas_call(
        functools.partial(_flash_kernel, t=t),
        grid=(batch, MLA_HEADS),
        in_specs=[
            pl.BlockSpec((seq, MLA_HEAD_PAD), lambda b, h: (b, h)),
            pl.BlockSpec((seq, MLA_HEAD_PAD), lambda b, h: (b, h)),
            pl.BlockSpec((1, FLASH_V_ROWS, seq), lambda b, h: (b, h, 0)),
        ],
        out_specs=pl.BlockSpec((seq, MLA_V), lambda b, h: (b, h)),
        out_shape=jax.ShapeDtypeStruct((rows, MLA_HEADS * MLA_V), BF16),
        scratch_shapes=[
            pltpu.VMEM((2, t, t), F32),
            pltpu.VMEM((2, 1, t), F32),
            pltpu.VMEM((FLASH_V_ROWS, t), F32),
            pltpu.VMEM((1, t), F32),
        ],
        compiler_params=_params("parallel", "parallel"),
        name="flash",
    )(q, k, vt)


def _merge_route_kernel(x_ref, oa_ref, ob_ref, gate_ref, wa_ref, wb_ref, wo_ref, nw_ref,
                        wr_hi_ref, wr_lo_ref, br_ref,
                        x1_ref, xn_ref, eid_ref, cw_ref, cnt_ref):
    @pl.when(pl.program_id(0) == 0)
    def _():
        cnt_ref[...] = jnp.zeros_like(cnt_ref)

    tm = x_ref.shape[0]
    ya = _dot(oa_ref[...], wa_ref[...])
    yb = _dot(ob_ref[...], wb_ref[...])
    ga = gate_ref[:, 0:D_MODEL].astype(F32)
    gb = gate_ref[:, D_MODEL:2 * D_MODEL].astype(F32)
    merged = jax.nn.sigmoid(ga) * ya + jax.nn.sigmoid(gb) * yb
    x1 = x_ref[...] + _dot(merged.astype(BF16), wo_ref[...])
    x1_ref[...] = x1
    ms = jnp.mean(x1 * x1, axis=-1, keepdims=True)
    xn = (x1 * lax.rsqrt(ms + EPS)) * nw_ref[...]
    xn_ref[...] = xn
    xn_hi, xn_lo = _split_bf16(xn)
    logits = (_dot(xn_hi, wr_hi_ref[...]) + _dot(xn_lo, wr_hi_ref[...])
              + _dot(xn_hi, wr_lo_ref[...])) + br_ref[...]
    lane = lax.broadcasted_iota(I32, (tm, LANES), 1)
    work = jnp.where(lane < N_EXPERTS, logits, -jnp.inf)
    vals, idxs = [], []
    for _ in range(TOP_K):
        m = jnp.max(work, axis=-1, keepdims=True)
        idx = jnp.min(jnp.where(work == m, lane, LANES), axis=-1, keepdims=True)
        vals.append(m)
        idxs.append(idx)
        work = jnp.where(lane == idx, -jnp.inf, work)
    exps = [jnp.exp(v - vals[0]) for v in vals]
    denom = exps[0] + exps[1] + exps[2] + exps[3]
    eid = jnp.zeros((tm, LANES), I32)
    cw = jnp.zeros((tm, LANES), F32)
    sel = jnp.zeros((tm, LANES), F32)
    for kk in range(TOP_K):
        eid = jnp.where(lane == kk, idxs[kk], eid)
        cw = jnp.where(lane == kk, exps[kk] / denom, cw)
        sel = sel + (lane == idxs[kk]).astype(F32)
    eid_ref[...] = eid
    cw_ref[...] = cw
    cnt_ref[0:1, :] = cnt_ref[0:1, :] + jnp.sum(sel, axis=0, keepdims=True)


def _merge_route(x2, oa, ob, gates, wa, wb, wo, nw, wr_hi, wr_lo, br, tm):
    t = x2.shape[0]
    const = lambda i: (0, 0)
    row = lambda i: (i, 0)
    return pl.pallas_call(
        _merge_route_kernel,
        grid=(t // tm,),
        in_specs=[
            pl.BlockSpec((tm, D_MODEL), row),
            pl.BlockSpec((tm, GLA_V_WIDTH), row),
            pl.BlockSpec((tm, MLA_HEADS * MLA_V), row),
            pl.BlockSpec((tm, _GATE_W), row),
            pl.BlockSpec((GLA_V_WIDTH, D_MODEL), const),
            pl.BlockSpec((MLA_HEADS * MLA_V, D_MODEL), const),
            pl.BlockSpec((D_MODEL, D_MODEL), const),
            pl.BlockSpec((1, D_MODEL), const),
            pl.BlockSpec((D_MODEL, LANES), const),
            pl.BlockSpec((D_MODEL, LANES), const),
            pl.BlockSpec((1, LANES), const),
        ],
        out_specs=[
            pl.BlockSpec((tm, D_MODEL), row),
            pl.BlockSpec((tm, D_MODEL), row),
            pl.BlockSpec((tm, LANES), row),
            pl.BlockSpec((tm, LANES), row),
            pl.BlockSpec((8, LANES), const),
        ],
        out_shape=[
            jax.ShapeDtypeStruct((t, D_MODEL), F32),
            jax.ShapeDtypeStruct((t, D_MODEL), F32),
            jax.ShapeDtypeStruct((t, LANES), I32),
            jax.ShapeDtypeStruct((t, LANES), F32),
            jax.ShapeDtypeStruct((8, LANES), F32),
        ],
        compiler_params=_params("arbitrary"),
        name="merge_route",
    )(x2, oa, ob, gates, wa, wb, wo, nw, wr_hi, wr_lo, br)


def _positions_kernel(eid_ref, cnt_ref, pos_ref, carry_ref, *, tile):
    @pl.when(pl.program_id(0) == 0)
    def _():
        carry_ref[...] = jnp.zeros_like(carry_ref)

    tb = eid_ref.shape[0]
    lane1 = lax.broadcasted_iota(I32, (1, LANES), 1)
    cnt = cnt_ref[0:1, :]
    padded = jnp.floor((cnt + (tile - 1)) / tile) * tile
    incl = padded
    shift = 1
    while shift < N_EXPERTS:
        incl = incl + jnp.where(lane1 >= shift, pltpu.roll(incl, shift, axis=1), 0.0)
        shift *= 2
    offs = incl - padded

    lane = lax.broadcasted_iota(I32, (tb, LANES), 1)
    eid = eid_ref[...]
    onehots = [lane == jnp.broadcast_to(eid[:, kk:kk + 1], (tb, LANES)) for kk in range(TOP_K)]
    sel = jnp.zeros((tb, LANES), F32)
    for oh in onehots:
        sel = sel + oh.astype(F32)
    row = lax.broadcasted_iota(I32, (tb, tb), 0)
    col = lax.broadcasted_iota(I32, (tb, tb), 1)
    strict = (row > col).astype(BF16)
    rank = _dot(strict, sel.astype(BF16)) + carry_ref[...] + offs
    pos = jnp.zeros((tb, LANES), I32)
    for kk in range(TOP_K):
        pk = jnp.sum(jnp.where(onehots[kk], rank, 0.0), axis=-1, keepdims=True)
        pos = jnp.where(lane == kk, pk.astype(I32), pos)
    pos_ref[...] = pos
    carry_ref[...] = carry_ref[...] + jnp.sum(sel, axis=0, keepdims=True)


def _positions(eid, cnt, tb, tile):
    t = eid.shape[0]
    return pl.pallas_call(
        functools.partial(_positions_kernel, tile=tile),
        grid=(t // tb,),
        in_specs=[
            pl.BlockSpec((tb, LANES), lambda i: (i, 0)),
            pl.BlockSpec((8, LANES), lambda i: (0, 0)),
        ],
        out_specs=pl.BlockSpec((tb, LANES), lambda i: (i, 0)),
        out_shape=jax.ShapeDtypeStruct((t, LANES), I32),
        scratch_shapes=[pltpu.VMEM((1, LANES), F32)],
        compiler_params=_params("arbitrary"),
        name="positions",
    )(eid, cnt)


ROW_DMA_UNROLL = 8


def _dispatch_kernel(tz_ref, pos_ref, xn_ref, xs_ref, zero_ref, sem, zsem, *, tile):
    i = pl.program_id(0)
    tb = xn_ref.shape[0]

    def zero_copy(e):
        start = pl.multiple_of(tz_ref[e], tile)
        return pltpu.make_async_copy(zero_ref, xs_ref.at[pl.ds(start, tile), 0], zsem)

    @pl.when(i == 0)
    def _():
        zero_ref[...] = jnp.zeros_like(zero_ref)
        for e in range(N_EXPERTS):
            @pl.when(tz_ref[e] >= 0)
            def _():
                zero_copy(e).start()
        for e in range(N_EXPERTS):
            @pl.when(tz_ref[e] >= 0)
            def _():
                zero_copy(e).wait()

    def issue(t, carry):
        for kk in range(TOP_K):
            p = pos_ref[0, 0, t * TOP_K + kk]
            pltpu.make_async_copy(xn_ref.at[pl.ds(t, 1)], xs_ref.at[p], sem).start(priority=kk % 2)
        return carry

    lax.fori_loop(0, tb, issue, 0, unroll=ROW_DMA_UNROLL)
    for _ in range(TOP_K):
        pltpu.make_async_copy(xn_ref, xs_ref.at[pl.ds(0, tb), 0], sem).wait()


def _dispatch(tile_zero, pos3, xn, n_rows, tb, tile):
    t = xn.shape[0]
    return pl.pallas_call(
        functools.partial(_dispatch_kernel, tile=tile),
        grid_spec=pltpu.PrefetchScalarGridSpec(
            num_scalar_prefetch=1,
            grid=(t // tb,),
            in_specs=[
                pl.BlockSpec((1, 1, tb * TOP_K), lambda i, tz: (i, 0, 0), memory_space=pltpu.SMEM),
                pl.BlockSpec((tb, D_MODEL), lambda i, tz: (i, 0)),
            ],
            out_specs=pl.BlockSpec(memory_space=pl.ANY),
            scratch_shapes=[pltpu.VMEM((tile, D_MODEL), F32), pltpu.SemaphoreType.DMA,
                            pltpu.SemaphoreType.DMA],
        ),
        out_shape=jax.ShapeDtypeStruct((n_rows, 1, D_MODEL), F32),
        compiler_params=_params("arbitrary"),
        name="dispatch",
    )(tile_zero, pos3, xn)


def _experts_kernel(te_ref, nv_ref, xs_ref, xsc_ref, wg_ref, bg_ref, wu_ref, bu_ref, wd_ref, bd_ref,
                    ys_ref,
                    wgb_ref, wub_ref, wdb_ref, wsc_ref, xbuf, ybuf, isem, osem, *, tile):
    i = pl.program_id(0)
    nv = nv_ref[0]
    valid = i < nv
    slot = i % 2

    def in_copies(step, s):
        r0 = pl.multiple_of(step * tile, tile)
        return [pltpu.make_async_copy(xs_ref.at[pl.ds(r0, tile), 0, pl.ds(c * LANES, LANES)],
                                      xbuf.at[s, :, pl.ds(c * LANES, LANES)], isem.at[s])
                for c in range(LANE_CHUNKS)]

    def out_copies(step, s):
        r0 = pl.multiple_of(step * tile, tile)
        return [pltpu.make_async_copy(ybuf.at[s, :, pl.ds(c * LANES, LANES)],
                                      ys_ref.at[pl.ds(r0, tile), 0, pl.ds(c * LANES, LANES)],
                                      osem.at[s])
                for c in range(LANE_CHUNKS)]

    @pl.when(i == 0)
    def _():
        for cp in in_copies(0, 0):
            cp.start()

    @pl.when(i + 1 < nv)
    def _():
        for cp in in_copies(i + 1, 1 - slot):
            cp.start()

    new_expert = jnp.logical_or(i == 0, te_ref[i] != te_ref[jnp.maximum(i - 1, 0)])

    @pl.when(jnp.logical_and(valid, new_expert))
    def _():
        for slot_w, (w_ref, w8_ref) in enumerate(((wg_ref, wgb_ref), (wu_ref, wub_ref),
                                                  (wd_ref, wdb_ref))):
            w = w_ref[0]
            sc = _pow2_scale(_abs_max(w))
            w8_ref[...] = (w * sc).astype(F8)
            wsc_ref[slot_w:slot_w + 1, :] = jnp.broadcast_to(1.0 / sc, (1, LANES))

    @pl.when(valid)
    def _():
        for cp in in_copies(i, slot):
            cp.wait()

        @pl.when(i >= 2)
        def _():
            for cp in out_copies(i - 2, slot):
                cp.wait()

        x = xbuf[slot]
        sx = xsc_ref[0:1, 0:1]
        x8 = (x * sx).astype(F8)
        inv_x = 1.0 / sx
        g = _dot(x8, wgb_ref[...]) * (inv_x * wsc_ref[0:1, 0:1]) + bg_ref[0]
        u = _dot(x8, wub_ref[...]) * (inv_x * wsc_ref[1:2, 0:1]) + bu_ref[0]
        g = jnp.minimum(g, SWIGLU_LIMIT)
        u = jnp.clip(u, -SWIGLU_LIMIT, SWIGLU_LIMIT)
        hidden = (u + 1.0) * g * jax.nn.sigmoid(SWIGLU_ALPHA * g)
        h8 = (hidden * HIDDEN_SCALE).astype(F8)
        ybuf[slot] = (_dot(h8, wdb_ref[...]) * (wsc_ref[2:3, 0:1] * (1.0 / HIDDEN_SCALE))
                      + bd_ref[0])
        for cp in out_copies(i, slot):
            cp.start()

    @pl.when(i == nv - 1)
    def _():
        for cp in out_copies(i, slot):
            cp.wait()

        @pl.when(i >= 1)
        def _():
            for cp in out_copies(i - 1, 1 - slot):
                cp.wait()


def _experts(tile_expert, n_valid, xs, x_scale, wg, bg, wu, bu, wd, bd, tile):
    rows = xs.shape[0]
    nt = rows // tile
    wmap = lambda i, te, nv: (te[i], 0, 0)
    return pl.pallas_call(
        functools.partial(_experts_kernel, tile=tile),
        grid_spec=pltpu.PrefetchScalarGridSpec(
            num_scalar_prefetch=2,
            grid=(nt,),
            in_specs=[
                pl.BlockSpec(memory_space=pl.ANY),
                pl.BlockSpec((1, LANES), lambda i, te, nv: (0, 0)),
                pl.BlockSpec((1, D_MODEL, D_FF), wmap),
                pl.BlockSpec((1, 1, D_FF), wmap),
                pl.BlockSpec((1, D_MODEL, D_FF), wmap),
                pl.BlockSpec((1, 1, D_FF), wmap),
                pl.BlockSpec((1, D_FF, D_MODEL), wmap),
                pl.BlockSpec((1, 1, D_MODEL), wmap),
            ],
            out_specs=pl.BlockSpec(memory_space=pl.ANY),
            scratch_shapes=[pltpu.VMEM((D_MODEL, D_FF), F8), pltpu.VMEM((D_MODEL, D_FF), F8),
                            pltpu.VMEM((D_FF, D_MODEL), F8), pltpu.VMEM((8, LANES), F32),
                            pltpu.VMEM((2, tile, D_MODEL), F32), pltpu.VMEM((2, tile, D_MODEL), F32),
                            pltpu.SemaphoreType.DMA((2,)), pltpu.SemaphoreType.DMA((2,))],
        ),
        out_shape=jax.ShapeDtypeStruct((rows, 1, D_MODEL), F32),
        compiler_params=_params("arbitrary"),
        name="experts",
    )(tile_expert, n_valid, xs, x_scale, wg, bg, wu, bu, wd, bd)


def _combine_kernel(pos_ref, posn_ref, x1_ref, cw_ref, ys_ref, o_ref, buf_ref, sem0, sem1):
    i = pl.program_id(0)
    n = pl.num_programs(0)
    tb = x1_ref.shape[0] // 2
    def issue(p_ref, sub, slot, sem):
        def body(t, carry):
            for kk in range(TOP_K):
                p = p_ref[0, 0, (sub * tb + t) * TOP_K + kk]
                pltpu.make_async_copy(ys_ref.at[p], buf_ref.at[slot, kk, pl.ds(t, 1)],
                                      sem).start(priority=kk % 2)
            return carry
        lax.fori_loop(0, tb, body, 0, unroll=ROW_DMA_UNROLL)

    def drain(slot, sem):
        for kk in range(TOP_K):
            pltpu.make_async_copy(ys_ref.at[pl.ds(0, tb), 0], buf_ref.at[slot, kk], sem).wait()

    def reduce(sub, slot):
        rows = slice(sub * tb, (sub + 1) * tb)
        cw = cw_ref[rows, :]
        acc = x1_ref[rows, :]
        for kk in range(TOP_K):
            acc = acc + cw[:, kk:kk + 1] * buf_ref[slot, kk]
        o_ref[rows, :] = acc

    @pl.when(i == 0)
    def _():
        issue(pos_ref, 0, 0, sem0)

    issue(pos_ref, 1, 1, sem1)
    drain(0, sem0)
    reduce(0, 0)

    @pl.when(i + 1 < n)
    def _():
        issue(posn_ref, 0, 0, sem0)

    drain(1, sem1)
    reduce(1, 1)


def _combine(pos3, x1, cw, ys, tb):
    t = x1.shape[0]
    n = t // (2 * tb)
    return pl.pallas_call(
        _combine_kernel,
        grid=(n,),
        in_specs=[
            pl.BlockSpec((1, 1, 2 * tb * TOP_K), lambda i: (i, 0, 0), memory_space=pltpu.SMEM),
            pl.BlockSpec((1, 1, 2 * tb * TOP_K), lambda i: (jnp.minimum(i + 1, n - 1), 0, 0),
                         memory_space=pltpu.SMEM),
            pl.BlockSpec((2 * tb, D_MODEL), lambda i: (i, 0)),
            pl.BlockSpec((2 * tb, LANES), lambda i: (i, 0)),
            pl.BlockSpec(memory_space=pl.ANY),
        ],
        out_specs=pl.BlockSpec((2 * tb, D_MODEL), lambda i: (i, 0)),
        out_shape=jax.ShapeDtypeStruct((t, D_MODEL), F32),
        scratch_shapes=[pltpu.VMEM((2, TOP_K, tb, D_MODEL), F32), pltpu.SemaphoreType.DMA,
                        pltpu.SemaphoreType.DMA],
        compiler_params=_params("arbitrary"),
        name="combine",
    )(pos3, pos3, x1, cw, ys)


def _pad_cols(a, width):
    return jnp.pad(a, ((0, 0), (0, width - a.shape[1])))


def _pack_in_proj(w_in):
    o = np.cumsum((GLA_QK_WIDTH, GLA_QK_WIDTH, GLA_V_WIDTH, GLA_V_WIDTH, GLA_GATE_RANK,
                   MLA_Q_RANK, MLA_KV_RANK, MLA_ROPE, D_MODEL, D_MODEL)).tolist()
    gla = w_in[:, 0:o[3]]
    lr = _pad_cols(w_in[:, o[3]:o[4]], _LR_W)
    mla = _pad_cols(w_in[:, o[4]:o[7]], _MLA_W)
    gates = w_in[:, o[7]:o[9]]
    return jnp.concatenate([gla, lr, mla, gates], axis=1).astype(BF16)


def _pad_heads(w, head_w):
    r = w.shape[0]
    w3 = w.reshape(r, MLA_HEADS, head_w)
    w3 = jnp.pad(w3, ((0, 0), (0, 0), (0, MLA_HEAD_PAD - head_w)))
    return w3.reshape(r, MLA_HEADS * MLA_HEAD_PAD)


def _tile_rows(n, pref):
    return pref if n % pref == 0 else n


def kernel(x, positions, attn_norm_w, w_in, w_gla_gk, b_gla_gk, gla_out_norm_w, w_gla_out,
           mla_q_norm_w, w_mla_uq, mla_kv_norm_w, w_mla_ukv, mla_qk_q_norm_w, mla_qk_k_norm_w,
           w_mla_out, w_out, moe_norm_w, w_router, b_router, w_exp_gate, b_exp_gate,
           w_exp_up, b_exp_up, w_exp_down, b_exp_down):
    batch, seq, _ = x.shape
    depth = w_in.shape[0]
    t = batch * seq
    x2 = x.reshape(t, D_MODEL)
    pos = positions.reshape(1, t).astype(F32)
    half = MLA_ROPE // 2
    freq = (ROPE_BASE ** (-jnp.arange(half, dtype=F32) / half))[:, None]

    tm = _tile_rows(t, TOKEN_TILE)
    seq_tile = _tile_rows(seq, TOKEN_TILE)
    moe_tile = TOKEN_TILE
    n_rows = ((t * TOP_K) // moe_tile + N_EXPERTS) * moe_tile

    for l in range(depth):
        w_pack = _pack_in_proj(w_in[l])
        wgk = jnp.pad(w_gla_gk[l], ((0, LANES - GLA_GATE_RANK), (0, 0)))
        wgk_hi, wgk_lo = _split_bf16(wgk)
        wuq = _pad_heads(w_mla_uq[l], MLA_QK).astype(BF16)
        wukv3 = w_mla_ukv[l].reshape(MLA_KV_RANK, MLA_HEADS, MLA_NOPE + MLA_V)
        wuk = wukv3[:, :, :MLA_NOPE].reshape(MLA_KV_RANK, MLA_HEADS * MLA_NOPE).astype(BF16)
        wvt = jnp.pad(wukv3[:, :, MLA_NOPE:].transpose(1, 2, 0),
                      ((0, 0), (0, FLASH_V_ROWS - MLA_V), (0, 0)))
        wvt = wvt.reshape(MLA_HEADS * FLASH_V_ROWS, MLA_KV_RANK).astype(BF16)
        vone = (jnp.arange(MLA_HEADS * FLASH_V_ROWS) % FLASH_V_ROWS == MLA_V).astype(F32)[:, None]
        qw = _pad_cols(mla_qk_q_norm_w[l][None, :], MLA_HEAD_PAD)
        kw = _pad_cols(mla_qk_k_norm_w[l][None, :], MLA_HEAD_PAD)
        o_a, gates, q, k, vt = _inproj(
            x2, attn_norm_w[l][None, :], w_pack, wgk_hi, wgk_lo, b_gla_gk[l][None, :], pos,
            wuq, wuk, wvt, vone, mla_q_norm_w[l][None, :], mla_kv_norm_w[l][None, :], qw, kw, freq,
            gla_out_norm_w[l][None, :], batch, seq, seq_tile)
        o_b = _flash(q, k, vt, batch, seq, seq_tile)

        wr = _pad_cols(w_router[l], LANES)
        wr_hi, wr_lo = _split_bf16(wr)
        br = _pad_cols(b_router[l][None, :], LANES)
        x1, xn, eid, cw, cnt = _merge_route(
            x2, o_a, o_b, gates, w_gla_out[l].astype(BF16), w_mla_out[l].astype(BF16),
            w_out[l].astype(BF16), moe_norm_w[l][None, :], wr_hi, wr_lo, br, tm)

        pos3 = _positions(eid, cnt, tm, moe_tile)[:, :TOP_K].reshape(t // tm, 1, tm * TOP_K)
        counts = cnt[0, :N_EXPERTS].astype(I32)
        padded = ((counts + moe_tile - 1) // moe_tile) * moe_tile
        ends = jnp.cumsum(padded)
        n_tiles = n_rows // moe_tile
        n_valid = (ends[-1] // moe_tile).astype(I32)
        starts = jnp.arange(n_tiles, dtype=I32) * moe_tile
        starts = jnp.minimum(starts, ends[-1] - moe_tile)
        tile_expert = jnp.minimum(jnp.sum((starts[:, None] >= ends[None, :]).astype(I32), axis=1),
                                  N_EXPERTS - 1)
        tile_zero = jnp.where(padded > 0, ends - moe_tile, -1).astype(I32)

        xs = _dispatch(tile_zero, pos3, xn, n_rows, tm, moe_tile)
        x_bound = (D_MODEL ** 0.5) * jnp.max(jnp.abs(moe_norm_w[l]))
        x_scale = jnp.broadcast_to(_pow2_scale(x_bound), (1, LANES))
        ys = _experts(tile_expert, n_valid.reshape(1), xs, x_scale,
                      w_exp_gate[l], b_exp_gate[l][:, None, :],
                      w_exp_up[l], b_exp_up[l][:, None, :],
                      w_exp_down[l], b_exp_down[l][:, None, :], moe_tile)
        x2 = _combine(pos3, x1, cw, ys, tm // 2)
    return x2.reshape(batch, seq, D_MODEL)
```

```python
import functools

import jax
import jax.numpy as jnp
import numpy as np
from jax import lax
from jax.experimental import pallas as pl
from jax.experimental.pallas import tpu as pltpu

F32 = jnp.float32
BF16 = jnp.bfloat16
I32 = jnp.int32
F8 = jnp.float8_e4m3fn
F8_TARGET = 224.0
HIDDEN_SCALE = 4.0

D_MODEL = 1024
EPS = 1e-6
GLA_HEADS = 4
GLA_DK = 128
GLA_DV = 256
GLA_GATE_RANK = 16
GLA_GATE_NORMALIZER = 16.0
GLA_QK_WIDTH = GLA_HEADS * GLA_DK
GLA_V_WIDTH = GLA_HEADS * GLA_DV
MLA_HEADS = 8
MLA_Q_RANK = 384
MLA_KV_RANK = 256
MLA_NOPE = 128
MLA_ROPE = 64
MLA_V = 128
MLA_QK = MLA_NOPE + MLA_ROPE
ROPE_BASE = 10000.0
N_EXPERTS = 32
TOP_K = 4
D_FF = 1024
SWIGLU_LIMIT = 7.0
SWIGLU_ALPHA = 1.702

LANES = 128
MLA_HEAD_PAD = 256
GLA_CHUNK = 128
TOKEN_TILE = 512
FLASH_TILE = 1024
FLASH_V_ROWS = 144
LOG2_E = 1.4426950408889634
VMEM_LIMIT = 56 * 1024 * 1024

_GLA_W = 2 * GLA_QK_WIDTH + 2 * GLA_V_WIDTH
_LR_W = LANES
_MLA_W = 768
_GATE_W = 2 * D_MODEL
_PACK_W = _GLA_W + _LR_W + _MLA_W + _GATE_W


def _dot(a, b):
    return jnp.dot(a, b, preferred_element_type=F32)


def _dot_nt(a, b):
    return lax.dot_general(a, b, (((1,), (1,)), ((), ())), preferred_element_type=F32)


def _dot_tn(a, b):
    return lax.dot_general(a, b, (((0,), (0,)), ((), ())), preferred_element_type=F32)


def _abs_max(a):
    return jnp.max(jnp.max(jnp.abs(a), axis=-1, keepdims=True), axis=0, keepdims=True)


def _pow2_scale(amax):
    return jnp.exp2(jnp.floor(jnp.log2(F8_TARGET / jnp.maximum(amax, 1e-30))))


def _split_bf16(a):
    hi = a.astype(BF16)
    lo = (a - hi.astype(F32)).astype(BF16)
    return hi, lo


def _params(*sem):
    return pltpu.CompilerParams(dimension_semantics=sem, vmem_limit_bytes=VMEM_LIMIT)


LANE_CHUNKS = D_MODEL // LANES


def _inproj_kernel(x_ref, nw_ref, w_ref, wgk_hi_ref, wgk_lo_ref, bgk_ref, pos_ref,
                   wuq_ref, wuk_ref, wvt_ref, vone_ref, qnw_ref, kvnw_ref, qw_ref, kw_ref, freq_ref,
                   gnw_ref, oa_ref, gate_ref, q_ref, k_ref, vt_ref, state_ref, *, steps_per_seq):
    @pl.when(pl.program_id(0) % steps_per_seq == 0)
    def _():
        state_ref[...] = jnp.zeros_like(state_ref)

    x = x_ref[...]
    ms = jnp.mean(x * x, axis=-1, keepdims=True)
    h = (x * lax.rsqrt(ms + EPS)) * nw_ref[...]
    hb = h.astype(BF16)
    mla = _dot(hb, w_ref[:, _GLA_W + _LR_W:_GLA_W + _LR_W + _MLA_W])
    _mla_prep(mla, pos_ref, wuq_ref, wuk_ref, wvt_ref, vone_ref, qnw_ref, kvnw_ref, qw_ref, kw_ref,
              freq_ref, q_ref, k_ref, vt_ref)
    o0 = 0
    gq = (_dot(hb, w_ref[:, 0:GLA_QK_WIDTH]) * (GLA_DK ** -0.5)).astype(BF16)
    gk = _dot(hb, w_ref[:, GLA_QK_WIDTH:2 * GLA_QK_WIDTH]).astype(BF16)
    gv = _dot(hb, w_ref[:, 2 * GLA_QK_WIDTH:2 * GLA_QK_WIDTH + GLA_V_WIDTH]).astype(BF16)
    gg = _dot(hb, w_ref[:, 2 * GLA_QK_WIDTH + GLA_V_WIDTH:_GLA_W])
    o0 = _GLA_W
    lr = _dot(hb, w_ref[:, o0:o0 + _LR_W])
    lr_hi, lr_lo = _split_bf16(lr)
    a_logit = (_dot(lr_hi, wgk_hi_ref[...]) + _dot(lr_lo, wgk_hi_ref[...])
               + _dot(lr_hi, wgk_lo_ref[...])) + bgk_ref[...]
    log_sig = jnp.minimum(a_logit, 0.0) - jnp.log1p(jnp.exp(-jnp.abs(a_logit)))
    _gla_chunks(gq, gk, gv, gg, log_sig / GLA_GATE_NORMALIZER, gnw_ref, oa_ref, state_ref)
    o0 += _LR_W + _MLA_W
    gate_ref[...] = _dot(hb, w_ref[:, o0:o0 + _GATE_W]).astype(BF16)


def _inproj(x2, nw, w_pack, wgk_hi, wgk_lo, bgk, pos, wuq, wuk, wvt, vone, qnw, kvnw, qw, kw, freq,
            gnw, batch, seq, tm):
    t = x2.shape[0]
    nblk = seq // tm
    const = lambda i: (0, 0)
    row = lambda i: (i, 0)
    hw = MLA_HEADS * MLA_HEAD_PAD
    return pl.pallas_call(
        functools.partial(_inproj_kernel, steps_per_seq=nblk),
        grid=(t // tm,),
        in_specs=[
            pl.BlockSpec((tm, D_MODEL), row),
            pl.BlockSpec((1, D_MODEL), const),
            pl.BlockSpec((D_MODEL, _PACK_W), const, pipeline_mode=pl.Buffered(1)),
            pl.BlockSpec((LANES, GLA_QK_WIDTH), const),
            pl.BlockSpec((LANES, GLA_QK_WIDTH), const),
            pl.BlockSpec((1, GLA_QK_WIDTH), const),
            pl.BlockSpec((1, tm), lambda i: (0, i)),
            pl.BlockSpec((MLA_Q_RANK, hw), const),
            pl.BlockSpec((MLA_KV_RANK, MLA_HEADS * MLA_NOPE), const),
            pl.BlockSpec((MLA_HEADS * FLASH_V_ROWS, MLA_KV_RANK), const),
            pl.BlockSpec((MLA_HEADS * FLASH_V_ROWS, 1), const),
            pl.BlockSpec((1, MLA_Q_RANK), const),
            pl.BlockSpec((1, MLA_KV_RANK), const),
            pl.BlockSpec((1, MLA_HEAD_PAD), const),
            pl.BlockSpec((1, MLA_HEAD_PAD), const),
            pl.BlockSpec((MLA_ROPE // 2, 1), const),
            pl.BlockSpec((1, GLA_DV), const),
        ],
        out_specs=[
            pl.BlockSpec((tm, GLA_V_WIDTH), row),
            pl.BlockSpec((tm, _GATE_W), row),
            pl.BlockSpec((tm, hw), row),
            pl.BlockSpec((tm, hw), row),
            pl.BlockSpec((1, MLA_HEADS * FLASH_V_ROWS, tm), lambda i: (i // nblk, 0, i % nblk)),
        ],
        out_shape=[
            jax.ShapeDtypeStruct((t, GLA_V_WIDTH), BF16),
            jax.ShapeDtypeStruct((t, _GATE_W), BF16),
            jax.ShapeDtypeStruct((t, hw), BF16),
            jax.ShapeDtypeStruct((t, hw), BF16),
            jax.ShapeDtypeStruct((batch, MLA_HEADS * FLASH_V_ROWS, seq), BF16),
        ],
        scratch_shapes=[pltpu.VMEM((GLA_HEADS, GLA_DV, GLA_DK), F32)],
        compiler_params=_params("arbitrary"),
        name="inproj",
    )(x2, nw, w_pack, wgk_hi, wgk_lo, bgk, pos, wuq, wuk, wvt, vone, qnw, kvnw, qw, kw, freq, gnw)


def _gla_chunks(q_all, k_all, v_all, g_all, la_all, nw_ref, o_ref, state_ref):
    c = GLA_CHUNK
    row = lax.broadcasted_iota(I32, (c, c), 0)
    col = lax.broadcasted_iota(I32, (c, c), 1)
    causal = row >= col
    tri = causal.astype(BF16)
    mid = c // 2 - 1
    for s in range(q_all.shape[0] // c):
        r0 = s * c
        la = la_all[r0:r0 + c, :]
        la_hi, la_lo = _split_bf16(la)
        cum_all = _dot(tri, la_hi) + _dot(tri, la_lo)
        for h in range(GLA_HEADS):
            ks = slice(h * GLA_DK, (h + 1) * GLA_DK)
            vs = slice(h * GLA_DV, (h + 1) * GLA_DV)
            cum = cum_all[:, ks]
            q = q_all[r0:r0 + c, ks].astype(F32)
            k = k_all[r0:r0 + c, ks].astype(F32)
            v = v_all[r0:r0 + c, vs]
            ref_row = cum[mid:mid + 1, :]
            last = cum[c - 1:c, :]
            qg = (q * jnp.exp(cum - ref_row)).astype(BF16)
            kg = (k * jnp.exp(ref_row - cum)).astype(BF16)
            scores = jnp.where(causal, _dot_nt(qg, kg), 0.0).astype(BF16)
            o = _dot(scores, v)
            st = state_ref[h]
            qe = (q * jnp.exp(cum)).astype(BF16)
            o = o + _dot_nt(qe, st.astype(BF16))
            ko = (k * jnp.exp(last - cum)).astype(BF16)
            state_ref[h] = st * jnp.exp(last) + _dot_tn(v, ko)
            ms = jnp.mean(o * o, axis=-1, keepdims=True)
            on = (o * lax.rsqrt(ms + EPS)) * nw_ref[...]
            g = g_all[r0:r0 + c, vs]
            o_ref[r0:r0 + c, vs] = (on * (g * jax.nn.sigmoid(g))).astype(BF16)


def _rope(x, cos, sin, lane):
    half = MLA_ROPE // 2
    rot = jnp.where(lane < half, -pltpu.roll(x, LANES - half, axis=1), pltpu.roll(x, half, axis=1))
    return x * cos + rot * sin


def _mla_prep(mla, pos_ref, wuq_ref, wuk_ref, wvt_ref, vone_ref, qnw_ref, kvnw_ref,
              qw_ref, kw_ref, freq_ref, q_ref, k_ref, vt_ref):
    tm = mla.shape[0]
    cq = mla[:, 0:MLA_Q_RANK]
    ckv = mla[:, MLA_Q_RANK:MLA_Q_RANK + MLA_KV_RANK]
    kr = mla[:, MLA_Q_RANK + MLA_KV_RANK:_MLA_W]

    def rms(a, w):
        ms = jnp.mean(a * a, axis=-1, keepdims=True)
        return (a * lax.rsqrt(ms + EPS)) * w

    q_all = _dot(rms(cq, qnw_ref[...]).astype(BF16), wuq_ref[...])
    ckvn = rms(ckv, kvnw_ref[...]).astype(BF16)
    k_all = _dot(ckvn, wuk_ref[...])
    vt_ref[0] = (_dot_nt(wvt_ref[...], ckvn) + vone_ref[:, 0:1]).astype(BF16)

    half = MLA_ROPE // 2
    ang_t = freq_ref[...] * pos_ref[...]
    cos_t = jnp.cos(ang_t)
    sin_t = jnp.sin(ang_t)
    cos = jnp.concatenate([cos_t, cos_t, jnp.ones((LANES - 2 * half, tm), F32)], axis=0).T
    sin = jnp.concatenate([sin_t, sin_t, jnp.zeros((LANES - 2 * half, tm), F32)], axis=0).T
    lane = lax.broadcasted_iota(I32, (tm, LANES), 1)
    qw_nope = qw_ref[:, 0:MLA_NOPE]
    qw_rope = qw_ref[:, MLA_NOPE:MLA_HEAD_PAD]
    kw_nope = kw_ref[:, 0:MLA_NOPE]
    kw_rope = kw_ref[:, MLA_NOPE:MLA_HEAD_PAD]
    kr_sq = kr * kr
    kr_rot = _rope(kr * kw_rope, cos, sin, lane)
    scale = (MLA_QK ** -0.5) * LOG2_E
    for h in range(MLA_HEADS):
        base = h * MLA_HEAD_PAD
        qn = q_all[:, base:base + MLA_NOPE]
        qr = q_all[:, base + MLA_NOPE:base + MLA_HEAD_PAD]
        ss = jnp.sum(qn * qn + qr * qr, axis=-1, keepdims=True)
        r = lax.rsqrt(ss / MLA_QK + EPS)
        q_ref[:, base:base + MLA_NOPE] = ((qn * r) * qw_nope * scale).astype(BF16)
        q_ref[:, base + MLA_NOPE:base + MLA_HEAD_PAD] = (
            _rope((qr * r) * qw_rope, cos, sin, lane) * scale).astype(BF16)
        kn = k_all[:, h * MLA_NOPE:(h + 1) * MLA_NOPE]
        ssk = jnp.sum(kn * kn + kr_sq, axis=-1, keepdims=True)
        rk = lax.rsqrt(ssk / MLA_QK + EPS)
        k_ref[:, base:base + MLA_NOPE] = ((kn * rk) * kw_nope).astype(BF16)
        k_ref[:, base + MLA_NOPE:base + MLA_HEAD_PAD] = (kr_rot * rk).astype(BF16)


def _flash_kernel(q_ref, k_ref, vt_ref, o_ref, s_ref, cm_ref, acc_ref, m_ref, *, t):
    seq = q_ref.shape[0]
    nq = seq // t
    krow = lax.broadcasted_iota(I32, (t, t), 0)
    qcol = lax.broadcasted_iota(I32, (t, t), 1)
    items = [(qi, j) for qi in range(nq) for j in range(qi + 1)]

    def scores(idx):
        qi, j = items[idx]
        s = _dot_nt(k_ref[j * t:(j + 1) * t, :], q_ref[qi * t:(qi + 1) * t, :])
        s_ref[idx % 2] = s
        cm_ref[idx % 2] = jnp.max(s, axis=0, keepdims=True)

    def update(idx):
        qi, j = items[idx]
        if j == 0:
            m_ref[...] = jnp.full_like(m_ref, -jnp.inf)
            acc_ref[...] = jnp.zeros_like(acc_ref)
        s = s_ref[idx % 2]
        if j == qi:
            s = jnp.where(krow <= qcol, s, -jnp.inf)
            cm = jnp.max(s, axis=0, keepdims=True)
        else:
            cm = cm_ref[idx % 2]
        m_prev = m_ref[...]
        m_new = jnp.maximum(m_prev, cm)
        alpha = jnp.exp2(m_prev - m_new)
        p = jnp.exp2(s - m_new).astype(BF16)
        m_ref[...] = m_new
        acc_ref[...] = alpha * acc_ref[...] + _dot(vt_ref[0, :, j * t:(j + 1) * t], p)
        if j == qi:
            acc = acc_ref[...]
            out_t = acc[0:MLA_V, :] * (1.0 / acc[MLA_V:MLA_V + 1, :])
            o_ref[qi * t:(qi + 1) * t, :] = out_t.T.astype(BF16)

    scores(0)
    for idx in range(len(items)):
        if idx + 1 < len(items):
            scores(idx + 1)
        update(idx)


def _flash(q, k, vt, batch, seq, t):
    rows = q.shape[0]
    return pl.pallas_call(
        functools.partial(_flash_kernel, t=t),
        grid=(batch, MLA_HEADS),
        in_specs=[
            pl.BlockSpec((seq, MLA_HEAD_PAD), lambda b, h: (b, h)),
            pl.BlockSpec((seq, MLA_HEAD_PAD), lambda b, h: (b, h)),
            pl.BlockSpec((1, FLASH_V_ROWS, seq), lambda b, h: (b, h, 0)),
        ],
        out_specs=pl.BlockSpec((seq, MLA_V), lambda b, h: (b, h)),
        out_shape=jax.ShapeDtypeStruct((rows, MLA_HEADS * MLA_V), BF16),
        scratch_shapes=[
            pltpu.VMEM((2, t, t), F32),
            pltpu.VMEM((2, 1, t), F32),
            pltpu.VMEM((FLASH_V_ROWS, t), F32),
            pltpu.VMEM((1, t), F32),
        ],
        compiler_params=_params("parallel", "parallel"),
        name="flash",
    )(q, k, vt)


def _merge_route_kernel(x_ref, oa_ref, ob_ref, gate_ref, wa_ref, wb_ref, wo_ref, nw_ref,
                        wr_hi_ref, wr_lo_ref, br_ref,
                        x1_ref, xn_ref, eid_ref, cw_ref, cnt_ref):
    @pl.when(pl.program_id(0) == 0)
    def _():
        cnt_ref[...] = jnp.zeros_like(cnt_ref)

    tm = x_ref.shape[0]
    ya = _dot(oa_ref[...], wa_ref[...])
    yb = _dot(ob_ref[...], wb_ref[...])
    ga = gate_ref[:, 0:D_MODEL].astype(F32)
    gb = gate_ref[:, D_MODEL:2 * D_MODEL].astype(F32)
    merged = jax.nn.sigmoid(ga) * ya + jax.nn.sigmoid(gb) * yb
    x1 = x_ref[...] + _dot(merged.astype(BF16), wo_ref[...])
    x1_ref[...] = x1
    ms = jnp.mean(x1 * x1, axis=-1, keepdims=True)
    xn = (x1 * lax.rsqrt(ms + EPS)) * nw_ref[...]
    xn_ref[...] = xn
    xn_hi, xn_lo = _split_bf16(xn)
    logits = (_dot(xn_hi, wr_hi_ref[...]) + _dot(xn_lo, wr_hi_ref[...])
              + _dot(xn_hi, wr_lo_ref[...])) + br_ref[...]
    lane = lax.broadcasted_iota(I32, (tm, LANES), 1)
    work = jnp.where(lane < N_EXPERTS, logits, -jnp.inf)
    vals, idxs = [], []
    for _ in range(TOP_K):
        m = jnp.max(work, axis=-1, keepdims=True)
        idx = jnp.min(jnp.where(work == m, lane, LANES), axis=-1, keepdims=True)
        vals.append(m)
        idxs.append(idx)
        work = jnp.where(lane == idx, -jnp.inf, work)
    exps = [jnp.exp(v - vals[0]) for v in vals]
    denom = exps[0] + exps[1] + exps[2] + exps[3]
    eid = jnp.zeros((tm, LANES), I32)
    cw = jnp.zeros((tm, LANES), F32)
    sel = jnp.zeros((tm, LANES), F32)
    for kk in range(TOP_K):
        eid = jnp.where(lane == kk, idxs[kk], eid)
        cw = jnp.where(lane == kk, exps[kk] / denom, cw)
        sel = sel + (lane == idxs[kk]).astype(F32)
    eid_ref[...] = eid
    cw_ref[...] = cw
    cnt_ref[0:1, :] = cnt_ref[0:1, :] + jnp.sum(sel, axis=0, keepdims=True)


def _merge_route(x2, oa, ob, gates, wa, wb, wo, nw, wr_hi, wr_lo, br, tm):
    t = x2.shape[0]
    const = lambda i: (0, 0)
    row = lambda i: (i, 0)
    return pl.pallas_call(
        _merge_route_kernel,
        grid=(t // tm,),
        in_specs=[
            pl.BlockSpec((tm, D_MODEL), row),
            pl.BlockSpec((tm, GLA_V_WIDTH), row),
            pl.BlockSpec((tm, MLA_HEADS * MLA_V), row),
            pl.BlockSpec((tm, _GATE_W), row),
            pl.BlockSpec((GLA_V_WIDTH, D_MODEL), const),
            pl.BlockSpec((MLA_HEADS * MLA_V, D_MODEL), const),
            pl.BlockSpec((D_MODEL, D_MODEL), const),
            pl.BlockSpec((1, D_MODEL), const),
            pl.BlockSpec((D_MODEL, LANES), const),
            pl.BlockSpec((D_MODEL, LANES), const),
            pl.BlockSpec((1, LANES), const),
        ],
        out_specs=[
            pl.BlockSpec((tm, D_MODEL), row),
            pl.BlockSpec((tm, D_MODEL), row),
            pl.BlockSpec((tm, LANES), row),
            pl.BlockSpec((tm, LANES), row),
            pl.BlockSpec((8, LANES), const),
        ],
        out_shape=[
            jax.ShapeDtypeStruct((t, D_MODEL), F32),
            jax.ShapeDtypeStruct((t, D_MODEL), F32),
            jax.ShapeDtypeStruct((t, LANES), I32),
            jax.ShapeDtypeStruct((t, LANES), F32),
            jax.ShapeDtypeStruct((8, LANES), F32),
        ],
        compiler_params=_params("arbitrary"),
        name="merge_route",
    )(x2, oa, ob, gates, wa, wb, wo, nw, wr_hi, wr_lo, br)


def _positions_kernel(eid_ref, cnt_ref, pos_ref, carry_ref, *, tile):
    @pl.when(pl.program_id(0) == 0)
    def _():
        carry_ref[...] = jnp.zeros_like(carry_ref)

    tb = eid_ref.shape[0]
    lane1 = lax.broadcasted_iota(I32, (1, LANES), 1)
    cnt = cnt_ref[0:1, :]
    padded = jnp.floor((cnt + (tile - 1)) / tile) * tile
    incl = padded
    shift = 1
    while shift < N_EXPERTS:
        incl = incl + jnp.where(lane1 >= shift, pltpu.roll(incl, shift, axis=1), 0.0)
        shift *= 2
    offs = incl - padded

    lane = lax.broadcasted_iota(I32, (tb, LANES), 1)
    eid = eid_ref[...]
    onehots = [lane == jnp.broadcast_to(eid[:, kk:kk + 1], (tb, LANES)) for kk in range(TOP_K)]
    sel = jnp.zeros((tb, LANES), F32)
    for oh in onehots:
        sel = sel + oh.astype(F32)
    row = lax.broadcasted_iota(I32, (tb, tb), 0)
    col = lax.broadcasted_iota(I32, (tb, tb), 1)
    strict = (row > col).astype(BF16)
    rank = _dot(strict, sel.astype(BF16)) + carry_ref[...] + offs
    pos = jnp.zeros((tb, LANES), I32)
    for kk in range(TOP_K):
        pk = jnp.sum(jnp.where(onehots[kk], rank, 0.0), axis=-1, keepdims=True)
        pos = jnp.where(lane == kk, pk.astype(I32), pos)
    pos_ref[...] = pos
    carry_ref[...] = carry_ref[...] + jnp.sum(sel, axis=0, keepdims=True)


def _positions(eid, cnt, tb, tile):
    t = eid.shape[0]
    return pl.pallas_call(
        functools.partial(_positions_kernel, tile=tile),
        grid=(t // tb,),
        in_specs=[
            pl.BlockSpec((tb, LANES), lambda i: (i, 0)),
            pl.BlockSpec((8, LANES), lambda i: (0, 0)),
        ],
        out_specs=pl.BlockSpec((tb, LANES), lambda i: (i, 0)),
        out_shape=jax.ShapeDtypeStruct((t, LANES), I32),
        scratch_shapes=[pltpu.VMEM((1, LANES), F32)],
        compiler_params=_params("arbitrary"),
        name="positions",
    )(eid, cnt)


ROW_DMA_UNROLL = 8


def _dispatch_kernel(tz_ref, pos_ref, xn_ref, xs_ref, zero_ref, sem, zsem, *, tile):
    i = pl.program_id(0)
    tb = xn_ref.shape[0]

    def zero_copy(e):
        start = pl.multiple_of(tz_ref[e], tile)
        return pltpu.make_async_copy(zero_ref, xs_ref.at[pl.ds(start, tile), 0], zsem)

    @pl.when(i == 0)
    def _():
        zero_ref[...] = jnp.zeros_like(zero_ref)
        for e in range(N_EXPERTS):
            @pl.when(tz_ref[e] >= 0)
            def _():
                zero_copy(e).start()
        for e in range(N_EXPERTS):
            @pl.when(tz_ref[e] >= 0)
            def _():
                zero_copy(e).wait()

    def issue(t, carry):
        for kk in range(TOP_K):
            p = pos_ref[0, 0, t * TOP_K + kk]
            pltpu.make_async_copy(xn_ref.at[pl.ds(t, 1)], xs_ref.at[p], sem).start(priority=kk % 2)
        return carry

    lax.fori_loop(0, tb, issue, 0, unroll=ROW_DMA_UNROLL)
    for _ in range(TOP_K):
        pltpu.make_async_copy(xn_ref, xs_ref.at[pl.ds(0, tb), 0], sem).wait()


def _dispatch(tile_zero, pos3, xn, n_rows, tb, tile):
    t = xn.shape[0]
    return pl.pallas_call(
        functools.partial(_dispatch_kernel, tile=tile),
        grid_spec=pltpu.PrefetchScalarGridSpec(
            num_scalar_prefetch=1,
            grid=(t // tb,),
            in_specs=[
                pl.BlockSpec((1, 1, tb * TOP_K), lambda i, tz: (i, 0, 0), memory_space=pltpu.SMEM),
                pl.BlockSpec((tb, D_MODEL), lambda i, tz: (i, 0)),
            ],
            out_specs=pl.BlockSpec(memory_space=pl.ANY),
            scratch_shapes=[pltpu.VMEM((tile, D_MODEL), F32), pltpu.SemaphoreType.DMA,
                            pltpu.SemaphoreType.DMA],
        ),
        out_shape=jax.ShapeDtypeStruct((n_rows, 1, D_MODEL), F32),
        compiler_params=_params("arbitrary"),
        name="dispatch",
    )(tile_zero, pos3, xn)


def _experts_kernel(te_ref, nv_ref, xs_ref, xsc_ref, wg_ref, bg_ref, wu_ref, bu_ref, wd_ref, bd_ref,
                    ys_ref,
                    wgb_ref, wub_ref, wdb_ref, wsc_ref, xbuf, ybuf, isem, osem, *, tile):
    i = pl.program_id(0)
    nv = nv_ref[0]
    valid = i < nv
    slot = i % 2

    def in_copies(step, s):
        r0 = pl.multiple_of(step * tile, tile)
        return [pltpu.make_async_copy(xs_ref.at[pl.ds(r0, tile), 0, pl.ds(c * LANES, LANES)],
                                      xbuf.at[s, :, pl.ds(c * LANES, LANES)], isem.at[s])
                for c in range(LANE_CHUNKS)]

    def out_copies(step, s):
        r0 = pl.multiple_of(step * tile, tile)
        return [pltpu.make_async_copy(ybuf.at[s, :, pl.ds(c * LANES, LANES)],
                                      ys_ref.at[pl.ds(r0, tile), 0, pl.ds(c * LANES, LANES)],
                                      osem.at[s])
                for c in range(LANE_CHUNKS)]

    @pl.when(i == 0)
    def _():
        for cp in in_copies(0, 0):
            cp.start()

    @pl.when(i + 1 < nv)
    def _():
        for cp in in_copies(i + 1, 1 - slot):
            cp.start()

    new_expert = jnp.logical_or(i == 0, te_ref[i] != te_ref[jnp.maximum(i - 1, 0)])

    @pl.when(jnp.logical_and(valid, new_expert))
    def _():
        for slot_w, (w_ref, w8_ref) in enumerate(((wg_ref, wgb_ref), (wu_ref, wub_ref),
                                                  (wd_ref, wdb_ref))):
            w = w_ref[0]
            sc = _pow2_scale(_abs_max(w))
            w8_ref[...] = (w * sc).astype(F8)
            wsc_ref[slot_w:slot_w + 1, :] = jnp.broadcast_to(1.0 / sc, (1, LANES))

    @pl.when(valid)
    def _():
        for cp in in_copies(i, slot):
            cp.wait()

        @pl.when(i >= 2)
        def _():
            for cp in out_copies(i - 2, slot):
                cp.wait()

        x = xbuf[slot]
        sx = xsc_ref[0:1, 0:1]
        x8 = (x * sx).astype(F8)
        inv_x = 1.0 / sx
        g = _dot(x8, wgb_ref[...]) * (inv_x * wsc_ref[0:1, 0:1]) + bg_ref[0]
        u = _dot(x8, wub_ref[...]) * (inv_x * wsc_ref[1:2, 0:1]) + bu_ref[0]
        g = jnp.minimum(g, SWIGLU_LIMIT)
        u = jnp.clip(u, -SWIGLU_LIMIT, SWIGLU_LIMIT)
        hidden = (u + 1.0) * g * jax.nn.sigmoid(SWIGLU_ALPHA * g)
        h8 = (hidden * HIDDEN_SCALE).astype(F8)
        ybuf[slot] = (_dot(h8, wdb_ref[...]) * (wsc_ref[2:3, 0:1] * (1.0 / HIDDEN_SCALE))
                      + bd_ref[0])
        for cp in out_copies(i, slot):
            cp.start()

    @pl.when(i == nv - 1)
    def _():
        for cp in out_copies(i, slot):
            cp.wait()

        @pl.when(i >= 1)
        def _():
            for cp in out_copies(i - 1, 1 - slot):
                cp.wait()


def _experts(tile_expert, n_valid, xs, x_scale, wg, bg, wu, bu, wd, bd, tile):
    rows = xs.shape[0]
    nt = rows // tile
    wmap = lambda i, te, nv: (te[i], 0, 0)
    return pl.pallas_call(
        functools.partial(_experts_kernel, tile=tile),
        grid_spec=pltpu.PrefetchScalarGridSpec(
            num_scalar_prefetch=2,
            grid=(nt,),
            in_specs=[
                pl.BlockSpec(memory_space=pl.ANY),
                pl.BlockSpec((1, LANES), lambda i, te, nv: (0, 0)),
                pl.BlockSpec((1, D_MODEL, D_FF), wmap),
                pl.BlockSpec((1, 1, D_FF), wmap),
                pl.BlockSpec((1, D_MODEL, D_FF), wmap),
                pl.BlockSpec((1, 1, D_FF), wmap),
                pl.BlockSpec((1, D_FF, D_MODEL), wmap),
                pl.BlockSpec((1, 1, D_MODEL), wmap),
            ],
            out_specs=pl.BlockSpec(memory_space=pl.ANY),
            scratch_shapes=[pltpu.VMEM((D_MODEL, D_FF), F8), pltpu.VMEM((D_MODEL, D_FF), F8),
                            pltpu.VMEM((D_FF, D_MODEL), F8), pltpu.VMEM((8, LANES), F32),
                            pltpu.VMEM((2, tile, D_MODEL), F32), pltpu.VMEM((2, tile, D_MODEL), F32),
                            pltpu.SemaphoreType.DMA((2,)), pltpu.SemaphoreType.DMA((2,))],
        ),
        out_shape=jax.ShapeDtypeStruct((rows, 1, D_MODEL), F32),
        compiler_params=_params("arbitrary"),
        name="experts",
    )(tile_expert, n_valid, xs, x_scale, wg, bg, wu, bu, wd, bd)


def _combine_kernel(pos_ref, posn_ref, x1_ref, cw_ref, ys_ref, o_ref, buf_ref, sem0, sem1):
    i = pl.program_id(0)
    n = pl.num_programs(0)
    tb = x1_ref.shape[0] // 2
    def issue(p_ref, sub, slot, sem):
        def body(t, carry):
            for kk in range(TOP_K):
                p = p_ref[0, 0, (sub * tb + t) * TOP_K + kk]
                pltpu.make_async_copy(ys_ref.at[p], buf_ref.at[slot, kk, pl.ds(t, 1)],
                                      sem).start(priority=kk % 2)
            return carry
        lax.fori_loop(0, tb, body, 0, unroll=ROW_DMA_UNROLL)

    def drain(slot, sem):
        for kk in range(TOP_K):
            pltpu.make_async_copy(ys_ref.at[pl.ds(0, tb), 0], buf_ref.at[slot, kk], sem).wait()

    def reduce(sub, slot):
        rows = slice(sub * tb, (sub + 1) * tb)
        cw = cw_ref[rows, :]
        acc = x1_ref[rows, :]
        for kk in range(TOP_K):
            acc = acc + cw[:, kk:kk + 1] * buf_ref[slot, kk]
        o_ref[rows, :] = acc

    @pl.when(i == 0)
    def _():
        issue(pos_ref, 0, 0, sem0)

    issue(pos_ref, 1, 1, sem1)
    drain(0, sem0)
    reduce(0, 0)

    @pl.when(i + 1 < n)
    def _():
        issue(posn_ref, 0, 0, sem0)

    drain(1, sem1)
    reduce(1, 1)


def _combine(pos3, x1, cw, ys, tb):
    t = x1.shape[0]
    n = t // (2 * tb)
    return pl.pallas_call(
        _combine_kernel,
        grid=(n,),
        in_specs=[
            pl.BlockSpec((1, 1, 2 * tb * TOP_K), lambda i: (i, 0, 0), memory_space=pltpu.SMEM),
            pl.BlockSpec((1, 1, 2 * tb * TOP_K), lambda i: (jnp.minimum(i + 1, n - 1), 0, 0),
                         memory_space=pltpu.SMEM),
            pl.BlockSpec((2 * tb, D_MODEL), lambda i: (i, 0)),
            pl.BlockSpec((2 * tb, LANES), lambda i: (i, 0)),
            pl.BlockSpec(memory_space=pl.ANY),
        ],
        out_specs=pl.BlockSpec((2 * tb, D_MODEL), lambda i: (i, 0)),
        out_shape=jax.ShapeDtypeStruct((t, D_MODEL), F32),
        scratch_shapes=[pltpu.VMEM((2, TOP_K, tb, D_MODEL), F32), pltpu.SemaphoreType.DMA,
                        pltpu.SemaphoreType.DMA],
        compiler_params=_params("arbitrary"),
        name="combine",
    )(pos3, pos3, x1, cw, ys)


def _pad_cols(a, width):
    return jnp.pad(a, ((0, 0), (0, width - a.shape[1])))


def _pack_in_proj(w_in):
    o = np.cumsum((GLA_QK_WIDTH, GLA_QK_WIDTH, GLA_V_WIDTH, GLA_V_WIDTH, GLA_GATE_RANK,
                   MLA_Q_RANK, MLA_KV_RANK, MLA_ROPE, D_MODEL, D_MODEL)).tolist()
    gla = w_in[:, 0:o[3]]
    lr = _pad_cols(w_in[:, o[3]:o[4]], _LR_W)
    mla = _pad_cols(w_in[:, o[4]:o[7]], _MLA_W)
    gates = w_in[:, o[7]:o[9]]
    return jnp.concatenate([gla, lr, mla, gates], axis=1).astype(BF16)


def _pad_heads(w, head_w):
    r = w.shape[0]
    w3 = w.reshape(r, MLA_HEADS, head_w)
    w3 = jnp.pad(w3, ((0, 0), (0, 0), (0, MLA_HEAD_PAD - head_w)))
    return w3.reshape(r, MLA_HEADS * MLA_HEAD_PAD)


def _tile_rows(n, pref):
    return pref if n % pref == 0 else n


def kernel(x, positions, attn_norm_w, w_in, w_gla_gk, b_gla_gk, gla_out_norm_w, w_gla_out,
           mla_q_norm_w, w_mla_uq, mla_kv_norm_w, w_mla_ukv, mla_qk_q_norm_w, mla_qk_k_norm_w,
           w_mla_out, w_out, moe_norm_w, w_router, b_router, w_exp_gate, b_exp_gate,
           w_exp_up, b_exp_up, w_exp_down, b_exp_down):
    batch, seq, _ = x.shape
    depth = w_in.shape[0]
    t = batch * seq
    x2 = x.reshape(t, D_MODEL)
    pos = positions.reshape(1, t).astype(F32)
    half = MLA_ROPE // 2
    freq = (ROPE_BASE ** (-jnp.arange(half, dtype=F32) / half))[:, None]

    tm = _tile_rows(t, TOKEN_TILE)
    seq_tile = _tile_rows(seq, TOKEN_TILE)
    moe_tile = TOKEN_TILE
    n_rows = ((t * TOP_K) // moe_tile + N_EXPERTS) * moe_tile

    for l in range(depth):
        w_pack = _pack_in_proj(w_in[l])
        wgk = jnp.pad(w_gla_gk[l], ((0, LANES - GLA_GATE_RANK), (0, 0)))
        wgk_hi, wgk_lo = _split_bf16(wgk)
        wuq = _pad_heads(w_mla_uq[l], MLA_QK).astype(BF16)
        wukv3 = w_mla_ukv[l].reshape(MLA_KV_RANK, MLA_HEADS, MLA_NOPE + MLA_V)
        wuk = wukv3[:, :, :MLA_NOPE].reshape(MLA_KV_RANK, MLA_HEADS * MLA_NOPE).astype(BF16)
        wvt = jnp.pad(wukv3[:, :, MLA_NOPE:].transpose(1, 2, 0),
                      ((0, 0), (0, FLASH_V_ROWS - MLA_V), (0, 0)))
        wvt = wvt.reshape(MLA_HEADS * FLASH_V_ROWS, MLA_KV_RANK).astype(BF16)
        vone = (jnp.arange(MLA_HEADS * FLASH_V_ROWS) % FLASH_V_ROWS == MLA_V).astype(F32)[:, None]
        qw = _pad_cols(mla_qk_q_norm_w[l][None, :], MLA_HEAD_PAD)
        kw = _pad_cols(mla_qk_k_norm_w[l][None, :], MLA_HEAD_PAD)
        o_a, gates, q, k, vt = _inproj(
            x2, attn_norm_w[l][None, :], w_pack, wgk_hi, wgk_lo, b_gla_gk[l][None, :], pos,
            wuq, wuk, wvt, vone, mla_q_norm_w[l][None, :], mla_kv_norm_w[l][None, :], qw, kw, freq,
            gla_out_norm_w[l][None, :], batch, seq, seq_tile)
        o_b = _flash(q, k, vt, batch, seq, _tile_rows(seq, FLASH_TILE))

        wr = _pad_cols(w_router[l], LANES)
        wr_hi, wr_lo = _split_bf16(wr)
        br = _pad_cols(b_router[l][None, :], LANES)
        x1, xn, eid, cw, cnt = _merge_route(
            x2, o_a, o_b, gates, w_gla_out[l].astype(BF16), w_mla_out[l].astype(BF16),
            w_out[l].astype(BF16), moe_norm_w[l][None, :], wr_hi, wr_lo, br, tm)

        pos3 = _positions(eid, cnt, tm, moe_tile)[:, :TOP_K].reshape(t // tm, 1, tm * TOP_K)
        counts = cnt[0, :N_EXPERTS].astype(I32)
        padded = ((counts + moe_tile - 1) // moe_tile) * moe_tile
        ends = jnp.cumsum(padded)
        n_tiles = n_rows // moe_tile
        n_valid = (ends[-1] // moe_tile).astype(I32)
        starts = jnp.arange(n_tiles, dtype=I32) * moe_tile
        starts = jnp.minimum(starts, ends[-1] - moe_tile)
        tile_expert = jnp.minimum(jnp.sum((starts[:, None] >= ends[None, :]).astype(I32), axis=1),
                                  N_EXPERTS - 1)
        tile_zero = jnp.where(padded > 0, ends - moe_tile, -1).astype(I32)

        xs = _dispatch(tile_zero, pos3, xn, n_rows, tm, moe_tile)
        x_bound = (D_MODEL ** 0.5) * jnp.max(jnp.abs(moe_norm_w[l]))
        x_scale = jnp.broadcast_to(_pow2_scale(x_bound), (1, LANES))
        ys = _experts(tile_expert, n_valid.reshape(1), xs, x_scale,
                      w_exp_gate[l], b_exp_gate[l][:, None, :],
                      w_exp_up[l], b_exp_up[l][:, None, :],
                      w_exp_down[l], b_exp_down[l][:, None, :], moe_tile)
        x2 = _combine(pos3, x1, cw, ys, tm // 2)
    return x2.reshape(batch, seq, D_MODEL)
```

```python
import functools

import jax
import jax.numpy as jnp
import numpy as np
from jax import lax
from jax.experimental import pallas as pl
from jax.experimental.pallas import tpu as pltpu

F32 = jnp.float32
BF16 = jnp.bfloat16
I32 = jnp.int32
F8 = jnp.float8_e4m3fn
F8_TARGET = 224.0
HIDDEN_SCALE = 4.0

D_MODEL = 1024
EPS = 1e-6
GLA_HEADS = 4
GLA_DK = 128
GLA_DV = 256
GLA_GATE_RANK = 16
GLA_GATE_NORMALIZER = 16.0
GLA_QK_WIDTH = GLA_HEADS * GLA_DK
GLA_V_WIDTH = GLA_HEADS * GLA_DV
MLA_HEADS = 8
MLA_Q_RANK = 384
MLA_KV_RANK = 256
MLA_NOPE = 128
MLA_ROPE = 64
MLA_V = 128
MLA_QK = MLA_NOPE + MLA_ROPE
ROPE_BASE = 10000.0
N_EXPERTS = 32
TOP_K = 4
D_FF = 1024
SWIGLU_LIMIT = 7.0
SWIGLU_ALPHA = 1.702

LANES = 128
MLA_HEAD_PAD = 256
GLA_CHUNK = 128
TOKEN_TILE = 512
FLASH_TILE = 1024
FLASH_V_ROWS = 144
LOG2_E = 1.4426950408889634
VMEM_LIMIT = 56 * 1024 * 1024

_GLA_W = 2 * GLA_QK_WIDTH + 2 * GLA_V_WIDTH
_LR_W = LANES
_MLA_W = 768
_GATE_W = 2 * D_MODEL


def _dot(a, b):
    return jnp.dot(a, b, preferred_element_type=F32)


def _dot_nt(a, b):
    return lax.dot_general(a, b, (((1,), (1,)), ((), ())), preferred_element_type=F32)


def _dot_tn(a, b):
    return lax.dot_general(a, b, (((0,), (0,)), ((), ())), preferred_element_type=F32)


def _abs_max(a):
    return jnp.max(jnp.max(jnp.abs(a), axis=-1, keepdims=True), axis=0, keepdims=True)


def _pow2_scale(amax):
    return jnp.exp2(jnp.floor(jnp.log2(F8_TARGET / jnp.maximum(amax, 1e-30))))


def _split_bf16(a):
    hi = a.astype(BF16)
    lo = (a - hi.astype(F32)).astype(BF16)
    return hi, lo


def _params(*sem):
    return pltpu.CompilerParams(dimension_semantics=sem, vmem_limit_bytes=VMEM_LIMIT)


LANE_CHUNKS = D_MODEL // LANES


def _inproj_kernel(x_ref, nw_ref, wgla_ref, wlr_ref, wmla_ref, wgate_ref, wgk_hi_ref, wgk_lo_ref,
                   bgk_ref, pos_ref,
                   wuq_ref, wuk_ref, wvt_ref, vone_ref, qnw_ref, kvnw_ref, qw_ref, kw_ref, freq_ref,
                   gnw_ref, oa_ref, gate_ref, q_ref, k_ref, vt_ref, state_ref, *, steps_per_seq):
    @pl.when(pl.program_id(0) % steps_per_seq == 0)
    def _():
        state_ref[...] = jnp.zeros_like(state_ref)

    x = x_ref[...]
    ms = jnp.mean(x * x, axis=-1, keepdims=True)
    h = (x * lax.rsqrt(ms + EPS)) * nw_ref[...]
    hb = h.astype(BF16)
    mla = _dot(hb, wmla_ref[...])
    _mla_prep(mla, pos_ref, wuq_ref, wuk_ref, wvt_ref, vone_ref, qnw_ref, kvnw_ref, qw_ref, kw_ref,
              freq_ref, q_ref, k_ref, vt_ref)
    gq = (_dot(hb, wgla_ref[:, 0:GLA_QK_WIDTH]) * (GLA_DK ** -0.5)).astype(BF16)
    gk = _dot(hb, wgla_ref[:, GLA_QK_WIDTH:2 * GLA_QK_WIDTH]).astype(BF16)
    gv = _dot(hb, wgla_ref[:, 2 * GLA_QK_WIDTH:2 * GLA_QK_WIDTH + GLA_V_WIDTH]).astype(BF16)
    gg = _dot(hb, wgla_ref[:, 2 * GLA_QK_WIDTH + GLA_V_WIDTH:_GLA_W])
    lr = _dot(hb, wlr_ref[...])
    lr_hi, lr_lo = _split_bf16(lr)
    a_logit = (_dot(lr_hi, wgk_hi_ref[...]) + _dot(lr_lo, wgk_hi_ref[...])
               + _dot(lr_hi, wgk_lo_ref[...])) + bgk_ref[...]
    log_sig = jnp.minimum(a_logit, 0.0) - jnp.log1p(jnp.exp(-jnp.abs(a_logit)))
    _gla_chunks(gq, gk, gv, gg, log_sig / GLA_GATE_NORMALIZER, gnw_ref, oa_ref, state_ref)
    gate_ref[...] = _dot(hb, wgate_ref[...]).astype(BF16)


def _inproj(x2, nw, w_parts, wgk_hi, wgk_lo, bgk, pos, wuq, wuk, wvt, vone, qnw, kvnw, qw, kw, freq,
            gnw, batch, seq, tm):
    t = x2.shape[0]
    nblk = seq // tm
    const = lambda i: (0, 0)
    row = lambda i: (i, 0)
    hw = MLA_HEADS * MLA_HEAD_PAD
    return pl.pallas_call(
        functools.partial(_inproj_kernel, steps_per_seq=nblk),
        grid=(t // tm,),
        in_specs=[
            pl.BlockSpec((tm, D_MODEL), row),
            pl.BlockSpec((1, D_MODEL), const),
            pl.BlockSpec((D_MODEL, _GLA_W), const, pipeline_mode=pl.Buffered(1)),
            pl.BlockSpec((D_MODEL, _LR_W), const, pipeline_mode=pl.Buffered(1)),
            pl.BlockSpec((D_MODEL, _MLA_W), const, pipeline_mode=pl.Buffered(1)),
            pl.BlockSpec((D_MODEL, _GATE_W), const, pipeline_mode=pl.Buffered(1)),
            pl.BlockSpec((LANES, GLA_QK_WIDTH), const),
            pl.BlockSpec((LANES, GLA_QK_WIDTH), const),
            pl.BlockSpec((1, GLA_QK_WIDTH), const),
            pl.BlockSpec((1, tm), lambda i: (0, i)),
            pl.BlockSpec((MLA_Q_RANK, hw), const),
            pl.BlockSpec((MLA_KV_RANK, MLA_HEADS * MLA_NOPE), const),
            pl.BlockSpec((MLA_HEADS * FLASH_V_ROWS, MLA_KV_RANK), const),
            pl.BlockSpec((MLA_HEADS * FLASH_V_ROWS, 1), const),
            pl.BlockSpec((1, MLA_Q_RANK), const),
            pl.BlockSpec((1, MLA_KV_RANK), const),
            pl.BlockSpec((1, MLA_HEAD_PAD), const),
            pl.BlockSpec((1, MLA_HEAD_PAD), const),
            pl.BlockSpec((MLA_ROPE // 2, 1), const),
            pl.BlockSpec((1, GLA_DV), const),
        ],
        out_specs=[
            pl.BlockSpec((tm, GLA_V_WIDTH), row),
            pl.BlockSpec((tm, _GATE_W), row),
            pl.BlockSpec((tm, hw), row),
            pl.BlockSpec((tm, hw), row),
            pl.BlockSpec((1, MLA_HEADS * FLASH_V_ROWS, tm), lambda i: (i // nblk, 0, i % nblk)),
        ],
        out_shape=[
            jax.ShapeDtypeStruct((t, GLA_V_WIDTH), BF16),
            jax.ShapeDtypeStruct((t, _GATE_W), BF16),
            jax.ShapeDtypeStruct((t, hw), BF16),
            jax.ShapeDtypeStruct((t, hw), BF16),
            jax.ShapeDtypeStruct((batch, MLA_HEADS * FLASH_V_ROWS, seq), BF16),
        ],
        scratch_shapes=[pltpu.VMEM((GLA_HEADS, GLA_DV, GLA_DK), F32)],
        compiler_params=_params("arbitrary"),
        name="inproj",
    )(x2, nw, *w_parts, wgk_hi, wgk_lo, bgk, pos, wuq, wuk, wvt, vone, qnw, kvnw, qw, kw, freq, gnw)


def _gla_chunks(q_all, k_all, v_all, g_all, la_all, nw_ref, o_ref, state_ref):
    c = GLA_CHUNK
    row = lax.broadcasted_iota(I32, (c, c), 0)
    col = lax.broadcasted_iota(I32, (c, c), 1)
    causal = row >= col
    tri = causal.astype(BF16)
    mid = c // 2 - 1
    for s in range(q_all.shape[0] // c):
        r0 = s * c
        la = la_all[r0:r0 + c, :]
        la_hi, la_lo = _split_bf16(la)
        cum_all = _dot(tri, la_hi) + _dot(tri, la_lo)
        for h in range(GLA_HEADS):
            ks = slice(h * GLA_DK, (h + 1) * GLA_DK)
            vs = slice(h * GLA_DV, (h + 1) * GLA_DV)
            cum = cum_all[:, ks]
            q = q_all[r0:r0 + c, ks].astype(F32)
            k = k_all[r0:r0 + c, ks].astype(F32)
            v = v_all[r0:r0 + c, vs]
            ref_row = cum[mid:mid + 1, :]
            last = cum[c - 1:c, :]
            qg = (q * jnp.exp(cum - ref_row)).astype(BF16)
            kg = (k * jnp.exp(ref_row - cum)).astype(BF16)
            scores = jnp.where(causal, _dot_nt(qg, kg), 0.0).astype(BF16)
            o = _dot(scores, v)
            st = state_ref[h]
            qe = (q * jnp.exp(cum)).astype(BF16)
            o = o + _dot_nt(qe, st.astype(BF16))
            ko = (k * jnp.exp(last - cum)).astype(BF16)
            state_ref[h] = st * jnp.exp(last) + _dot_tn(v, ko)
            ms = jnp.mean(o * o, axis=-1, keepdims=True)
            on = (o * lax.rsqrt(ms + EPS)) * nw_ref[...]
            g = g_all[r0:r0 + c, vs]
            o_ref[r0:r0 + c, vs] = (on * (g * jax.nn.sigmoid(g))).astype(BF16)


def _rope(x, cos, sin, lane):
    half = MLA_ROPE // 2
    rot = jnp.where(lane < half, -pltpu.roll(x, LANES - half, axis=1), pltpu.roll(x, half, axis=1))
    return x * cos + rot * sin


def _mla_prep(mla, pos_ref, wuq_ref, wuk_ref, wvt_ref, vone_ref, qnw_ref, kvnw_ref,
              qw_ref, kw_ref, freq_ref, q_ref, k_ref, vt_ref):
    tm = mla.shape[0]
    cq = mla[:, 0:MLA_Q_RANK]
    ckv = mla[:, MLA_Q_RANK:MLA_Q_RANK + MLA_KV_RANK]
    kr = mla[:, MLA_Q_RANK + MLA_KV_RANK:_MLA_W]

    def rms(a, w):
        ms = jnp.mean(a * a, axis=-1, keepdims=True)
        return (a * lax.rsqrt(ms + EPS)) * w

    q_all = _dot(rms(cq, qnw_ref[...]).astype(BF16), wuq_ref[...])
    ckvn = rms(ckv, kvnw_ref[...]).astype(BF16)
    k_all = _dot(ckvn, wuk_ref[...])
    vt_ref[0] = (_dot_nt(wvt_ref[...], ckvn) + vone_ref[:, 0:1]).astype(BF16)

    half = MLA_ROPE // 2
    ang_t = freq_ref[...] * pos_ref[...]
    cos_t = jnp.cos(ang_t)
    sin_t = jnp.sin(ang_t)
    cos = jnp.concatenate([cos_t, cos_t, jnp.ones((LANES - 2 * half, tm), F32)], axis=0).T
    sin = jnp.concatenate([sin_t, sin_t, jnp.zeros((LANES - 2 * half, tm), F32)], axis=0).T
    lane = lax.broadcasted_iota(I32, (tm, LANES), 1)
    qw_nope = qw_ref[:, 0:MLA_NOPE]
    qw_rope = qw_ref[:, MLA_NOPE:MLA_HEAD_PAD]
    kw_nope = kw_ref[:, 0:MLA_NOPE]
    kw_rope = kw_ref[:, MLA_NOPE:MLA_HEAD_PAD]
    kr_sq = kr * kr
    kr_rot = _rope(kr * kw_rope, cos, sin, lane)
    scale = (MLA_QK ** -0.5) * LOG2_E
    for h in range(MLA_HEADS):
        base = h * MLA_HEAD_PAD
        qn = q_all[:, base:base + MLA_NOPE]
        qr = q_all[:, base + MLA_NOPE:base + MLA_HEAD_PAD]
        ss = jnp.sum(qn * qn + qr * qr, axis=-1, keepdims=True)
        r = lax.rsqrt(ss / MLA_QK + EPS)
        q_ref[:, base:base + MLA_NOPE] = ((qn * r) * qw_nope * scale).astype(BF16)
        q_ref[:, base + MLA_NOPE:base + MLA_HEAD_PAD] = (
            _rope((qr * r) * qw_rope, cos, sin, lane) * scale).astype(BF16)
        kn = k_all[:, h * MLA_NOPE:(h + 1) * MLA_NOPE]
        ssk = jnp.sum(kn * kn + kr_sq, axis=-1, keepdims=True)
        rk = lax.rsqrt(ssk / MLA_QK + EPS)
        k_ref[:, base:base + MLA_NOPE] = ((kn * rk) * kw_nope).astype(BF16)
        k_ref[:, base + MLA_NOPE:base + MLA_HEAD_PAD] = (kr_rot * rk).astype(BF16)


def _flash_kernel(q_ref, k_ref, vt_ref, o_ref, s_ref, cm_ref, acc_ref, m_ref, *, t):
    seq = q_ref.shape[0]
    nq = seq // t
    krow = lax.broadcasted_iota(I32, (t, t), 0)
    qcol = lax.broadcasted_iota(I32, (t, t), 1)
    items = [(qi, j) for qi in range(nq) for j in range(qi + 1)]

    def scores(idx):
        qi, j = items[idx]
        s = _dot_nt(k_ref[j * t:(j + 1) * t, :], q_ref[qi * t:(qi + 1) * t, :])
        s_ref[idx % 2] = s
        cm_ref[idx % 2] = jnp.max(s, axis=0, keepdims=True)

    def update(idx):
        qi, j = items[idx]
        if j == 0:
            m_ref[...] = jnp.full_like(m_ref, -jnp.inf)
            acc_ref[...] = jnp.zeros_like(acc_ref)
        s = s_ref[idx % 2]
        if j == qi:
            s = jnp.where(krow <= qcol, s, -jnp.inf)
            cm = jnp.max(s, axis=0, keepdims=True)
        else:
            cm = cm_ref[idx % 2]
        m_prev = m_ref[...]
        m_new = jnp.maximum(m_prev, cm)
        alpha = jnp.exp2(m_prev - m_new)
        p = jnp.exp2(s - m_new).astype(BF16)
        m_ref[...] = m_new
        acc_ref[...] = alpha * acc_ref[...] + _dot(vt_ref[0, :, j * t:(j + 1) * t], p)
        if j == qi:
            acc = acc_ref[...]
            out_t = acc[0:MLA_V, :] * (1.0 / acc[MLA_V:MLA_V + 1, :])
            o_ref[qi * t:(qi + 1) * t, :] = out_t.T.astype(BF16)

    scores(0)
    for idx in range(len(items)):
        if idx + 1 < len(items):
            scores(idx + 1)
        update(idx)


def _flash(q, k, vt, batch, seq, t):
    rows = q.shape[0]
    return pl.pallas_call(
        functools.partial(_flash_kernel, t=t),
        grid=(batch, MLA_HEADS),
        in_specs=[
            pl.BlockSpec((seq, MLA_HEAD_PAD), lambda b, h: (b, h)),
            pl.BlockSpec((seq, MLA_HEAD_PAD), lambda b, h: (b, h)),
            pl.BlockSpec((1, FLASH_V_ROWS, seq), lambda b, h: (b, h, 0)),
        ],
        out_specs=pl.BlockSpec((seq, MLA_V), lambda b, h: (b, h)),
        out_shape=jax.ShapeDtypeStruct((rows, MLA_HEADS * MLA_V), BF16),
        scratch_shapes=[
            pltpu.VMEM((2, t, t), F32),
            pltpu.VMEM((2, 1, t), F32),
            pltpu.VMEM((FLASH_V_ROWS, t), F32),
            pltpu.VMEM((1, t), F32),
        ],
        compiler_params=_params("parallel", "parallel"),
        name="flash",
    )(q, k, vt)


def _merge_route_kernel(x_ref, oa_ref, ob_ref, gate_ref, wa_ref, wb_ref, wo_ref, nw_ref,
                        wr_hi_ref, wr_lo_ref, br_ref,
                        x1_ref, xn_ref, eid_ref, cw_ref, cnt_ref):
    @pl.when(pl.program_id(0) == 0)
    def _():
        cnt_ref[...] = jnp.zeros_like(cnt_ref)

    tm = x_ref.shape[0]
    ya = _dot(oa_ref[...], wa_ref[...])
    yb = _dot(ob_ref[...], wb_ref[...])
    ga = gate_ref[:, 0:D_MODEL].astype(F32)
    gb = gate_ref[:, D_MODEL:2 * D_MODEL].astype(F32)
    merged = jax.nn.sigmoid(ga) * ya + jax.nn.sigmoid(gb) * yb
    x1 = x_ref[...] + _dot(merged.astype(BF16), wo_ref[...])
    x1_ref[...] = x1
    ms = jnp.mean(x1 * x1, axis=-1, keepdims=True)
    xn = (x1 * lax.rsqrt(ms + EPS)) * nw_ref[...]
    xn_ref[...] = xn
    xn_hi, xn_lo = _split_bf16(xn)
    logits = (_dot(xn_hi, wr_hi_ref[...]) + _dot(xn_lo, wr_hi_ref[...])
              + _dot(xn_hi, wr_lo_ref[...])) + br_ref[...]
    lane = lax.broadcasted_iota(I32, (tm, LANES), 1)
    work = jnp.where(lane < N_EXPERTS, logits, -jnp.inf)
    vals, idxs = [], []
    for _ in range(TOP_K):
        m = jnp.max(work, axis=-1, keepdims=True)
        idx = jnp.min(jnp.where(work == m, lane, LANES), axis=-1, keepdims=True)
        vals.append(m)
        idxs.append(idx)
        work = jnp.where(lane == idx, -jnp.inf, work)
    exps = [jnp.exp(v - vals[0]) for v in vals]
    denom = exps[0] + exps[1] + exps[2] + exps[3]
    eid = jnp.zeros((tm, LANES), I32)
    cw = jnp.zeros((tm, LANES), F32)
    sel = jnp.zeros((tm, LANES), F32)
    for kk in range(TOP_K):
        eid = jnp.where(lane == kk, idxs[kk], eid)
        cw = jnp.where(lane == kk, exps[kk] / denom, cw)
        sel = sel + (lane == idxs[kk]).astype(F32)
    eid_ref[...] = eid
    cw_ref[...] = cw
    cnt_ref[0:1, :] = cnt_ref[0:1, :] + jnp.sum(sel, axis=0, keepdims=True)


def _merge_route(x2, oa, ob, gates, wa, wb, wo, nw, wr_hi, wr_lo, br, tm):
    t = x2.shape[0]
    const = lambda i: (0, 0)
    row = lambda i: (i, 0)
    return pl.pallas_call(
        _merge_route_kernel,
        grid=(t // tm,),
        in_specs=[
            pl.BlockSpec((tm, D_MODEL), row),
            pl.BlockSpec((tm, GLA_V_WIDTH), row),
            pl.BlockSpec((tm, MLA_HEADS * MLA_V), row),
            pl.BlockSpec((tm, _GATE_W), row),
            pl.BlockSpec((GLA_V_WIDTH, D_MODEL), const),
            pl.BlockSpec((MLA_HEADS * MLA_V, D_MODEL), const),
            pl.BlockSpec((D_MODEL, D_MODEL), const),
            pl.BlockSpec((1, D_MODEL), const),
            pl.BlockSpec((D_MODEL, LANES), const),
            pl.BlockSpec((D_MODEL, LANES), const),
            pl.BlockSpec((1, LANES), const),
        ],
        out_specs=[
            pl.BlockSpec((tm, D_MODEL), row),
            pl.BlockSpec((tm, D_MODEL), row),
            pl.BlockSpec((tm, LANES), row),
            pl.BlockSpec((tm, LANES), row),
            pl.BlockSpec((8, LANES), const),
        ],
        out_shape=[
            jax.ShapeDtypeStruct((t, D_MODEL), F32),
            jax.ShapeDtypeStruct((t, D_MODEL), F32),
            jax.ShapeDtypeStruct((t, LANES), I32),
            jax.ShapeDtypeStruct((t, LANES), F32),
            jax.ShapeDtypeStruct((8, LANES), F32),
        ],
        compiler_params=_params("arbitrary"),
        name="merge_route",
    )(x2, oa, ob, gates, wa, wb, wo, nw, wr_hi, wr_lo, br)


def _positions_kernel(eid_ref, cnt_ref, pos_ref, carry_ref, *, tile):
    @pl.when(pl.program_id(0) == 0)
    def _():
        carry_ref[...] = jnp.zeros_like(carry_ref)

    tb = eid_ref.shape[0]
    lane1 = lax.broadcasted_iota(I32, (1, LANES), 1)
    cnt = cnt_ref[0:1, :]
    padded = jnp.floor((cnt + (tile - 1)) / tile) * tile
    incl = padded
    shift = 1
    while shift < N_EXPERTS:
        incl = incl + jnp.where(lane1 >= shift, pltpu.roll(incl, shift, axis=1), 0.0)
        shift *= 2
    offs = incl - padded

    lane = lax.broadcasted_iota(I32, (tb, LANES), 1)
    eid = eid_ref[...]
    onehots = [lane == jnp.broadcast_to(eid[:, kk:kk + 1], (tb, LANES)) for kk in range(TOP_K)]
    sel = jnp.zeros((tb, LANES), F32)
    for oh in onehots:
        sel = sel + oh.astype(F32)
    row = lax.broadcasted_iota(I32, (tb, tb), 0)
    col = lax.broadcasted_iota(I32, (tb, tb), 1)
    strict = (row > col).astype(BF16)
    rank = _dot(strict, sel.astype(BF16)) + carry_ref[...] + offs
    pos = jnp.zeros((tb, LANES), I32)
    for kk in range(TOP_K):
        pk = jnp.sum(jnp.where(onehots[kk], rank, 0.0), axis=-1, keepdims=True)
        pos = jnp.where(lane == kk, pk.astype(I32), pos)
    pos_ref[...] = pos
    carry_ref[...] = carry_ref[...] + jnp.sum(sel, axis=0, keepdims=True)


def _positions(eid, cnt, tb, tile):
    t = eid.shape[0]
    return pl.pallas_call(
        functools.partial(_positions_kernel, tile=tile),
        grid=(t // tb,),
        in_specs=[
            pl.BlockSpec((tb, LANES), lambda i: (i, 0)),
            pl.BlockSpec((8, LANES), lambda i: (0, 0)),
        ],
        out_specs=pl.BlockSpec((tb, LANES), lambda i: (i, 0)),
        out_shape=jax.ShapeDtypeStruct((t, LANES), I32),
        scratch_shapes=[pltpu.VMEM((1, LANES), F32)],
        compiler_params=_params("arbitrary"),
        name="positions",
    )(eid, cnt)


ROW_DMA_UNROLL = 8


def _dispatch_kernel(tz_ref, pos_ref, xn_ref, xs_ref, zero_ref, sem, zsem, *, tile):
    i = pl.program_id(0)
    tb = xn_ref.shape[0]

    def zero_copy(e):
        start = pl.multiple_of(tz_ref[e], tile)
        return pltpu.make_async_copy(zero_ref, xs_ref.at[pl.ds(start, tile), 0], zsem)

    @pl.when(i == 0)
    def _():
        zero_ref[...] = jnp.zeros_like(zero_ref)
        for e in range(N_EXPERTS):
            @pl.when(tz_ref[e] >= 0)
            def _():
                zero_copy(e).start()
        for e in range(N_EXPERTS):
            @pl.when(tz_ref[e] >= 0)
            def _():
                zero_copy(e).wait()

    def issue(t, carry):
        for kk in range(TOP_K):
            p = pos_ref[0, 0, t * TOP_K + kk]
            pltpu.make_async_copy(xn_ref.at[pl.ds(t, 1)], xs_ref.at[p], sem).start(priority=kk % 2)
        return carry

    lax.fori_loop(0, tb, issue, 0, unroll=ROW_DMA_UNROLL)
    for _ in range(TOP_K):
        pltpu.make_async_copy(xn_ref, xs_ref.at[pl.ds(0, tb), 0], sem).wait()


def _dispatch(tile_zero, pos3, xn, n_rows, tb, tile):
    t = xn.shape[0]
    return pl.pallas_call(
        functools.partial(_dispatch_kernel, tile=tile),
        grid_spec=pltpu.PrefetchScalarGridSpec(
            num_scalar_prefetch=1,
            grid=(t // tb,),
            in_specs=[
                pl.BlockSpec((1, 1, tb * TOP_K), lambda i, tz: (i, 0, 0), memory_space=pltpu.SMEM),
                pl.BlockSpec((tb, D_MODEL), lambda i, tz: (i, 0)),
            ],
            out_specs=pl.BlockSpec(memory_space=pl.ANY),
            scratch_shapes=[pltpu.VMEM((tile, D_MODEL), F32), pltpu.SemaphoreType.DMA,
                            pltpu.SemaphoreType.DMA],
        ),
        out_shape=jax.ShapeDtypeStruct((n_rows, 1, D_MODEL), F32),
        compiler_params=_params("arbitrary"),
        name="dispatch",
    )(tile_zero, pos3, xn)


def _experts_kernel(te_ref, nv_ref, xs_ref, xsc_ref, wg_ref, bg_ref, wu_ref, bu_ref, wd_ref, bd_ref,
                    ys_ref,
                    wgb_ref, wub_ref, wdb_ref, wsc_ref, xbuf, ybuf, isem, osem, *, tile):
    i = pl.program_id(0)
    nv = nv_ref[0]
    valid = i < nv
    slot = i % 2

    def in_copies(step, s):
        r0 = pl.multiple_of(step * tile, tile)
        return [pltpu.make_async_copy(xs_ref.at[pl.ds(r0, tile), 0, pl.ds(c * LANES, LANES)],
                                      xbuf.at[s, :, pl.ds(c * LANES, LANES)], isem.at[s])
                for c in range(LANE_CHUNKS)]

    def out_copies(step, s):
        r0 = pl.multiple_of(step * tile, tile)
        return [pltpu.make_async_copy(ybuf.at[s, :, pl.ds(c * LANES, LANES)],
                                      ys_ref.at[pl.ds(r0, tile), 0, pl.ds(c * LANES, LANES)],
                                      osem.at[s])
                for c in range(LANE_CHUNKS)]

    @pl.when(i == 0)
    def _():
        for cp in in_copies(0, 0):
            cp.start()

    @pl.when(i + 1 < nv)
    def _():
        for cp in in_copies(i + 1, 1 - slot):
            cp.start()

    new_expert = jnp.logical_or(i == 0, te_ref[i] != te_ref[jnp.maximum(i - 1, 0)])

    @pl.when(jnp.logical_and(valid, new_expert))
    def _():
        for slot_w, (w_ref, w8_ref) in enumerate(((wg_ref, wgb_ref), (wu_ref, wub_ref),
                                                  (wd_ref, wdb_ref))):
            w = w_ref[0]
            sc = _pow2_scale(_abs_max(w))
            w8_ref[...] = (w * sc).astype(F8)
            wsc_ref[slot_w:slot_w + 1, :] = jnp.broadcast_to(1.0 / sc, (1, LANES))

    @pl.when(valid)
    def _():
        for cp in in_copies(i, slot):
            cp.wait()

        @pl.when(i >= 2)
        def _():
            for cp in out_copies(i - 2, slot):
                cp.wait()

        x = xbuf[slot]
        sx = xsc_ref[0:1, 0:1]
        x8 = (x * sx).astype(F8)
        inv_x = 1.0 / sx
        g = _dot(x8, wgb_ref[...]) * (inv_x * wsc_ref[0:1, 0:1]) + bg_ref[0]
        u = _dot(x8, wub_ref[...]) * (inv_x * wsc_ref[1:2, 0:1]) + bu_ref[0]
        g = jnp.minimum(g, SWIGLU_LIMIT)
        u = jnp.clip(u, -SWIGLU_LIMIT, SWIGLU_LIMIT)
        hidden = (u + 1.0) * g * jax.nn.sigmoid(SWIGLU_ALPHA * g)
        h8 = (hidden * HIDDEN_SCALE).astype(F8)
        ybuf[slot] = (_dot(h8, wdb_ref[...]) * (wsc_ref[2:3, 0:1] * (1.0 / HIDDEN_SCALE))
                      + bd_ref[0])
        for cp in out_copies(i, slot):
            cp.start()

    @pl.when(i == nv - 1)
    def _():
        for cp in out_copies(i, slot):
            cp.wait()

        @pl.when(i >= 1)
        def _():
            for cp in out_copies(i - 1, 1 - slot):
                cp.wait()


def _experts(tile_expert, n_valid, xs, x_scale, wg, bg, wu, bu, wd, bd, tile):
    rows = xs.shape[0]
    nt = rows // tile
    wmap = lambda i, te, nv: (te[i], 0, 0)
    return pl.pallas_call(
        functools.partial(_experts_kernel, tile=tile),
        grid_spec=pltpu.PrefetchScalarGridSpec(
            num_scalar_prefetch=2,
            grid=(nt,),
            in_specs=[
                pl.BlockSpec(memory_space=pl.ANY),
                pl.BlockSpec((1, LANES), lambda i, te, nv: (0, 0)),
                pl.BlockSpec((1, D_MODEL, D_FF), wmap),
                pl.BlockSpec((1, 1, D_FF), wmap),
                pl.BlockSpec((1, D_MODEL, D_FF), wmap),
                pl.BlockSpec((1, 1, D_FF), wmap),
                pl.BlockSpec((1, D_FF, D_MODEL), wmap),
                pl.BlockSpec((1, 1, D_MODEL), wmap),
            ],
            out_specs=pl.BlockSpec(memory_space=pl.ANY),
            scratch_shapes=[pltpu.VMEM((D_MODEL, D_FF), F8), pltpu.VMEM((D_MODEL, D_FF), F8),
                            pltpu.VMEM((D_FF, D_MODEL), F8), pltpu.VMEM((8, LANES), F32),
                            pltpu.VMEM((2, tile, D_MODEL), F32), pltpu.VMEM((2, tile, D_MODEL), F32),
                            pltpu.SemaphoreType.DMA((2,)), pltpu.SemaphoreType.DMA((2,))],
        ),
        out_shape=jax.ShapeDtypeStruct((rows, 1, D_MODEL), F32),
        compiler_params=_params("arbitrary"),
        name="experts",
    )(tile_expert, n_valid, xs, x_scale, wg, bg, wu, bu, wd, bd)


def _combine_kernel(pos_ref, posn_ref, x1_ref, cw_ref, ys_ref, o_ref, buf_ref, sem0, sem1):
    i = pl.program_id(0)
    n = pl.num_programs(0)
    tb = x1_ref.shape[0] // 2
    def issue(p_ref, sub, slot, sem):
        def body(t, carry):
            for kk in range(TOP_K):
                p = p_ref[0, 0, (sub * tb + t) * TOP_K + kk]
                pltpu.make_async_copy(ys_ref.at[p], buf_ref.at[slot, kk, pl.ds(t, 1)],
                                      sem).start(priority=kk % 2)
            return carry
        lax.fori_loop(0, tb, body, 0, unroll=ROW_DMA_UNROLL)

    def drain(slot, sem):
        for kk in range(TOP_K):
            pltpu.make_async_copy(ys_ref.at[pl.ds(0, tb), 0], buf_ref.at[slot, kk], sem).wait()

    def reduce(sub, slot):
        rows = slice(sub * tb, (sub + 1) * tb)
        cw = cw_ref[rows, :]
        acc = x1_ref[rows, :]
        for kk in range(TOP_K):
            acc = acc + cw[:, kk:kk + 1] * buf_ref[slot, kk]
        o_ref[rows, :] = acc

    @pl.when(i == 0)
    def _():
        issue(pos_ref, 0, 0, sem0)

    issue(pos_ref, 1, 1, sem1)
    drain(0, sem0)
    reduce(0, 0)

    @pl.when(i + 1 < n)
    def _():
        issue(posn_ref, 0, 0, sem0)

    drain(1, sem1)
    reduce(1, 1)


def _combine(pos3, x1, cw, ys, tb):
    t = x1.shape[0]
    n = t // (2 * tb)
    return pl.pallas_call(
        _combine_kernel,
        grid=(n,),
        in_specs=[
            pl.BlockSpec((1, 1, 2 * tb * TOP_K), lambda i: (i, 0, 0), memory_space=pltpu.SMEM),
            pl.BlockSpec((1, 1, 2 * tb * TOP_K), lambda i: (jnp.minimum(i + 1, n - 1), 0, 0),
                         memory_space=pltpu.SMEM),
            pl.BlockSpec((2 * tb, D_MODEL), lambda i: (i, 0)),
            pl.BlockSpec((2 * tb, LANES), lambda i: (i, 0)),
            pl.BlockSpec(memory_space=pl.ANY),
        ],
        out_specs=pl.BlockSpec((2 * tb, D_MODEL), lambda i: (i, 0)),
        out_shape=jax.ShapeDtypeStruct((t, D_MODEL), F32),
        scratch_shapes=[pltpu.VMEM((2, TOP_K, tb, D_MODEL), F32), pltpu.SemaphoreType.DMA,
                        pltpu.SemaphoreType.DMA],
        compiler_params=_params("arbitrary"),
        name="combine",
    )(pos3, pos3, x1, cw, ys)


def _pad_cols(a, width):
    return jnp.pad(a, ((0, 0), (0, width - a.shape[1])))


def _split_in_proj(w_in):
    o = np.cumsum((GLA_QK_WIDTH, GLA_QK_WIDTH, GLA_V_WIDTH, GLA_V_WIDTH, GLA_GATE_RANK,
                   MLA_Q_RANK, MLA_KV_RANK, MLA_ROPE, D_MODEL, D_MODEL)).tolist()
    gla = w_in[:, 0:o[3]]
    lr = _pad_cols(w_in[:, o[3]:o[4]], _LR_W)
    mla = _pad_cols(w_in[:, o[4]:o[7]], _MLA_W)
    gates = w_in[:, o[7]:o[9]]
    return gla.astype(BF16), lr.astype(BF16), mla.astype(BF16), gates.astype(BF16)


def _pad_heads(w, head_w):
    r = w.shape[0]
    w3 = w.reshape(r, MLA_HEADS, head_w)
    w3 = jnp.pad(w3, ((0, 0), (0, 0), (0, MLA_HEAD_PAD - head_w)))
    return w3.reshape(r, MLA_HEADS * MLA_HEAD_PAD)


def _tile_rows(n, pref):
    return pref if n % pref == 0 else n


def kernel(x, positions, attn_norm_w, w_in, w_gla_gk, b_gla_gk, gla_out_norm_w, w_gla_out,
           mla_q_norm_w, w_mla_uq, mla_kv_norm_w, w_mla_ukv, mla_qk_q_norm_w, mla_qk_k_norm_w,
           w_mla_out, w_out, moe_norm_w, w_router, b_router, w_exp_gate, b_exp_gate,
           w_exp_up, b_exp_up, w_exp_down, b_exp_down):
    batch, seq, _ = x.shape
    depth = w_in.shape[0]
    t = batch * seq
    x2 = x.reshape(t, D_MODEL)
    pos = positions.reshape(1, t).astype(F32)
    half = MLA_ROPE // 2
    freq = (ROPE_BASE ** (-jnp.arange(half, dtype=F32) / half))[:, None]

    tm = _tile_rows(t, TOKEN_TILE)
    seq_tile = _tile_rows(seq, TOKEN_TILE)
    moe_tile = TOKEN_TILE
    n_rows = ((t * TOP_K) // moe_tile + N_EXPERTS) * moe_tile

    for l in range(depth):
        w_parts = _split_in_proj(w_in[l])
        wgk = jnp.pad(w_gla_gk[l], ((0, LANES - GLA_GATE_RANK), (0, 0)))
        wgk_hi, wgk_lo = _split_bf16(wgk)
        wuq = _pad_heads(w_mla_uq[l], MLA_QK).astype(BF16)
        wukv3 = w_mla_ukv[l].reshape(MLA_KV_RANK, MLA_HEADS, MLA_NOPE + MLA_V)
        wuk = wukv3[:, :, :MLA_NOPE].reshape(MLA_KV_RANK, MLA_HEADS * MLA_NOPE).astype(BF16)
        wvt = jnp.pad(wukv3[:, :, MLA_NOPE:].transpose(1, 2, 0),
                      ((0, 0), (0, FLASH_V_ROWS - MLA_V), (0, 0)))
        wvt = wvt.reshape(MLA_HEADS * FLASH_V_ROWS, MLA_KV_RANK).astype(BF16)
        vone = (jnp.arange(MLA_HEADS * FLASH_V_ROWS) % FLASH_V_ROWS == MLA_V).astype(F32)[:, None]
        qw = _pad_cols(mla_qk_q_norm_w[l][None, :], MLA_HEAD_PAD)
        kw = _pad_cols(mla_qk_k_norm_w[l][None, :], MLA_HEAD_PAD)
        o_a, gates, q, k, vt = _inproj(
            x2, attn_norm_w[l][None, :], w_parts, wgk_hi, wgk_lo, b_gla_gk[l][None, :], pos,
            wuq, wuk, wvt, vone, mla_q_norm_w[l][None, :], mla_kv_norm_w[l][None, :], qw, kw, freq,
            gla_out_norm_w[l][None, :], batch, seq, seq_tile)
        o_b = _flash(q, k, vt, batch, seq, _tile_rows(seq, FLASH_TILE))

        wr = _pad_cols(w_router[l], LANES)
        wr_hi, wr_lo = _split_bf16(wr)
        br = _pad_cols(b_router[l][None, :], LANES)
        x1, xn, eid, cw, cnt = _merge_route(
            x2, o_a, o_b, gates, w_gla_out[l].astype(BF16), w_mla_out[l].astype(BF16),
            w_out[l].astype(BF16), moe_norm_w[l][None, :], wr_hi, wr_lo, br, tm)

        pos3 = _positions(eid, cnt, tm, moe_tile)[:, :TOP_K].reshape(t // tm, 1, tm * TOP_K)
        counts = cnt[0, :N_EXPERTS].astype(I32)
        padded = ((counts + moe_tile - 1) // moe_tile) * moe_tile
        ends = jnp.cumsum(padded)
        n_tiles = n_rows // moe_tile
        n_valid = (ends[-1] // moe_tile).astype(I32)
        starts = jnp.arange(n_tiles, dtype=I32) * moe_tile
        starts = jnp.minimum(starts, ends[-1] - moe_tile)
        tile_expert = jnp.minimum(jnp.sum((starts[:, None] >= ends[None, :]).astype(I32), axis=1),
                                  N_EXPERTS - 1)
        tile_zero = jnp.where(padded > 0, ends - moe_tile, -1).astype(I32)

        xs = _dispatch(tile_zero, pos3, xn, n_rows, tm, moe_tile)
        x_bound = (D_MODEL ** 0.5) * jnp.max(jnp.abs(moe_norm_w[l]))
        x_scale = jnp.broadcast_to(_pow2_scale(x_bound), (1, LANES))
        ys = _experts(tile_expert, n_valid.reshape(1), xs, x_scale,
                      w_exp_gate[l], b_exp_gate[l][:, None, :],
                      w_exp_up[l], b_exp_up[l][:, None, :],
                      w_exp_down[l], b_exp_down[l][:, None, :], moe_tile)
        x2 = _combine(pos3, x1, cw, ys, tm // 2)
    return x2.reshape(batch, seq, D_MODEL)
```

```python
import functools

import jax
import jax.numpy as jnp
import numpy as np
from jax import lax
from jax.experimental import pallas as pl
from jax.experimental.pallas import tpu as pltpu

F32 = jnp.float32
BF16 = jnp.bfloat16
I32 = jnp.int32
F8 = jnp.float8_e4m3fn
F8_TARGET = 224.0
HIDDEN_SCALE = 4.0

D_MODEL = 1024
EPS = 1e-6
GLA_HEADS = 4
GLA_DK = 128
GLA_DV = 256
GLA_GATE_RANK = 16
GLA_GATE_NORMALIZER = 16.0
GLA_QK_WIDTH = GLA_HEADS * GLA_DK
GLA_V_WIDTH = GLA_HEADS * GLA_DV
MLA_HEADS = 8
MLA_Q_RANK = 384
MLA_KV_RANK = 256
MLA_NOPE = 128
MLA_ROPE = 64
MLA_V = 128
MLA_QK = MLA_NOPE + MLA_ROPE
ROPE_BASE = 10000.0
N_EXPERTS = 32
TOP_K = 4
D_FF = 1024
SWIGLU_LIMIT = 7.0
SWIGLU_ALPHA = 1.702

LANES = 128
MLA_HEAD_PAD = 256
GLA_CHUNK = 128
TOKEN_TILE = 512
FLASH_TILE = 1024
FLASH_V_ROWS = 144
LOG2_E = 1.4426950408889634
VMEM_LIMIT = 56 * 1024 * 1024

_GLA_W = 2 * GLA_QK_WIDTH + 2 * GLA_V_WIDTH
_LR_W = LANES
_MLA_W = 768
_GATE_W = 2 * D_MODEL


def _dot(a, b):
    return jnp.dot(a, b, preferred_element_type=F32)


def _dot_nt(a, b):
    return lax.dot_general(a, b, (((1,), (1,)), ((), ())), preferred_element_type=F32)


def _dot_tn(a, b):
    return lax.dot_general(a, b, (((0,), (0,)), ((), ())), preferred_element_type=F32)


def _abs_max(a):
    return jnp.max(jnp.max(jnp.abs(a), axis=-1, keepdims=True), axis=0, keepdims=True)


def _pow2_scale(amax):
    return jnp.exp2(jnp.floor(jnp.log2(F8_TARGET / jnp.maximum(amax, 1e-30))))


def _split_bf16(a):
    hi = a.astype(BF16)
    lo = (a - hi.astype(F32)).astype(BF16)
    return hi, lo


def _params(*sem):
    return pltpu.CompilerParams(dimension_semantics=sem, vmem_limit_bytes=VMEM_LIMIT)


LANE_CHUNKS = D_MODEL // LANES


def _inproj_kernel(x_ref, nw_ref, wgla_ref, wlr_ref, wmla_ref, wgate_ref, wgk_hi_ref, wgk_lo_ref,
                   bgk_ref, pos_ref,
                   wuq_ref, wuk_ref, wvt_ref, vone_ref, qnw_ref, kvnw_ref, qw_ref, kw_ref, freq_ref,
                   gnw_ref, oa_ref, gate_ref, q_ref, k_ref, vt_ref, state_ref, *, steps_per_seq):
    @pl.when(pl.program_id(0) % steps_per_seq == 0)
    def _():
        state_ref[...] = jnp.zeros_like(state_ref)

    x = x_ref[...]
    ms = jnp.mean(x * x, axis=-1, keepdims=True)
    h = (x * lax.rsqrt(ms + EPS)) * nw_ref[...]
    hb = h.astype(BF16)
    mla = _dot(hb, wmla_ref[...])
    _mla_prep(mla, pos_ref, wuq_ref, wuk_ref, wvt_ref, vone_ref, qnw_ref, kvnw_ref, qw_ref, kw_ref,
              freq_ref, q_ref, k_ref, vt_ref)
    gq = (_dot(hb, wgla_ref[:, 0:GLA_QK_WIDTH]) * (GLA_DK ** -0.5)).astype(BF16)
    gk = _dot(hb, wgla_ref[:, GLA_QK_WIDTH:2 * GLA_QK_WIDTH]).astype(BF16)
    gv = _dot(hb, wgla_ref[:, 2 * GLA_QK_WIDTH:2 * GLA_QK_WIDTH + GLA_V_WIDTH]).astype(BF16)
    gg = _dot(hb, wgla_ref[:, 2 * GLA_QK_WIDTH + GLA_V_WIDTH:_GLA_W])
    lr = _dot(hb, wlr_ref[...])
    lr_hi, lr_lo = _split_bf16(lr)
    a_logit = (_dot(lr_hi, wgk_hi_ref[...]) + _dot(lr_lo, wgk_hi_ref[...])
               + _dot(lr_hi, wgk_lo_ref[...])) + bgk_ref[...]
    log_sig = jnp.minimum(a_logit, 0.0) - jnp.log1p(jnp.exp(-jnp.abs(a_logit)))
    _gla_chunks(gq, gk, gv, gg, log_sig / GLA_GATE_NORMALIZER, gnw_ref, oa_ref, state_ref)
    gate_ref[...] = _dot(hb, wgate_ref[...]).astype(BF16)


def _inproj(x2, nw, w_parts, wgk_hi, wgk_lo, bgk, pos, wuq, wuk, wvt, vone, qnw, kvnw, qw, kw, freq,
            gnw, batch, seq, tm):
    t = x2.shape[0]
    nblk = seq // tm
    const = lambda i: (0, 0)
    row = lambda i: (i, 0)
    hw = MLA_HEADS * MLA_HEAD_PAD
    return pl.pallas_call(
        functools.partial(_inproj_kernel, steps_per_seq=nblk),
        grid=(t // tm,),
        in_specs=[
            pl.BlockSpec((tm, D_MODEL), row),
            pl.BlockSpec((1, D_MODEL), const),
            pl.BlockSpec((D_MODEL, _GLA_W), const, pipeline_mode=pl.Buffered(1)),
            pl.BlockSpec((D_MODEL, _LR_W), const, pipeline_mode=pl.Buffered(1)),
            pl.BlockSpec((D_MODEL, _MLA_W), const, pipeline_mode=pl.Buffered(1)),
            pl.BlockSpec((D_MODEL, _GATE_W), const, pipeline_mode=pl.Buffered(1)),
            pl.BlockSpec((LANES, GLA_QK_WIDTH), const),
            pl.BlockSpec((LANES, GLA_QK_WIDTH), const),
            pl.BlockSpec((1, GLA_QK_WIDTH), const),
            pl.BlockSpec((1, tm), lambda i: (0, i)),
            pl.BlockSpec((MLA_Q_RANK, hw), const),
            pl.BlockSpec((MLA_KV_RANK, MLA_HEADS * MLA_NOPE), const),
            pl.BlockSpec((MLA_HEADS * FLASH_V_ROWS, MLA_KV_RANK), const),
            pl.BlockSpec((MLA_HEADS * FLASH_V_ROWS, 1), const),
            pl.BlockSpec((1, MLA_Q_RANK), const),
            pl.BlockSpec((1, MLA_KV_RANK), const),
            pl.BlockSpec((1, MLA_HEAD_PAD), const),
            pl.BlockSpec((1, MLA_HEAD_PAD), const),
            pl.BlockSpec((MLA_ROPE // 2, 1), const),
            pl.BlockSpec((1, GLA_DV), const),
        ],
        out_specs=[
            pl.BlockSpec((tm, GLA_V_WIDTH), row),
            pl.BlockSpec((tm, _GATE_W), row),
            pl.BlockSpec((tm, hw), row),
            pl.BlockSpec((tm, hw), row),
            pl.BlockSpec((1, MLA_HEADS * FLASH_V_ROWS, tm), lambda i: (i // nblk, 0, i % nblk)),
        ],
        out_shape=[
            jax.ShapeDtypeStruct((t, GLA_V_WIDTH), BF16),
            jax.ShapeDtypeStruct((t, _GATE_W), BF16),
            jax.ShapeDtypeStruct((t, hw), BF16),
            jax.ShapeDtypeStruct((t, hw), BF16),
            jax.ShapeDtypeStruct((batch, MLA_HEADS * FLASH_V_ROWS, seq), BF16),
        ],
        scratch_shapes=[pltpu.VMEM((GLA_HEADS, GLA_DV, GLA_DK), F32)],
        compiler_params=_params("arbitrary"),
        name="inproj",
    )(x2, nw, *w_parts, wgk_hi, wgk_lo, bgk, pos, wuq, wuk, wvt, vone, qnw, kvnw, qw, kw, freq, gnw)


def _gla_chunks(q_all, k_all, v_all, g_all, la_all, nw_ref, o_ref, state_ref):
    c = GLA_CHUNK
    row = lax.broadcasted_iota(I32, (c, c), 0)
    col = lax.broadcasted_iota(I32, (c, c), 1)
    causal = row >= col
    tri = causal.astype(BF16)
    mid = c // 2 - 1
    for s in range(q_all.shape[0] // c):
        r0 = s * c
        la = la_all[r0:r0 + c, :]
        la_hi, la_lo = _split_bf16(la)
        cum_all = _dot(tri, la_hi) + _dot(tri, la_lo)
        for h in range(GLA_HEADS):
            ks = slice(h * GLA_DK, (h + 1) * GLA_DK)
            vs = slice(h * GLA_DV, (h + 1) * GLA_DV)
            cum = cum_all[:, ks]
            q = q_all[r0:r0 + c, ks].astype(F32)
            k = k_all[r0:r0 + c, ks].astype(F32)
            v = v_all[r0:r0 + c, vs]
            ref_row = cum[mid:mid + 1, :]
            last = cum[c - 1:c, :]
            qg = (q * jnp.exp(cum - ref_row)).astype(BF16)
            kg = (k * jnp.exp(ref_row - cum)).astype(BF16)
            scores = jnp.where(causal, _dot_nt(qg, kg), 0.0).astype(BF16)
            o = _dot(scores, v)
            st = state_ref[h]
            qe = (q * jnp.exp(cum)).astype(BF16)
            o = o + _dot_nt(qe, st.astype(BF16))
            ko = (k * jnp.exp(last - cum)).astype(BF16)
            state_ref[h] = st * jnp.exp(last) + _dot_tn(v, ko)
            ms = jnp.mean(o * o, axis=-1, keepdims=True)
            on = (o * lax.rsqrt(ms + EPS)) * nw_ref[...]
            g = g_all[r0:r0 + c, vs]
            o_ref[r0:r0 + c, vs] = (on * (g * jax.nn.sigmoid(g))).astype(BF16)


def _rope(x, cos, sin, lane):
    half = MLA_ROPE // 2
    rot = jnp.where(lane < half, -pltpu.roll(x, LANES - half, axis=1), pltpu.roll(x, half, axis=1))
    return x * cos + rot * sin


def _mla_prep(mla, pos_ref, wuq_ref, wuk_ref, wvt_ref, vone_ref, qnw_ref, kvnw_ref,
              qw_ref, kw_ref, freq_ref, q_ref, k_ref, vt_ref):
    tm = mla.shape[0]
    cq = mla[:, 0:MLA_Q_RANK]
    ckv = mla[:, MLA_Q_RANK:MLA_Q_RANK + MLA_KV_RANK]
    kr = mla[:, MLA_Q_RANK + MLA_KV_RANK:_MLA_W]

    def rms(a, w):
        ms = jnp.mean(a * a, axis=-1, keepdims=True)
        return (a * lax.rsqrt(ms + EPS)) * w

    q_all = _dot(rms(cq, qnw_ref[...]).astype(BF16), wuq_ref[...])
    ckvn = rms(ckv, kvnw_ref[...]).astype(BF16)
    k_all = _dot(ckvn, wuk_ref[...])
    vt_ref[0] = (_dot_nt(wvt_ref[...], ckvn) + vone_ref[:, 0:1]).astype(BF16)

    half = MLA_ROPE // 2
    ang_t = freq_ref[...] * pos_ref[...]
    cos_t = jnp.cos(ang_t)
    sin_t = jnp.sin(ang_t)
    cos = jnp.concatenate([cos_t, cos_t, jnp.ones((LANES - 2 * half, tm), F32)], axis=0).T
    sin = jnp.concatenate([sin_t, sin_t, jnp.zeros((LANES - 2 * half, tm), F32)], axis=0).T
    lane = lax.broadcasted_iota(I32, (tm, LANES), 1)
    qw_nope = qw_ref[:, 0:MLA_NOPE]
    qw_rope = qw_ref[:, MLA_NOPE:MLA_HEAD_PAD]
    kw_nope = kw_ref[:, 0:MLA_NOPE]
    kw_rope = kw_ref[:, MLA_NOPE:MLA_HEAD_PAD]
    kr_sq = kr * kr
    kr_rot = _rope(kr * kw_rope, cos, sin, lane)
    scale = (MLA_QK ** -0.5) * LOG2_E
    for h in range(MLA_HEADS):
        base = h * MLA_HEAD_PAD
        qn = q_all[:, base:base + MLA_NOPE]
        qr = q_all[:, base + MLA_NOPE:base + MLA_HEAD_PAD]
        ss = jnp.sum(qn * qn + qr * qr, axis=-1, keepdims=True)
        r = lax.rsqrt(ss / MLA_QK + EPS)
        q_ref[:, base:base + MLA_NOPE] = ((qn * r) * qw_nope * scale).astype(BF16)
        q_ref[:, base + MLA_NOPE:base + MLA_HEAD_PAD] = (
            _rope((qr * r) * qw_rope, cos, sin, lane) * scale).astype(BF16)
        kn = k_all[:, h * MLA_NOPE:(h + 1) * MLA_NOPE]
        ssk = jnp.sum(kn * kn + kr_sq, axis=-1, keepdims=True)
        rk = lax.rsqrt(ssk / MLA_QK + EPS)
        k_ref[:, base:base + MLA_NOPE] = ((kn * rk) * kw_nope).astype(BF16)
        k_ref[:, base + MLA_NOPE:base + MLA_HEAD_PAD] = (kr_rot * rk).astype(BF16)


def _flash_kernel(q_ref, k_ref, vt_ref, o_ref, s_ref, cm_ref, acc_ref, m_ref, *, t):
    seq = q_ref.shape[0]
    nq = seq // t
    krow = lax.broadcasted_iota(I32, (t, t), 0)
    qcol = lax.broadcasted_iota(I32, (t, t), 1)
    items = [(qi, j) for qi in range(nq) for j in range(qi + 1)]

    def scores(idx):
        qi, j = items[idx]
        s = _dot_nt(k_ref[j * t:(j + 1) * t, :], q_ref[qi * t:(qi + 1) * t, :])
        s_ref[idx % 2] = s
        cm_ref[idx % 2] = jnp.max(s, axis=0, keepdims=True)

    def update(idx):
        qi, j = items[idx]
        if j == 0:
            m_ref[...] = jnp.full_like(m_ref, -jnp.inf)
            acc_ref[...] = jnp.zeros_like(acc_ref)
        s = s_ref[idx % 2]
        if j == qi:
            s = jnp.where(krow <= qcol, s, -jnp.inf)
            cm = jnp.max(s, axis=0, keepdims=True)
        else:
            cm = cm_ref[idx % 2]
        m_prev = m_ref[...]
        m_new = jnp.maximum(m_prev, cm)
        alpha = jnp.exp2(m_prev - m_new)
        p = jnp.exp2(s - m_new).astype(BF16)
        m_ref[...] = m_new
        acc_ref[...] = alpha * acc_ref[...] + _dot(vt_ref[0, :, j * t:(j + 1) * t], p)
        if j == qi:
            acc = acc_ref[...]
            out_t = acc[0:MLA_V, :] * (1.0 / acc[MLA_V:MLA_V + 1, :])
            o_ref[qi * t:(qi + 1) * t, :] = out_t.T.astype(BF16)

    scores(0)
    for idx in range(len(items)):
        if idx + 1 < len(items):
            scores(idx + 1)
        update(idx)


def _flash(q, k, vt, batch, seq, t):
    rows = q.shape[0]
    return pl.pallas_call(
        functools.partial(_flash_kernel, t=t),
        grid=(batch, MLA_HEADS),
        in_specs=[
            pl.BlockSpec((seq, MLA_HEAD_PAD), lambda b, h: (b, h)),
            pl.BlockSpec((seq, MLA_HEAD_PAD), lambda b, h: (b, h)),
            pl.BlockSpec((1, FLASH_V_ROWS, seq), lambda b, h: (b, h, 0)),
        ],
        out_specs=pl.BlockSpec((seq, MLA_V), lambda b, h: (b, h)),
        out_shape=jax.ShapeDtypeStruct((rows, MLA_HEADS * MLA_V), BF16),
        scratch_shapes=[
            pltpu.VMEM((2, t, t), F32),
            pltpu.VMEM((2, 1, t), F32),
            pltpu.VMEM((FLASH_V_ROWS, t), F32),
            pltpu.VMEM((1, t), F32),
        ],
        compiler_params=_params("parallel", "parallel"),
        name="flash",
    )(q, k, vt)


def _merge_route_kernel(x_ref, oa_ref, ob_ref, gate_ref, wa_ref, wb_ref, wo_ref, nw_ref,
                        wr_hi_ref, wr_lo_ref, br_ref,
                        x1_ref, xn_ref, eid_ref, cw_ref, cnt_ref):
    @pl.when(pl.program_id(0) == 0)
    def _():
        cnt_ref[...] = jnp.zeros_like(cnt_ref)

    tm = x_ref.shape[0]
    ya = _dot(oa_ref[...], wa_ref[...])
    yb = _dot(ob_ref[...], wb_ref[...])
    ga = gate_ref[:, 0:D_MODEL].astype(F32)
    gb = gate_ref[:, D_MODEL:2 * D_MODEL].astype(F32)
    merged = jax.nn.sigmoid(ga) * ya + jax.nn.sigmoid(gb) * yb
    x1 = x_ref[...] + _dot(merged.astype(BF16), wo_ref[...])
    x1_ref[...] = x1
    ms = jnp.mean(x1 * x1, axis=-1, keepdims=True)
    xn = (x1 * lax.rsqrt(ms + EPS)) * nw_ref[...]
    xn_ref[...] = xn
    xn_hi, xn_lo = _split_bf16(xn)
    logits = (_dot(xn_hi, wr_hi_ref[...]) + _dot(xn_lo, wr_hi_ref[...])
              + _dot(xn_hi, wr_lo_ref[...])) + br_ref[...]
    lane = lax.broadcasted_iota(I32, (tm, LANES), 1)
    work = jnp.where(lane < N_EXPERTS, logits, -jnp.inf)
    vals, idxs = [], []
    for _ in range(TOP_K):
        m = jnp.max(work, axis=-1, keepdims=True)
        idx = jnp.min(jnp.where(work == m, lane, LANES), axis=-1, keepdims=True)
        vals.append(m)
        idxs.append(idx)
        work = jnp.where(lane == idx, -jnp.inf, work)
    exps = [jnp.exp(v - vals[0]) for v in vals]
    denom = exps[0] + exps[1] + exps[2] + exps[3]
    eid = jnp.zeros((tm, LANES), I32)
    cw = jnp.zeros((tm, LANES), F32)
    sel = jnp.zeros((tm, LANES), F32)
    for kk in range(TOP_K):
        eid = jnp.where(lane == kk, idxs[kk], eid)
        cw = jnp.where(lane == kk, exps[kk] / denom, cw)
        sel = sel + (lane == idxs[kk]).astype(F32)
    eid_ref[...] = eid
    cw_ref[...] = cw
    cnt_ref[0:1, :] = cnt_ref[0:1, :] + jnp.sum(sel, axis=0, keepdims=True)


def _merge_route(x2, oa, ob, gates, wa, wb, wo, nw, wr_hi, wr_lo, br, tm):
    t = x2.shape[0]
    const = lambda i: (0, 0)
    row = lambda i: (i, 0)
    return pl.pallas_call(
        _merge_route_kernel,
        grid=(t // tm,),
        in_specs=[
            pl.BlockSpec((tm, D_MODEL), row),
            pl.BlockSpec((tm, GLA_V_WIDTH), row),
            pl.BlockSpec((tm, MLA_HEADS * MLA_V), row),
            pl.BlockSpec((tm, _GATE_W), row),
            pl.BlockSpec((GLA_V_WIDTH, D_MODEL), const),
            pl.BlockSpec((MLA_HEADS * MLA_V, D_MODEL), const),
            pl.BlockSpec((D_MODEL, D_MODEL), const),
            pl.BlockSpec((1, D_MODEL), const),
            pl.BlockSpec((D_MODEL, LANES), const),
            pl.BlockSpec((D_MODEL, LANES), const),
            pl.BlockSpec((1, LANES), const),
        ],
        out_specs=[
            pl.BlockSpec((tm, D_MODEL), row),
            pl.BlockSpec((tm, D_MODEL), row),
            pl.BlockSpec((tm, LANES), row),
            pl.BlockSpec((tm, LANES), row),
            pl.BlockSpec((8, LANES), const),
        ],
        out_shape=[
            jax.ShapeDtypeStruct((t, D_MODEL), F32),
            jax.ShapeDtypeStruct((t, D_MODEL), F32),
            jax.ShapeDtypeStruct((t, LANES), I32),
            jax.ShapeDtypeStruct((t, LANES), F32),
            jax.ShapeDtypeStruct((8, LANES), F32),
        ],
        compiler_params=_params("arbitrary"),
        name="merge_route",
    )(x2, oa, ob, gates, wa, wb, wo, nw, wr_hi, wr_lo, br)


def _positions_kernel(eid_ref, cnt_ref, pos_ref, carry_ref, *, tile):
    @pl.when(pl.program_id(0) == 0)
    def _():
        carry_ref[...] = jnp.zeros_like(carry_ref)

    tb = eid_ref.shape[0]
    lane1 = lax.broadcasted_iota(I32, (1, LANES), 1)
    cnt = cnt_ref[0:1, :]
    padded = jnp.floor((cnt + (tile - 1)) / tile) * tile
    incl = padded
    shift = 1
    while shift < N_EXPERTS:
        incl = incl + jnp.where(lane1 >= shift, pltpu.roll(incl, shift, axis=1), 0.0)
        shift *= 2
    offs = incl - padded

    lane = lax.broadcasted_iota(I32, (tb, LANES), 1)
    eid = eid_ref[...]
    onehots = [lane == jnp.broadcast_to(eid[:, kk:kk + 1], (tb, LANES)) for kk in range(TOP_K)]
    sel = jnp.zeros((tb, LANES), F32)
    for oh in onehots:
        sel = sel + oh.astype(F32)
    row = lax.broadcasted_iota(I32, (tb, tb), 0)
    col = lax.broadcasted_iota(I32, (tb, tb), 1)
    strict = (row > col).astype(BF16)
    rank = _dot(strict, sel.astype(BF16)) + carry_ref[...] + offs
    pos = jnp.zeros((tb, LANES), I32)
    for kk in range(TOP_K):
        pk = jnp.sum(jnp.where(onehots[kk], rank, 0.0), axis=-1, keepdims=True)
        pos = jnp.where(lane == kk, pk.astype(I32), pos)
    pos_ref[...] = pos
    carry_ref[...] = carry_ref[...] + jnp.sum(sel, axis=0, keepdims=True)


def _positions(eid, cnt, tb, tile):
    t = eid.shape[0]
    return pl.pallas_call(
        functools.partial(_positions_kernel, tile=tile),
        grid=(t // tb,),
        in_specs=[
            pl.BlockSpec((tb, LANES), lambda i: (i, 0)),
            pl.BlockSpec((8, LANES), lambda i: (0, 0)),
        ],
        out_specs=pl.BlockSpec((tb, LANES), lambda i: (i, 0)),
        out_shape=jax.ShapeDtypeStruct((t, LANES), I32),
        scratch_shapes=[pltpu.VMEM((1, LANES), F32)],
        compiler_params=_params("arbitrary"),
        name="positions",
    )(eid, cnt)


ROW_DMA_UNROLL = 8


def _dispatch_kernel(tz_ref, pos_ref, xn_ref, xs_ref, zero_ref, sem, zsem, *, tile):
    i = pl.program_id(0)
    tb = xn_ref.shape[0]

    def zero_copy(e):
        start = pl.multiple_of(tz_ref[e], tile)
        return pltpu.make_async_copy(zero_ref, xs_ref.at[pl.ds(start, tile), 0], zsem)

    @pl.when(i == 0)
    def _():
        zero_ref[...] = jnp.zeros_like(zero_ref)
        for e in range(N_EXPERTS):
            @pl.when(tz_ref[e] >= 0)
            def _():
                zero_copy(e).start()
        for e in range(N_EXPERTS):
            @pl.when(tz_ref[e] >= 0)
            def _():
                zero_copy(e).wait()

    def issue(t, carry):
        for kk in range(TOP_K):
            p = pos_ref[0, 0, t * TOP_K + kk]
            pltpu.make_async_copy(xn_ref.at[pl.ds(t, 1)], xs_ref.at[p], sem).start(priority=kk % 2)
        return carry

    lax.fori_loop(0, tb, issue, 0, unroll=ROW_DMA_UNROLL)
    for _ in range(TOP_K):
        pltpu.make_async_copy(xn_ref, xs_ref.at[pl.ds(0, tb), 0], sem).wait()


def _dispatch(tile_zero, pos3, xn, n_rows, tb, tile):
    t = xn.shape[0]
    return pl.pallas_call(
        functools.partial(_dispatch_kernel, tile=tile),
        grid_spec=pltpu.PrefetchScalarGridSpec(
            num_scalar_prefetch=1,
            grid=(t // tb,),
            in_specs=[
                pl.BlockSpec((1, 1, tb * TOP_K), lambda i, tz: (i, 0, 0), memory_space=pltpu.SMEM),
                pl.BlockSpec((tb, D_MODEL), lambda i, tz: (i, 0)),
            ],
            out_specs=pl.BlockSpec(memory_space=pl.ANY),
            scratch_shapes=[pltpu.VMEM((tile, D_MODEL), F32), pltpu.SemaphoreType.DMA,
                            pltpu.SemaphoreType.DMA],
        ),
        out_shape=jax.ShapeDtypeStruct((n_rows, 1, D_MODEL), F32),
        compiler_params=_params("arbitrary"),
        name="dispatch",
    )(tile_zero, pos3, xn)


def _experts_kernel(te_ref, nv_ref, xs_ref, xsc_ref, wg_ref, bg_ref, wu_ref, bu_ref, wd_ref, bd_ref,
                    ys_ref,
                    wgb_ref, wub_ref, wdb_ref, wsc_ref, xbuf, ybuf, isem, osem, *, tile):
    i = pl.program_id(0)
    nv = nv_ref[0]
    valid = i < nv
    slot = i % 2

    def in_copies(step, s):
        r0 = pl.multiple_of(step * tile, tile)
        return [pltpu.make_async_copy(xs_ref.at[pl.ds(r0, tile), 0, pl.ds(c * LANES, LANES)],
                                      xbuf.at[s, :, pl.ds(c * LANES, LANES)], isem.at[s])
                for c in range(LANE_CHUNKS)]

    def out_copies(step, s):
        r0 = pl.multiple_of(step * tile, tile)
        return [pltpu.make_async_copy(ybuf.at[s, :, pl.ds(c * LANES, LANES)],
                                      ys_ref.at[pl.ds(r0, tile), 0, pl.ds(c * LANES, LANES)],
                                      osem.at[s])
                for c in range(LANE_CHUNKS)]

    @pl.when(i == 0)
    def _():
        for cp in in_copies(0, 0):
            cp.start()

    @pl.when(i + 1 < nv)
    def _():
        for cp in in_copies(i + 1, 1 - slot):
            cp.start()

    new_expert = jnp.logical_or(i == 0, te_ref[i] != te_ref[jnp.maximum(i - 1, 0)])

    @pl.when(jnp.logical_and(valid, new_expert))
    def _():
        for slot_w, (w_ref, w8_ref) in enumerate(((wg_ref, wgb_ref), (wu_ref, wub_ref),
                                                  (wd_ref, wdb_ref))):
            w = w_ref[0]
            sc = _pow2_scale(_abs_max(w))
            w8_ref[...] = (w * sc).astype(F8)
            wsc_ref[slot_w:slot_w + 1, :] = jnp.broadcast_to(1.0 / sc, (1, LANES))

    @pl.when(valid)
    def _():
        for cp in in_copies(i, slot):
            cp.wait()

        @pl.when(i >= 2)
        def _():
            for cp in out_copies(i - 2, slot):
                cp.wait()

        x = xbuf[slot]
        sx = xsc_ref[0:1, 0:1]
        x8 = (x * sx).astype(F8)
        inv_x = 1.0 / sx
        y = None
        for f0 in range(0, D_FF, D_FF // 2):
            fs = slice(f0, f0 + D_FF // 2)
            g = _dot(x8, wgb_ref[:, fs]) * (inv_x * wsc_ref[0:1, 0:1]) + bg_ref[0, :, fs]
            u = _dot(x8, wub_ref[:, fs]) * (inv_x * wsc_ref[1:2, 0:1]) + bu_ref[0, :, fs]
            g = jnp.minimum(g, SWIGLU_LIMIT)
            u = jnp.clip(u, -SWIGLU_LIMIT, SWIGLU_LIMIT)
            hidden = (u + 1.0) * g * jax.nn.sigmoid(SWIGLU_ALPHA * g)
            h8 = (hidden * HIDDEN_SCALE).astype(F8)
            part = _dot(h8, wdb_ref[fs, :])
            y = part if y is None else y + part
        ybuf[slot] = y * (wsc_ref[2:3, 0:1] * (1.0 / HIDDEN_SCALE)) + bd_ref[0]
        for cp in out_copies(i, slot):
            cp.start()

    @pl.when(i == nv - 1)
    def _():
        for cp in out_copies(i, slot):
            cp.wait()

        @pl.when(i >= 1)
        def _():
            for cp in out_copies(i - 1, 1 - slot):
                cp.wait()


def _experts(tile_expert, n_valid, xs, x_scale, wg, bg, wu, bu, wd, bd, tile):
    rows = xs.shape[0]
    nt = rows // tile
    wmap = lambda i, te, nv: (te[i], 0, 0)
    return pl.pallas_call(
        functools.partial(_experts_kernel, tile=tile),
        grid_spec=pltpu.PrefetchScalarGridSpec(
            num_scalar_prefetch=2,
            grid=(nt,),
            in_specs=[
                pl.BlockSpec(memory_space=pl.ANY),
                pl.BlockSpec((1, LANES), lambda i, te, nv: (0, 0)),
                pl.BlockSpec((1, D_MODEL, D_FF), wmap),
                pl.BlockSpec((1, 1, D_FF), wmap),
                pl.BlockSpec((1, D_MODEL, D_FF), wmap),
                pl.BlockSpec((1, 1, D_FF), wmap),
                pl.BlockSpec((1, D_FF, D_MODEL), wmap),
                pl.BlockSpec((1, 1, D_MODEL), wmap),
            ],
            out_specs=pl.BlockSpec(memory_space=pl.ANY),
            scratch_shapes=[pltpu.VMEM((D_MODEL, D_FF), F8), pltpu.VMEM((D_MODEL, D_FF), F8),
                            pltpu.VMEM((D_FF, D_MODEL), F8), pltpu.VMEM((8, LANES), F32),
                            pltpu.VMEM((2, tile, D_MODEL), F32), pltpu.VMEM((2, tile, D_MODEL), F32),
                            pltpu.SemaphoreType.DMA((2,)), pltpu.SemaphoreType.DMA((2,))],
        ),
        out_shape=jax.ShapeDtypeStruct((rows, 1, D_MODEL), F32),
        compiler_params=_params("arbitrary"),
        name="experts",
    )(tile_expert, n_valid, xs, x_scale, wg, bg, wu, bu, wd, bd)


def _combine_kernel(pos_ref, posn_ref, x1_ref, cw_ref, ys_ref, o_ref, buf_ref, sem0, sem1):
    i = pl.program_id(0)
    n = pl.num_programs(0)
    tb = x1_ref.shape[0] // 2
    def issue(p_ref, sub, slot, sem):
        def body(t, carry):
            for kk in range(TOP_K):
                p = p_ref[0, 0, (sub * tb + t) * TOP_K + kk]
                pltpu.make_async_copy(ys_ref.at[p], buf_ref.at[slot, kk, pl.ds(t, 1)],
                                      sem).start(priority=kk % 2)
            return carry
        lax.fori_loop(0, tb, body, 0, unroll=ROW_DMA_UNROLL)

    def drain(slot, sem):
        for kk in range(TOP_K):
            pltpu.make_async_copy(ys_ref.at[pl.ds(0, tb), 0], buf_ref.at[slot, kk], sem).wait()

    def reduce(sub, slot):
        rows = slice(sub * tb, (sub + 1) * tb)
        cw = cw_ref[rows, :]
        acc = x1_ref[rows, :]
        for kk in range(TOP_K):
            acc = acc + cw[:, kk:kk + 1] * buf_ref[slot, kk]
        o_ref[rows, :] = acc

    @pl.when(i == 0)
    def _():
        issue(pos_ref, 0, 0, sem0)

    issue(pos_ref, 1, 1, sem1)
    drain(0, sem0)
    reduce(0, 0)

    @pl.when(i + 1 < n)
    def _():
        issue(posn_ref, 0, 0, sem0)

    drain(1, sem1)
    reduce(1, 1)


def _combine(pos3, x1, cw, ys, tb):
    t = x1.shape[0]
    n = t // (2 * tb)
    return pl.pallas_call(
        _combine_kernel,
        grid=(n,),
        in_specs=[
            pl.BlockSpec((1, 1, 2 * tb * TOP_K), lambda i: (i, 0, 0), memory_space=pltpu.SMEM),
            pl.BlockSpec((1, 1, 2 * tb * TOP_K), lambda i: (jnp.minimum(i + 1, n - 1), 0, 0),
                         memory_space=pltpu.SMEM),
            pl.BlockSpec((2 * tb, D_MODEL), lambda i: (i, 0)),
            pl.BlockSpec((2 * tb, LANES), lambda i: (i, 0)),
            pl.BlockSpec(memory_space=pl.ANY),
        ],
        out_specs=pl.BlockSpec((2 * tb, D_MODEL), lambda i: (i, 0)),
        out_shape=jax.ShapeDtypeStruct((t, D_MODEL), F32),
        scratch_shapes=[pltpu.VMEM((2, TOP_K, tb, D_MODEL), F32), pltpu.SemaphoreType.DMA,
                        pltpu.SemaphoreType.DMA],
        compiler_params=_params("arbitrary"),
        name="combine",
    )(pos3, pos3, x1, cw, ys)


def _pad_cols(a, width):
    return jnp.pad(a, ((0, 0), (0, width - a.shape[1])))


def _split_in_proj(w_in):
    o = np.cumsum((GLA_QK_WIDTH, GLA_QK_WIDTH, GLA_V_WIDTH, GLA_V_WIDTH, GLA_GATE_RANK,
                   MLA_Q_RANK, MLA_KV_RANK, MLA_ROPE, D_MODEL, D_MODEL)).tolist()
    gla = w_in[:, 0:o[3]]
    lr = _pad_cols(w_in[:, o[3]:o[4]], _LR_W)
    mla = _pad_cols(w_in[:, o[4]:o[7]], _MLA_W)
    gates = w_in[:, o[7]:o[9]]
    return gla.astype(BF16), lr.astype(BF16), mla.astype(BF16), gates.astype(BF16)


def _pad_heads(w, head_w):
    r = w.shape[0]
    w3 = w.reshape(r, MLA_HEADS, head_w)
    w3 = jnp.pad(w3, ((0, 0), (0, 0), (0, MLA_HEAD_PAD - head_w)))
    return w3.reshape(r, MLA_HEADS * MLA_HEAD_PAD)


def _tile_rows(n, pref):
    return pref if n % pref == 0 else n


def kernel(x, positions, attn_norm_w, w_in, w_gla_gk, b_gla_gk, gla_out_norm_w, w_gla_out,
           mla_q_norm_w, w_mla_uq, mla_kv_norm_w, w_mla_ukv, mla_qk_q_norm_w, mla_qk_k_norm_w,
           w_mla_out, w_out, moe_norm_w, w_router, b_router, w_exp_gate, b_exp_gate,
           w_exp_up, b_exp_up, w_exp_down, b_exp_down):
    batch, seq, _ = x.shape
    depth = w_in.shape[0]
    t = batch * seq
    x2 = x.reshape(t, D_MODEL)
    pos = positions.reshape(1, t).astype(F32)
    half = MLA_ROPE // 2
    freq = (ROPE_BASE ** (-jnp.arange(half, dtype=F32) / half))[:, None]

    tm = _tile_rows(t, TOKEN_TILE)
    seq_tile = _tile_rows(seq, TOKEN_TILE)
    moe_tile = TOKEN_TILE
    n_rows = ((t * TOP_K) // moe_tile + N_EXPERTS) * moe_tile

    for l in range(depth):
        w_parts = _split_in_proj(w_in[l])
        wgk = jnp.pad(w_gla_gk[l], ((0, LANES - GLA_GATE_RANK), (0, 0)))
        wgk_hi, wgk_lo = _split_bf16(wgk)
        wuq = _pad_heads(w_mla_uq[l], MLA_QK).astype(BF16)
        wukv3 = w_mla_ukv[l].reshape(MLA_KV_RANK, MLA_HEADS, MLA_NOPE + MLA_V)
        wuk = wukv3[:, :, :MLA_NOPE].reshape(MLA_KV_RANK, MLA_HEADS * MLA_NOPE).astype(BF16)
        wvt = jnp.pad(wukv3[:, :, MLA_NOPE:].transpose(1, 2, 0),
                      ((0, 0), (0, FLASH_V_ROWS - MLA_V), (0, 0)))
        wvt = wvt.reshape(MLA_HEADS * FLASH_V_ROWS, MLA_KV_RANK).astype(BF16)
        vone = (jnp.arange(MLA_HEADS * FLASH_V_ROWS) % FLASH_V_ROWS == MLA_V).astype(F32)[:, None]
        qw = _pad_cols(mla_qk_q_norm_w[l][None, :], MLA_HEAD_PAD)
        kw = _pad_cols(mla_qk_k_norm_w[l][None, :], MLA_HEAD_PAD)
        o_a, gates, q, k, vt = _inproj(
            x2, attn_norm_w[l][None, :], w_parts, wgk_hi, wgk_lo, b_gla_gk[l][None, :], pos,
            wuq, wuk, wvt, vone, mla_q_norm_w[l][None, :], mla_kv_norm_w[l][None, :], qw, kw, freq,
            gla_out_norm_w[l][None, :], batch, seq, seq_tile)
        o_b = _flash(q, k, vt, batch, seq, _tile_rows(seq, FLASH_TILE))

        wr = _pad_cols(w_router[l], LANES)
        wr_hi, wr_lo = _split_bf16(wr)
        br = _pad_cols(b_router[l][None, :], LANES)
        x1, xn, eid, cw, cnt = _merge_route(
            x2, o_a, o_b, gates, w_gla_out[l].astype(BF16), w_mla_out[l].astype(BF16),
            w_out[l].astype(BF16), moe_norm_w[l][None, :], wr_hi, wr_lo, br, tm)

        pos3 = _positions(eid, cnt, tm, moe_tile)[:, :TOP_K].reshape(t // tm, 1, tm * TOP_K)
        counts = cnt[0, :N_EXPERTS].astype(I32)
        padded = ((counts + moe_tile - 1) // moe_tile) * moe_tile
        ends = jnp.cumsum(padded)
        n_tiles = n_rows // moe_tile
        n_valid = (ends[-1] // moe_tile).astype(I32)
        starts = jnp.arange(n_tiles, dtype=I32) * moe_tile
        starts = jnp.minimum(starts, ends[-1] - moe_tile)
        tile_expert = jnp.minimum(jnp.sum((starts[:, None] >= ends[None, :]).astype(I32), axis=1),
                                  N_EXPERTS - 1)
        tile_zero = jnp.where(padded > 0, ends - moe_tile, -1).astype(I32)

        xs = _dispatch(tile_zero, pos3, xn, n_rows, tm, moe_tile)
        x_bound = (D_MODEL ** 0.5) * jnp.max(jnp.abs(moe_norm_w[l]))
        x_scale = jnp.broadcast_to(_pow2_scale(x_bound), (1, LANES))
        ys = _experts(tile_expert, n_valid.reshape(1), xs, x_scale,
                      w_exp_gate[l], b_exp_gate[l][:, None, :],
                      w_exp_up[l], b_exp_up[l][:, None, :],
                      w_exp_down[l], b_exp_down[l][:, None, :], moe_tile)
        x2 = _combine(pos3, x1, cw, ys, tm // 2)
    return x2.reshape(batch, seq, D_MODEL)
```
